```python
import jax, jax.numpy as jnp
from jax import lax
import numpy as np

D_MODEL = 1024
BATCH = 8
SEQ = 2048
DEPTH = 1

GRID_W = 64
CTX_LEN = 256
D_MIX = D_MODEL
SGU_HEADS = 4
SGU_HEAD_DIM = 128
SGU_WIDTH = SGU_HEADS * SGU_HEAD_DIM
SGU_CHUNK = 2 * GRID_W
HGRN_HEADS = 4
HGRN_HEAD_DIM = 128
HGRN_WIDTH = HGRN_HEADS * HGRN_HEAD_DIM
HGRN_CHUNK = 64
N_PROJ = 7
D_IN = N_PROJ * SGU_WIDTH
N_EXPERTS = 32
TOP_K = 4
D_EXPERT = 1024
SWIGLU_LIMIT = 7.0
SWIGLU_ALPHA = 1.702
MOE_BLOCK = 128
EPS = 1e-6

kernel_name = "hybrid_sgu_hgrn2_moe_dit_block"


def rms_norm(x, g):
    xf = x.astype(jnp.float32)
    y = xf * lax.rsqrt(jnp.mean(xf * xf, axis=-1, keepdims=True) + EPS)
    return (y * g.astype(jnp.float32)).astype(x.dtype)


def modulate(h, shift, scale):
    return h * (1.0 + scale) + shift


def flip(a):
    return jnp.flip(a, axis=1)


def lower_bound(table, layer):
    return jnp.cumsum(jax.nn.softmax(table.astype(jnp.float32), axis=0), axis=0)[layer]


def chunk_sgu(u, v, ln_g, w_s, b_s):
    bsz, n, _ = u.shape
    u = jax.nn.gelu(u)
    v = jax.nn.gelu(v)
    vh = v.reshape(bsz, n // SGU_CHUNK, SGU_CHUNK, SGU_HEADS, SGU_HEAD_DIM).astype(jnp.float32)
    mu = jnp.mean(vh, axis=-1, keepdims=True)
    var = jnp.mean(jnp.square(vh - mu), axis=-1, keepdims=True)
    vn = ((vh - mu) * lax.rsqrt(var + EPS) * ln_g.astype(jnp.float32)).astype(v.dtype)
    z = jnp.einsum('hts,bnshd->bnthd', w_s, vn) + b_s.T[:, :, None]
    return u * z.reshape(bsz, n, SGU_WIDTH)


def hgrn_chunk_scan(q, k, v, log_f, s0):
    bsz, n, nh, dk = q.shape
    n_chunks = n // HGRN_CHUNK

    def to_chunks(a):
        return a.reshape(bsz, n_chunks, HGRN_CHUNK, nh, a.shape[-1]).transpose(1, 0, 3, 2, 4)

    mask = jnp.tril(jnp.ones((HGRN_CHUNK, HGRN_CHUNK), dtype=bool))[:, :, None]

    def step(s, inp):
        qc, kc, vc, gc = inp
        b = jnp.cumsum(gc, axis=2)
        diff = b[:, :, :, None, :] - b[:, :, None, :, :]
        decay = jnp.exp(jnp.where(mask, diff, -jnp.inf))
        scores = jnp.einsum('bhtk,bhsk,bhtsk->bhts', qc, kc, decay)
        o = jnp.einsum('bhts,bhsv->bhtv', scores, vc) \
            + jnp.einsum('bhtk,bhkv->bhtv', qc * jnp.exp(b), s)
        b_last = b[:, :, -1:, :]
        s_new = jnp.exp(b_last[:, :, 0, :])[..., None] * s \
            + jnp.einsum('bhsk,bhsv->bhkv', kc * jnp.exp(b_last - b), vc)
        return s_new, o

    s_fin, o = lax.scan(step, s0, (to_chunks(q), to_chunks(k), to_chunks(v), to_chunks(log_f)))
    o = o.transpose(1, 0, 3, 2, 4).reshape(bsz, n, nh, v.shape[-1])
    return o, s_fin


def hgrn_direction(q, i, z, lb, s0):
    bsz, n, _ = q.shape
    f = lb + (1.0 - lb) * jax.nn.sigmoid(z.astype(jnp.float32))
    k = 1.0 - f
    sh = (bsz, n, HGRN_HEADS, HGRN_HEAD_DIM)
    return hgrn_chunk_scan(q.astype(jnp.float32).reshape(sh), k.reshape(sh),
                           i.astype(jnp.float32).reshape(sh), jnp.log(f).reshape(sh), s0)


def merge_heads(u, v, o, g, sgu_ln_l, sgu_w_l, sgu_b_l, hgrn_norm_l, w_out_l):
    y_a = chunk_sgu(u, v, sgu_ln_l, sgu_w_l, sgu_b_l)
    bsz, n = o.shape[:2]
    on = o * lax.rsqrt(jnp.mean(o * o, axis=-1, keepdims=True) + EPS)
    y_b = (on.reshape(bsz, n, HGRN_WIDTH) * hgrn_norm_l.astype(jnp.float32)).astype(g.dtype) \
        * jax.nn.silu(g)
    return jnp.concatenate([y_a, y_b], axis=-1) @ w_out_l


def moe(h, router_w, router_b, w1, b1, w2, b2):
    bsz, n, d = h.shape
    n_tok = bsz * n
    hf = h.reshape(n_tok, d)
    logits = (hf @ router_w + router_b).astype(jnp.float32)
    top_val, top_idx = lax.top_k(logits, TOP_K)
    gate_w = jax.nn.softmax(top_val, axis=-1)
    n_assign = n_tok * TOP_K
    e_flat = top_idx.reshape(-1)
    tok_flat = jnp.repeat(jnp.arange(n_tok, dtype=jnp.int32), TOP_K)
    w_flat = gate_w.reshape(-1)
    order = jnp.argsort(e_flat)
    e_s, tok_s, w_s = e_flat[order], tok_flat[order], w_flat[order]
    counts = jnp.bincount(e_flat, length=N_EXPERTS)
    padded = (counts + MOE_BLOCK - 1) // MOE_BLOCK * MOE_BLOCK
    start = jnp.cumsum(counts) - counts
    pend = jnp.cumsum(padded)
    pstart = pend - padded
    dest = pstart[e_s] + jnp.arange(n_assign, dtype=jnp.int32) - start[e_s]
    n_blocks = -(-(n_assign + N_EXPERTS * (MOE_BLOCK - 1)) // MOE_BLOCK)
    n_rows = n_blocks * MOE_BLOCK
    row_tok = jnp.zeros((n_rows,), jnp.int32).at[dest].set(tok_s)
    row_w = jnp.zeros((n_rows,), jnp.float32).at[dest].set(w_s)
    block_start = jnp.arange(n_blocks, dtype=jnp.int32) * MOE_BLOCK
    block_e = jnp.minimum(jnp.sum(block_start[:, None] >= pend[None, :], axis=1), N_EXPERTS - 1)

    def expert_block(args):
        toks, e = args
        xb = hf[toks]
        gu = xb @ w1[e] + b1[e]
        gate, up = jnp.split(gu, 2, axis=-1)
        gate = jnp.minimum(gate, SWIGLU_LIMIT)
        up = jnp.clip(up, -SWIGLU_LIMIT, SWIGLU_LIMIT)
        act = gate * jax.nn.sigmoid(SWIGLU_ALPHA * gate) * (up + 1.0)
        return act @ w2[e] + b2[e]

    out = lax.map(expert_block, (row_tok.reshape(n_blocks, MOE_BLOCK), block_e))
    out = out.reshape(n_rows, d) * row_w[:, None].astype(h.dtype)
    y = jnp.zeros_like(hf).at[row_tok].add(out)
    return y.reshape(bsz, n, d)


def setup_inputs(seed: int = 0) -> dict:
    key = jax.random.key(seed)
    ks = jax.random.split(key, 24)
    nrm = jax.random.normal
    f32 = jnp.float32
    d = D_MODEL
    return {
        "x": nrm(ks[0], (BATCH, SEQ, d), f32),
        "c": nrm(ks[1], (BATCH, d), f32),
        "ctx": nrm(ks[2], (BATCH, CTX_LEN, d), f32),
        "c_ctx": nrm(ks[3], (d,), f32),
        "w_mod": nrm(ks[4], (DEPTH, d, 6 * d), f32) * (0.5 * d ** -0.5),
        "b_mod": nrm(ks[5], (DEPTH, 6 * d), f32) * 0.02,
        "norm1": 1.0 + 0.02 * nrm(ks[6], (DEPTH, d), f32),
        "w_in": nrm(ks[7], (DEPTH, d, D_IN), f32) * d ** -0.5,
        "sgu_ln": 1.0 + 0.02 * nrm(ks[8], (DEPTH, SGU_HEADS, SGU_HEAD_DIM), f32),
        "sgu_w": nrm(ks[9], (DEPTH, SGU_HEADS, SGU_CHUNK, SGU_CHUNK), f32) * SGU_CHUNK ** -0.5,
        "sgu_b": 1.0 + 0.02 * nrm(ks[10], (DEPTH, SGU_HEADS, SGU_CHUNK), f32),
        "lb_fwd": nrm(ks[11], (DEPTH + 1, HGRN_WIDTH), f32),
        "lb_bwd": nrm(ks[12], (DEPTH + 1, HGRN_WIDTH), f32),
        "hgrn_norm": 1.0 + 0.02 * nrm(ks[13], (DEPTH, HGRN_WIDTH), f32),
        "w_out": nrm(ks[14], (DEPTH, D_MIX, d), f32) * D_MIX ** -0.5,
        "norm2": 1.0 + 0.02 * nrm(ks[15], (DEPTH, d), f32),
        "router_w": nrm(ks[16], (DEPTH, d, N_EXPERTS), f32) * d ** -0.5,
        "router_b": nrm(ks[17], (DEPTH, N_EXPERTS), f32) * 0.01,
        "w1": nrm(ks[18], (DEPTH, N_EXPERTS, d, 2 * D_EXPERT), f32) * d ** -0.5,
        "b1": nrm(ks[19], (DEPTH, N_EXPERTS, 2 * D_EXPERT), f32) * 0.01,
        "w2": nrm(ks[20], (DEPTH, N_EXPERTS, D_EXPERT, d), f32) * D_EXPERT ** -0.5,
        "b2": nrm(ks[21], (DEPTH, N_EXPERTS, d), f32) * 0.01,
        "final_norm": 1.0 + 0.02 * nrm(ks[22], (d,), f32),
    }


def reference(x, c, ctx, c_ctx, w_mod, b_mod, norm1, w_in, sgu_ln, sgu_w, sgu_b,
              lb_fwd, lb_bwd, hgrn_norm, w_out, norm2, router_w, router_b,
              w1, b1, w2, b2, final_norm):
    bsz = x.shape[0]
    s_zero = jnp.zeros((bsz, HGRN_HEADS, HGRN_HEAD_DIM, HGRN_HEAD_DIM), jnp.float32)
    x_ctx = ctx
    for l in range(DEPTH):
        mod = jax.nn.silu(c) @ w_mod[l] + b_mod[l]
        sh1, sc1, g1, sh2, sc2, g2 = jnp.split(mod[:, None, :], 6, axis=-1)
        mod_c = jax.nn.silu(c_ctx) @ w_mod[l] + b_mod[l]
        csh1, csc1, cg1, csh2, csc2, cg2 = jnp.split(mod_c, 6, axis=-1)
        lbf = lower_bound(lb_fwd, l)
        lbb = lower_bound(lb_bwd, l)

        hc = modulate(rms_norm(x_ctx, norm1[l]), csh1, csc1)
        uc, vc, qc, zfc, zbc, ic, gc = jnp.split(hc @ w_in[l], N_PROJ, axis=-1)
        oc_f, st_f = hgrn_direction(qc, ic, zfc, lbf, s_zero)
        oc_b, st_b = hgrn_direction(flip(qc), flip(ic), flip(zbc), lbb, s_zero)

        h = modulate(rms_norm(x, norm1[l]), sh1, sc1)
        u, v, q, zf, zb, i, g = jnp.split(h @ w_in[l], N_PROJ, axis=-1)
        o_f, _ = hgrn_direction(q, i, zf, lbf, st_f)
        o_b, _ = hgrn_direction(flip(q), flip(i), flip(zb), lbb, st_b)
        y = merge_heads(u, v, o_f + flip(o_b), g, sgu_ln[l], sgu_w[l], sgu_b[l],
                        hgrn_norm[l], w_out[l])
        x = x + g1 * y
        h2 = modulate(rms_norm(x, norm2[l]), sh2, sc2)
        x = x + g2 * moe(h2, router_w[l], router_b[l], w1[l], b1[l], w2[l], b2[l])

        if l < DEPTH - 1:
            yc = merge_heads(uc, vc, oc_f + flip(oc_b), gc, sgu_ln[l], sgu_w[l], sgu_b[l],
                             hgrn_norm[l], w_out[l])
            x_ctx = x_ctx + cg1 * yc
            hc2 = modulate(rms_norm(x_ctx, norm2[l]), csh2, csc2)
            x_ctx = x_ctx + cg2 * moe(hc2, router_w[l], router_b[l], w1[l], b1[l], w2[l], b2[l])
    return rms_norm(x, final_norm)
```

```python
import functools
import math

import jax
import jax.numpy as jnp
import numpy as np
from jax import lax
from jax.experimental import pallas as pl
from jax.experimental.pallas import tpu as pltpu

F32 = jnp.float32
BF16 = jnp.bfloat16
EPS = 1e-6

HEAD_DIM = 128
SGU_CHUNK = 128
N_PROJ = 7
TOP_K = 4
SWIGLU_LIMIT = 7.0
SWIGLU_ALPHA = 1.702

SCAN_CHUNK = 64
INPROJ_ROWS = 512
POSTMIX_ROWS = 256
EXPERT_ROWS = 256
TOKEN_TILE = 128
MOD_ROWS = 16
VMEM_LIMIT = 56 * 1024 * 1024


def _params(sem, vmem=None):
    return pltpu.CompilerParams(dimension_semantics=sem, vmem_limit_bytes=vmem or VMEM_LIMIT)


def _mod_kernel(c_ref, w_ref, b_ref, o_ref):
    c = c_ref[...]
    s = c * jax.nn.sigmoid(c)
    o_ref[...] = jnp.dot(s, w_ref[...], precision=lax.Precision.HIGHEST,
                         preferred_element_type=F32) + b_ref[...]


def _modulation(cc, w_mod, b_mod):
    d, n6 = w_mod.shape
    tn = 1024
    return pl.pallas_call(
        _mod_kernel,
        grid=(n6 // tn,),
        in_specs=[pl.BlockSpec((MOD_ROWS, d), lambda j: (0, 0)),
                  pl.BlockSpec((d, tn), lambda j: (0, j)),
                  pl.BlockSpec((1, tn), lambda j: (0, j))],
        out_specs=pl.BlockSpec((MOD_ROWS, tn), lambda j: (0, j)),
        out_shape=jax.ShapeDtypeStruct((MOD_ROWS, n6), F32),
        compiler_params=_params(("arbitrary",)),
        name="mod",
    )(cc, w_mod, b_mod.reshape(1, n6))


def _rms(x, g):
    return x * lax.rsqrt(jnp.mean(x * x, axis=-1, keepdims=True) + EPS) * g


def _inproj_kernel(x_ref, sh_ref, sc_ref, g_ref, w_ref, o_ref, *, width):
    h = _rms(x_ref[0], g_ref[...])
    h = h * (1.0 + sc_ref[0]) + sh_ref[0]
    hb = h.astype(BF16)
    for j in range(w_ref.shape[1] // width):
        cols = slice(j * width, (j + 1) * width)
        o_ref[0, :, cols] = jnp.dot(hb, w_ref[:, cols],
                                    preferred_element_type=F32).astype(o_ref.dtype)


def _inproj(x, mod3, mod_row, norm_g, w_bf16, rows, name):
    bsz, n, d = x.shape
    wcols = w_bf16.shape[1]
    return pl.pallas_call(
        functools.partial(_inproj_kernel, width=512),
        grid=(bsz, n // rows),
        in_specs=[pl.BlockSpec((1, rows, d), lambda b, j: (b, j, 0)),
                  pl.BlockSpec((1, 1, d), lambda b, j: (mod_row(b) * 6 + 0, 0, 0)),
                  pl.BlockSpec((1, 1, d), lambda b, j: (mod_row(b) * 6 + 1, 0, 0)),
                  pl.BlockSpec((1, d), lambda b, j: (0, 0)),
                  pl.BlockSpec((d, wcols), lambda b, j: (0, 0))],
        out_specs=pl.BlockSpec((1, rows, wcols), lambda b, j: (b, j, 0)),
        out_shape=jax.ShapeDtypeStruct((bsz, n, wcols), BF16),
        compiler_params=_params(("arbitrary", "arbitrary")),
        name=name,
    )(x, mod3, mod3, norm_g, w_bf16)


def _scan_constants(length):
    levels = int(math.log2(length))
    assert 1 << levels == length
    t = np.arange(length)
    sums = np.zeros((2, levels + 1, length, length), np.float32)
    masks = np.zeros((2, levels + 1, length, length), np.float32)
    for lv in range(levels):
        half = 1 << lv
        blk = t // (2 * half)
        mid = blk * 2 * half + half
        later = (t % (2 * half)) >= half
        for r in range(length):
            if later[r]:
                sums[0, lv, r, mid[r] + 1:r + 1] = 1.0
            else:
                sums[0, lv, r, r + 1:mid[r] + 1] = 1.0
        masks[0, lv] = (blk[:, None] == blk[None, :]) & later[:, None] & (~later)[None, :]
    sums[0, levels] = np.tril(np.ones((length, length), np.float32))
    masks[0, levels] = np.eye(length, dtype=np.float32)
    sums[1] = sums[0][:, ::-1, ::-1]
    masks[1] = masks[0][:, ::-1, ::-1]
    return sums.reshape(2, (levels + 1) * length, length), masks


def _nt_dot(a, b):
    return lax.dot_general(a, b, (((1,), (1,)), ((), ())), preferred_element_type=F32)


def _scan_kernel(*refs, heads, emit_o, has_init):
    qf_ref, zf_ref, if_ref, qb_ref, zb_ref, ib_ref, tabf_ref, tabb_ref, sums_ref, masks_ref = refs[:10]
    pos = 10
    init_ref = None
    if has_init:
        init_ref = refs[pos]
        pos += 1
    out_ref = refs[pos]
    state_ref = refs[pos + 1]

    c = pl.program_id(1)
    n_chunks = pl.num_programs(1)
    length = qf_ref.shape[1]
    levels = masks_ref.shape[1] - 1

    @pl.when(c == 0)
    def _():
        if has_init:
            state_ref[...] = init_ref[0]
        else:
            state_ref[...] = jnp.zeros_like(state_ref)
        if emit_o:
            out_ref[...] = jnp.zeros_like(out_ref)

    for d, (q_ref, z_ref, i_ref, tab_ref) in enumerate(
            ((qf_ref, zf_ref, if_ref, tabf_ref), (qb_ref, zb_ref, ib_ref, tabb_ref))):
        tab = tab_ref[...]
        ex = jnp.exp(tab - jnp.max(tab, axis=0, keepdims=True))
        lb_all = ex[0:1] / jnp.sum(ex, axis=0, keepdims=True)
        row0 = (c if d == 0 else n_chunks - 1 - c) * length
        total_row = length - 1 if d == 0 else 0
        for h in range(heads):
            cols = slice(h * HEAD_DIM, (h + 1) * HEAD_DIM)
            lb = lb_all[:, cols]
            q = q_ref[0, :, cols].astype(F32)
            v = i_ref[0, :, cols].astype(F32)
            f = lb + (1.0 - lb) * jax.nn.sigmoid(z_ref[0, :, cols].astype(F32))
            k = 1.0 - f
            g = jnp.log(f)
            g_hi = g.astype(BF16)
            g_lo = (g - g_hi.astype(F32)).astype(BF16)
            state = state_ref[d, h]
            if emit_o:
                sm = sums_ref[d]
            else:
                sm = sums_ref[d, levels * length:(levels + 1) * length, :]
            dall = (jnp.dot(sm, g_hi, preferred_element_type=F32)
                    + jnp.dot(sm, g_lo, preferred_element_type=F32))
            bp = dall[-length:]
            btot = bp[total_row:total_row + 1]
            if emit_o:
                scores = masks_ref[d, levels] * _nt_dot(q.astype(BF16), k.astype(BF16))
                for lv in range(levels):
                    e = jnp.exp(dall[lv * length:(lv + 1) * length])
                    scores = scores + masks_ref[d, lv] * _nt_dot((q * e).astype(BF16),
                                                                 (k * e).astype(BF16))
                o = (jnp.dot(scores.astype(BF16), v.astype(BF16), preferred_element_type=F32)
                     + _nt_dot((q * jnp.exp(bp)).astype(BF16), state.astype(BF16)))
                rows = pl.ds(pl.multiple_of(row0, length), length)
                out_ref[0, rows, cols] += o
            kd = (k * jnp.exp(btot - bp)).astype(BF16)
            state_ref[d, h] = state * jnp.exp(btot) + jnp.dot(
                v.T.astype(BF16), kd, preferred_element_type=F32)

    if not emit_o:
        @pl.when(c == n_chunks - 1)
        def _():
            out_ref[0] = state_ref[...]


def _scan(proj, col_q, col_zf, col_zb, col_i, tab_f, tab_b, init, emit_o, name):
    bsz, n, _ = proj.shape
    width = tab_f.shape[1]
    heads = width // HEAD_DIM
    length = SCAN_CHUNK
    n_chunks = n // length
    sums_np, masks_np = _scan_constants(length)
    sums = jnp.asarray(sums_np, BF16)
    masks = jnp.asarray(masks_np, F32)

    def fwd(col):
        return pl.BlockSpec((1, length, width), lambda b, c: (b, c, col))

    def bwd(col):
        return pl.BlockSpec((1, length, width), lambda b, c: (b, n_chunks - 1 - c, col))

    def whole(a):
        nd = a.ndim
        return pl.BlockSpec(a.shape, lambda b, c: (0,) * nd)

    in_specs = [fwd(col_q), fwd(col_zf), fwd(col_i), bwd(col_q), bwd(col_zb), bwd(col_i),
                whole(tab_f), whole(tab_b), whole(sums), whole(masks)]
    args = [proj, proj, proj, proj, proj, proj, tab_f, tab_b, sums, masks]
    state_shape = (2, heads, HEAD_DIM, HEAD_DIM)
    if init is not None:
        in_specs.append(pl.BlockSpec((1,) + state_shape, lambda b, c: (b, 0, 0, 0, 0)))
        args.append(init)
    if emit_o:
        out_spec = pl.BlockSpec((1, n, width), lambda b, c: (b, 0, 0))
        out_shape = jax.ShapeDtypeStruct((bsz, n, width), F32)
    else:
        out_spec = pl.BlockSpec((1,) + state_shape, lambda b, c: (b, 0, 0, 0, 0))
        out_shape = jax.ShapeDtypeStruct((bsz,) + state_shape, F32)
    return pl.pallas_call(
        functools.partial(_scan_kernel, heads=heads, emit_o=emit_o, has_init=init is not None),
        grid=(bsz, n_chunks),
        in_specs=in_specs,
        out_specs=out_spec,
        out_shape=out_shape,
        scratch_shapes=[pltpu.VMEM(state_shape, F32)],
        compiler_params=_params(("arbitrary", "arbitrary")),
        name=name,
    )(*args)


def _postmix_kernel(u_ref, v_ref, g_ref, o_ref, x_ref, g1_ref, sh2_ref, sc2_ref,
                    ln_ref, ws_ref, bs_ref, hn_ref, wout_ref, n2_ref, rw_ref, rb_ref,
                    x1_ref, h2_ref, lg_ref, y_scr):
    rows = x_ref.shape[1]
    heads = ln_ref.shape[0]
    sgu_w = heads * HEAD_DIM
    u = jax.nn.gelu(u_ref[0].astype(F32))
    v = jax.nn.gelu(v_ref[0].astype(F32))
    for ch in range(rows // SGU_CHUNK):
        rs = slice(ch * SGU_CHUNK, (ch + 1) * SGU_CHUNK)
        for h in range(heads):
            cols = slice(h * HEAD_DIM, (h + 1) * HEAD_DIM)
            vh = v[rs, cols]
            mu = jnp.mean(vh, axis=-1, keepdims=True)
            var = jnp.mean(jnp.square(vh - mu), axis=-1, keepdims=True)
            vn = (vh - mu) * lax.rsqrt(var + EPS) * ln_ref[h:h + 1, :]
            z = jnp.dot(ws_ref[h], vn.astype(BF16), preferred_element_type=F32) + bs_ref[h]
            y_scr[rs, cols] = (u[rs, cols] * z).astype(BF16)
    o = o_ref[0]
    gate = g_ref[0].astype(F32)
    gate = gate * jax.nn.sigmoid(gate)
    for h in range(o.shape[1] // HEAD_DIM):
        cols = slice(h * HEAD_DIM, (h + 1) * HEAD_DIM)
        oh = o[:, cols]
        on = oh * lax.rsqrt(jnp.mean(oh * oh, axis=-1, keepdims=True) + EPS)
        y_scr[:, sgu_w + h * HEAD_DIM:sgu_w + (h + 1) * HEAD_DIM] = (
            on * hn_ref[:, cols] * gate[:, cols]).astype(BF16)
    y = jnp.dot(y_scr[...], wout_ref[...], preferred_element_type=F32)
    x1 = x_ref[0] + g1_ref[0] * y
    x1_ref[0] = x1
    h2 = _rms(x1, n2_ref[...]) * (1.0 + sc2_ref[0]) + sh2_ref[0]
    h2_ref[...] = h2.reshape(h2_ref.shape)
    lg_ref[0] = jnp.dot(h2, rw_ref[...], precision=lax.Precision.HIGHEST,
                        preferred_element_type=F32) + rb_ref[...]


def _postmix(proj, o, x, mod3, sgu_ln, sgu_w_bf16, sgu_b3, hgrn_norm, w_out_bf16, norm2,
             router_w, router_b):
    bsz, n, d = x.shape
    width = o.shape[2]
    rows = POSTMIX_ROWS
    n_exp = router_w.shape[1]
    nj = n // rows

    def colgroup(col):
        return pl.BlockSpec((1, rows, width), lambda b, j: (b, j, col))

    def modrow(idx):
        return pl.BlockSpec((1, 1, d), lambda b, j: (b * 6 + idx, 0, 0))

    def whole(a):
        nd = a.ndim
        return pl.BlockSpec(a.shape, lambda b, j: (0,) * nd)

    return pl.pallas_call(
        _postmix_kernel,
        grid=(bsz, nj),
        in_specs=[colgroup(0), colgroup(1), colgroup(6),
                  pl.BlockSpec((1, rows, width), lambda b, j: (b, j, 0)),
                  pl.BlockSpec((1, rows, d), lambda b, j: (b, j, 0)),
                  modrow(2), modrow(3), modrow(4),
                  whole(sgu_ln), whole(sgu_w_bf16), whole(sgu_b3), whole(hgrn_norm),
                  whole(w_out_bf16), whole(norm2), whole(router_w), whole(router_b)],
        out_specs=[pl.BlockSpec((1, rows, d), lambda b, j: (b, j, 0)),
                   pl.BlockSpec((rows, 1, d), lambda b, j: (b * nj + j, 0, 0)),
                   pl.BlockSpec((1, rows, n_exp), lambda b, j: (b, j, 0))],
        out_shape=[jax.ShapeDtypeStruct((bsz, n, d), F32),
                   jax.ShapeDtypeStruct((bsz * n, 1, d), F32),
                   jax.ShapeDtypeStruct((bsz, n, n_exp), F32)],
        scratch_shapes=[pltpu.VMEM((rows, d), BF16)],
        compiler_params=_params(("arbitrary", "arbitrary")),
        name="postmix",
    )(proj, proj, proj, o, x, mod3, mod3, mod3, sgu_ln, sgu_w_bf16, sgu_b3, hgrn_norm,
      w_out_bf16, norm2, router_w, router_b)


def _route_kernel(lt_ref, upper_ref, ones_ref, ltri_ref,
                  dest_ref, gate_ref, be_ref, meta_ref, pads_ref, pos_scr, eidx_scr, *, block_rows):
    n_tiles, n_exp, lanes = lt_ref.shape
    iota_e = lax.broadcasted_iota(jnp.int32, (n_exp, lanes), 0).astype(F32)

    def tile_body(j, carry):
        l = lt_ref[j]
        picks, vals = [], []
        for _ in range(TOP_K):
            m = jnp.max(l, axis=0, keepdims=True)
            idx = jnp.min(jnp.where(l == m, iota_e, float(n_exp)), axis=0, keepdims=True)
            pick = iota_e == idx
            picks.append(pick)
            vals.append(m)
            l = jnp.where(pick, -jnp.inf, l)
            eidx_scr[j, len(picks) - 1:len(picks), :] = idx
        exps = [jnp.exp(v - vals[0]) for v in vals]
        tot = exps[0]
        for e in exps[1:]:
            tot = tot + e
        for kk in range(TOP_K):
            gate_ref[j, kk:kk + 1, :] = exps[kk] / tot
        chosen = picks[0]
        for p in picks[1:]:
            chosen = chosen | p
        cb = jnp.where(chosen, 1.0, 0.0).astype(BF16)
        before = carry + jnp.dot(cb, upper_ref[...], preferred_element_type=F32)
        for kk in range(TOP_K):
            pos_scr[j, kk:kk + 1, :] = jnp.sum(jnp.where(picks[kk], before, 0.0),
                                               axis=0, keepdims=True)
        return carry + jnp.dot(cb, ones_ref[...], preferred_element_type=F32)

    counts = lax.fori_loop(0, n_tiles, tile_body, jnp.zeros((n_exp, lanes), F32))
    n_blk = jnp.ceil(counts * (1.0 / block_rows))
    end_blk = jnp.dot(ltri_ref[...], n_blk.astype(BF16), preferred_element_type=F32)
    start_row = (end_blk - n_blk) * block_rows

    def dest_body(j, _):
        for kk in range(TOP_K):
            idx = eidx_scr[j, kk:kk + 1, :]
            base = jnp.sum(jnp.where(iota_e == idx, start_row, 0.0), axis=0, keepdims=True)
            dest_ref[j, kk:kk + 1, :] = (base + pos_scr[j, kk:kk + 1, :]).astype(jnp.int32)
        return 0

    lax.fori_loop(0, n_tiles, dest_body, 0)

    for piece in range(be_ref.shape[1] // lanes):
        blk = (lax.broadcasted_iota(jnp.int32, (n_exp, lanes), 1) + piece * lanes).astype(F32)
        owner = jnp.sum(jnp.where(blk >= end_blk, 1.0, 0.0), axis=0, keepdims=True)
        be_ref[:, piece * lanes:(piece + 1) * lanes] = jnp.minimum(
            owner, float(n_exp - 1)).astype(jnp.int32)
    meta_ref[...] = end_blk[n_exp - 1:n_exp, :].astype(jnp.int32)
    pads_ref[0] = (start_row + counts).astype(jnp.int32)
    pads_ref[1] = (n_blk * block_rows - counts).astype(jnp.int32)
    pads_ref[2] = end_blk.astype(jnp.int32)


def _route(lt3, n_blocks_pad):
    n_tiles, n_exp, lanes = lt3.shape
    upper = jnp.asarray(np.triu(np.ones((lanes, lanes), np.float32), 1), BF16)
    ones = jnp.ones((lanes, lanes), BF16)
    ltri = jnp.asarray(np.tril(np.ones((n_exp, n_exp), np.float32)), BF16)
    vm = pl.BlockSpec(memory_space=pltpu.VMEM)
    return pl.pallas_call(
        functools.partial(_route_kernel, block_rows=EXPERT_ROWS),
        in_specs=[vm, vm, vm, vm],
        out_specs=[vm, vm, vm, vm, vm],
        out_shape=[jax.ShapeDtypeStruct((n_tiles, TOP_K, lanes), jnp.int32),
                   jax.ShapeDtypeStruct((n_tiles, TOP_K, lanes), F32),
                   jax.ShapeDtypeStruct((1, n_blocks_pad), jnp.int32),
                   jax.ShapeDtypeStruct((1, lanes), jnp.int32),
                   jax.ShapeDtypeStruct((3, n_exp, lanes), jnp.int32)],
        scratch_shapes=[pltpu.VMEM((n_tiles, TOP_K, lanes), F32),
                        pltpu.VMEM((n_tiles, TOP_K, lanes), F32)],
        compiler_params=pltpu.CompilerParams(vmem_limit_bytes=VMEM_LIMIT),
        name="route",
    )(lt3, upper, ones, ltri)


def _pad_sizes():
    return [1 << b for b in reversed(range(int(math.log2(EXPERT_ROWS))))]


def _dispatch_kernel(pads_ref, dest_ref, h2_hbm, xs_hbm, zero_scr, sem, zsem):
    j = pl.program_id(0)
    n_steps = pl.num_programs(0)
    tile = dest_ref.shape[2]
    n_exp = pads_ref.shape[1]

    def pad_copies(e, wait):
        off = pads_ref[0, e]
        cnt = pads_ref[1, e]
        for sz in _pad_sizes():
            cp = pltpu.make_async_copy(zero_scr.at[pl.ds(0, sz)], xs_hbm.at[pl.ds(off, sz)], zsem)

            @pl.when((cnt & sz) != 0)
            def _():
                if wait:
                    cp.wait()
                else:
                    cp.start()
            off = off + (cnt & sz)

    half = zero_scr.shape[0]
    n_blocks = xs_hbm.shape[0] // (2 * half)
    n_used = pads_ref[2, n_exp - 1]

    def tail_copies(blk, wait):
        for part in range(2):
            cp = pltpu.make_async_copy(
                zero_scr, xs_hbm.at[pl.ds((blk * 2 + part) * half, half)], zsem)
            if wait:
                cp.wait()
            else:
                cp.start()

    @pl.when(j == 0)
    def _():
        zero_scr[...] = jnp.zeros_like(zero_scr)

        def start(e, _):
            pad_copies(e, False)
            return 0
        lax.fori_loop(0, n_exp, start, 0)

        def start_tail(blk, _):
            tail_copies(blk, False)
            return 0
        lax.fori_loop(n_used, n_blocks, start_tail, 0)

    def issue(i, _):
        tok = j * tile + i
        for kk in range(TOP_K):
            pltpu.make_async_copy(h2_hbm.at[tok], xs_hbm.at[dest_ref[0, kk, i]], sem).start()
        return 0
    lax.fori_loop(0, tile, issue, 0)

    def drain():
        pltpu.make_async_copy(h2_hbm.at[pl.ds(0, tile * TOP_K)],
                              xs_hbm.at[pl.ds(0, tile * TOP_K)], sem).wait()

    @pl.when(j > 0)
    def _():
        drain()

    @pl.when(j == n_steps - 1)
    def _():
        drain()

        def finish(e, _):
            pad_copies(e, True)
            return 0
        lax.fori_loop(0, n_exp, finish, 0)

        def finish_tail(blk, _):
            tail_copies(blk, True)
            return 0
        lax.fori_loop(n_used, n_blocks, finish_tail, 0)


def _dispatch(pads, dest, h2_rows, n_rows):
    n_tok, _, d = h2_rows.shape
    n_tiles = dest.shape[0]
    grid_spec = pltpu.PrefetchScalarGridSpec(
        num_scalar_prefetch=1,
        grid=(n_tiles,),
        in_specs=[pl.BlockSpec((1, TOP_K, TOKEN_TILE), lambda j, p: (j, 0, 0),
                               memory_space=pltpu.SMEM),
                  pl.BlockSpec(memory_space=pl.ANY)],
        out_specs=pl.BlockSpec(memory_space=pl.ANY),
        scratch_shapes=[pltpu.VMEM((EXPERT_ROWS // 2, 1, d), F32),
                        pltpu.SemaphoreType.DMA(()),
                        pltpu.SemaphoreType.DMA(())],
    )
    return pl.pallas_call(
        _dispatch_kernel,
        grid_spec=grid_spec,
        out_shape=jax.ShapeDtypeStruct((n_rows, 1, d), F32),
        compiler_params=_params(("arbitrary",)),
        name="dispatch",
    )(pads, dest, h2_rows)


def _experts_kernel(be_ref, nv_ref, x_ref, w1_ref, b1_ref, w2_ref, b2_ref, o_ref,
                    w1_scr, w2_scr, x_scr, act_scr):
    i = pl.program_id(0)
    prev = be_ref[jnp.maximum(i - 1, 0)]
    changed = (i == 0) | (be_ref[i] != prev)
    d, two_f = w1_scr.shape
    f = two_f // 2

    @pl.when(changed)
    def _():
        step = 64

        def cast1(r, _):
            rs = pl.ds(pl.multiple_of(r * step, step), step)
            w1_scr[rs, :] = w1_ref[0, rs, :].astype(BF16)
            return 0
        lax.fori_loop(0, d // step, cast1, 0)

        def cast2(r, _):
            rs = pl.ds(pl.multiple_of(r * step, step), step)
            w2_scr[rs, :] = w2_ref[0, rs, :].astype(BF16)
            return 0
        lax.fori_loop(0, f // step, cast2, 0)

    @pl.when(i < nv_ref[0])
    def _():
        x_scr[...] = x_ref[...].reshape(x_scr.shape)
        xb = x_scr[...].astype(BF16)
        width = 512
        for n in range(f // width):
            cg = slice(n * width, (n + 1) * width)
            cu = slice(f + n * width, f + (n + 1) * width)
            gate = jnp.dot(xb, w1_scr[:, cg], preferred_element_type=F32) + b1_ref[0, :, cg]
            up = jnp.dot(xb, w1_scr[:, cu], preferred_element_type=F32) + b1_ref[0, :, cu]
            gate = jnp.minimum(gate, SWIGLU_LIMIT)
            up = jnp.clip(up, -SWIGLU_LIMIT, SWIGLU_LIMIT)
            act_scr[:, cg] = (gate * jax.nn.sigmoid(SWIGLU_ALPHA * gate) * (up + 1.0)).astype(BF16)
        out = jnp.dot(act_scr[...], w2_scr[...], preferred_element_type=F32) + b2_ref[0]
        o_ref[...] = out.reshape(o_ref.shape)

    @pl.when(i >= nv_ref[0])
    def _():
        o_ref[...] = jnp.zeros_like(o_ref)


def _experts(block_expert, n_valid, xs, w1, b1, w2, b2):
    n_rows, _, d = xs.shape
    n_exp, _, two_f = w1.shape
    f = two_f // 2
    rows = EXPERT_ROWS
    n_blocks = n_rows // rows
    grid_spec = pltpu.PrefetchScalarGridSpec(
        num_scalar_prefetch=2,
        grid=(n_blocks,),
        in_specs=[pl.BlockSpec((rows, 1, d), lambda i, be, nv: (jnp.minimum(i, nv[0] - 1), 0, 0)),
                  pl.BlockSpec((1, d, two_f), lambda i, be, nv: (be[i], 0, 0)),
                  pl.BlockSpec((1, 1, two_f), lambda i, be, nv: (be[i], 0, 0)),
                  pl.BlockSpec((1, f, d), lambda i, be, nv: (be[i], 0, 0)),
                  pl.BlockSpec((1, 1, d), lambda i, be, nv: (be[i], 0, 0))],
        out_specs=pl.BlockSpec((rows, 1, d), lambda i, be, nv: (i, 0, 0)),
        scratch_shapes=[pltpu.VMEM((d, two_f), BF16),
                        pltpu.VMEM((f, d), BF16),
                        pltpu.VMEM((rows, d), F32),
                        pltpu.VMEM((rows, f), BF16)],
    )
    return pl.pallas_call(
        _experts_kernel,
        grid_spec=grid_spec,
        out_shape=jax.ShapeDtypeStruct((n_rows, 1, d), F32),
        compiler_params=_params(("arbitrary",)),
        name="experts",
    )(block_expert, n_valid, xs, w1, b1.reshape(n_exp, 1, two_f), w2, b2.reshape(n_exp, 1, d))


def _combine_kernel(dest_ref, ys_hbm, gate_ref, x1_ref, g2_ref, fn_ref, o_ref,
                    b0, b1, b2, b3, flat_scr, sem):
    tile = dest_ref.shape[2]
    bufs = (b0, b1, b2, b3)

    def issue(i, _):
        for kk in range(TOP_K):
            pltpu.make_async_copy(ys_hbm.at[dest_ref[0, kk, i]], bufs[kk].at[i], sem).start()
        return 0
    lax.fori_loop(0, tile, issue, 0)
    for kk in range(TOP_K):
        pltpu.make_async_copy(ys_hbm.at[pl.ds(0, tile)], bufs[kk], sem).wait()
    y = None
    for kk in range(TOP_K):
        flat_scr[...] = bufs[kk][...].reshape(flat_scr.shape)
        term = gate_ref[:, kk:kk + 1] * flat_scr[...]
        y = term if y is None else y + term
    x2 = x1_ref[...] + g2_ref[0] * y
    o_ref[...] = _rms(x2, fn_ref[...])


def _combine(dest, ys, gates_tok, x1, mod3, final_norm, seq):
    n_tok, d = x1.shape
    n_tiles = dest.shape[0]
    tiles_per_sample = seq // TOKEN_TILE
    return pl.pallas_call(
        _combine_kernel,
        grid=(n_tiles,),
        in_specs=[pl.BlockSpec((1, TOP_K, TOKEN_TILE), lambda j: (j, 0, 0),
                               memory_space=pltpu.SMEM),
                  pl.BlockSpec(memory_space=pl.ANY),
                  pl.BlockSpec((TOKEN_TILE, TOP_K), lambda j: (j, 0)),
                  pl.BlockSpec((TOKEN_TILE, d), lambda j: (j, 0)),
                  pl.BlockSpec((1, 1, d), lambda j: ((j // tiles_per_sample) * 6 + 5, 0, 0)),
                  pl.BlockSpec((1, d), lambda j: (0, 0))],
        out_specs=pl.BlockSpec((TOKEN_TILE, d), lambda j: (j, 0)),
        out_shape=jax.ShapeDtypeStruct((n_tok, d), F32),
        scratch_shapes=[pltpu.VMEM((TOKEN_TILE, 1, d), F32) for _ in range(TOP_K)]
        + [pltpu.VMEM((TOKEN_TILE, d), F32), pltpu.SemaphoreType.DMA(())],
        compiler_params=_params(("arbitrary",)),
        name="combine",
    )(dest, ys, gates_tok, x1, mod3, final_norm)


def kernel(x, c, ctx, c_ctx, w_mod, b_mod, norm1, w_in, sgu_ln, sgu_w, sgu_b, lb_fwd, lb_bwd,
           hgrn_norm, w_out, norm2, router_w, router_b, w1, b1, w2, b2, final_norm):
    depth = w_mod.shape[0]
    assert depth == 1, "single-layer block only"
    bsz, seq, d = x.shape
    assert bsz + 1 <= MOD_ROWS
    width = w_in.shape[2] // N_PROJ
    n_exp = router_w.shape[2]
    n_tok = bsz * seq

    cc = jnp.zeros((MOD_ROWS, d), F32).at[:bsz].set(c).at[bsz].set(c_ctx)
    mod = _modulation(cc, w_mod[0], b_mod[0])
    mod3 = mod.reshape(MOD_ROWS * 6, 1, d)

    w_in_b = w_in[0].astype(BF16)
    norm1_l = norm1[0].reshape(1, d)
    proj_ctx = _inproj(ctx, mod3, lambda b: bsz, norm1_l, w_in_b[:, 2 * width:6 * width],
                       ctx.shape[1], "inproj_ctx")
    states = _scan(proj_ctx, 0, 1, 2, 3, lb_fwd, lb_bwd, None, False, "scan_ctx")
    proj = _inproj(x, mod3, lambda b: b, norm1_l, w_in_b, INPROJ_ROWS, "inproj")
    o = _scan(proj, 2, 3, 4, 5, lb_fwd, lb_bwd, states, True, "scan")

    x1, h2_rows, logits = _postmix(
        proj, o, x, mod3, sgu_ln[0], sgu_w[0].astype(BF16),
        sgu_b[0].reshape(sgu_b.shape[1], sgu_b.shape[2], 1), hgrn_norm[0].reshape(1, -1),
        w_out[0].astype(BF16), norm2[0].reshape(1, d), router_w[0], router_b[0].reshape(1, n_exp))

    n_tiles = n_tok // TOKEN_TILE
    n_blocks = -(-(n_tok * TOP_K + n_exp * (EXPERT_ROWS - 1)) // EXPERT_ROWS)
    n_blocks_pad = -(-n_blocks // 128) * 128
    lt3 = logits.reshape(n_tiles, TOKEN_TILE, n_exp).transpose(0, 2, 1)
    dest, gates, block_expert, meta, pads = _route(lt3, n_blocks_pad)

    xs = _dispatch(pads[:, :, 0], dest, h2_rows, n_blocks * EXPERT_ROWS)
    ys = _experts(block_expert[0], meta[0, :1], xs, w1[0], b1[0], w2[0], b2[0])
    gates_tok = gates.transpose(0, 2, 1).reshape(n_tok, TOP_K)
    out = _combine(dest, ys, gates_tok, x1.reshape(n_tok, d), mod3, final_norm.reshape(1, d), seq)
    return out.reshape(bsz, seq, d)
```

```python
import functools
import math

import jax
import jax.numpy as jnp
import numpy as np
from jax import lax
from jax.experimental import pallas as pl
from jax.experimental.pallas import tpu as pltpu

F32 = jnp.float32
BF16 = jnp.bfloat16
EPS = 1e-6

HEAD_DIM = 128
SGU_CHUNK = 128
N_PROJ = 7
TOP_K = 4
SWIGLU_LIMIT = 7.0
SWIGLU_ALPHA = 1.702

SCAN_CHUNK = 64
INPROJ_ROWS = 512
POSTMIX_ROWS = 256
EXPERT_ROWS = 256
TOKEN_TILE = 128
DISPATCH_SUBTILES = 4
MOD_ROWS = 16
VMEM_LIMIT = 56 * 1024 * 1024


def _params(sem, vmem=None):
    return pltpu.CompilerParams(dimension_semantics=sem, vmem_limit_bytes=vmem or VMEM_LIMIT)


def _mod_kernel(c_ref, w_ref, b_ref, o_ref):
    c = c_ref[...]
    s = c * jax.nn.sigmoid(c)
    o_ref[...] = jnp.dot(s, w_ref[...], precision=lax.Precision.HIGHEST,
                         preferred_element_type=F32) + b_ref[...]


def _modulation(cc, w_mod, b_mod):
    d, n6 = w_mod.shape
    tn = 1024
    return pl.pallas_call(
        _mod_kernel,
        grid=(n6 // tn,),
        in_specs=[pl.BlockSpec((MOD_ROWS, d), lambda j: (0, 0)),
                  pl.BlockSpec((d, tn), lambda j: (0, j)),
                  pl.BlockSpec((1, tn), lambda j: (0, j))],
        out_specs=pl.BlockSpec((MOD_ROWS, tn), lambda j: (0, j)),
        out_shape=jax.ShapeDtypeStruct((MOD_ROWS, n6), F32),
        compiler_params=_params(("arbitrary",)),
        name="mod",
    )(cc, w_mod, b_mod.reshape(1, n6))


def _rms(x, g):
    return x * lax.rsqrt(jnp.mean(x * x, axis=-1, keepdims=True) + EPS) * g


def _inproj_kernel(x_ref, sh_ref, sc_ref, g_ref, w_ref, o_ref, *, width):
    h = _rms(x_ref[0], g_ref[...])
    h = h * (1.0 + sc_ref[0]) + sh_ref[0]
    hb = h.astype(BF16)
    for j in range(w_ref.shape[1] // width):
        cols = slice(j * width, (j + 1) * width)
        o_ref[0, :, cols] = jnp.dot(hb, w_ref[:, cols],
                                    preferred_element_type=F32).astype(o_ref.dtype)


def _inproj(x, mod3, mod_row, norm_g, w_bf16, rows, name):
    bsz, n, d = x.shape
    wcols = w_bf16.shape[1]
    return pl.pallas_call(
        functools.partial(_inproj_kernel, width=512),
        grid=(bsz, n // rows),
        in_specs=[pl.BlockSpec((1, rows, d), lambda b, j: (b, j, 0)),
                  pl.BlockSpec((1, 1, d), lambda b, j: (mod_row(b) * 6 + 0, 0, 0)),
                  pl.BlockSpec((1, 1, d), lambda b, j: (mod_row(b) * 6 + 1, 0, 0)),
                  pl.BlockSpec((1, d), lambda b, j: (0, 0)),
                  pl.BlockSpec((d, wcols), lambda b, j: (0, 0))],
        out_specs=pl.BlockSpec((1, rows, wcols), lambda b, j: (b, j, 0)),
        out_shape=jax.ShapeDtypeStruct((bsz, n, wcols), BF16),
        compiler_params=_params(("arbitrary", "arbitrary")),
        name=name,
    )(x, mod3, mod3, norm_g, w_bf16)


def _scan_constants(length):
    levels = int(math.log2(length))
    assert 1 << levels == length
    t = np.arange(length)
    sums = np.zeros((2, levels + 1, length, length), np.float32)
    masks = np.zeros((2, levels + 1, length, length), np.float32)
    for lv in range(levels):
        half = 1 << lv
        blk = t // (2 * half)
        mid = blk * 2 * half + half
        later = (t % (2 * half)) >= half
        for r in range(length):
            if later[r]:
                sums[0, lv, r, mid[r] + 1:r + 1] = 1.0
            else:
                sums[0, lv, r, r + 1:mid[r] + 1] = 1.0
        masks[0, lv] = (blk[:, None] == blk[None, :]) & later[:, None] & (~later)[None, :]
    sums[0, levels] = np.tril(np.ones((length, length), np.float32))
    masks[0, levels] = np.eye(length, dtype=np.float32)
    sums[1] = sums[0][:, ::-1, ::-1]
    masks[1] = masks[0][:, ::-1, ::-1]
    return sums.reshape(2, (levels + 1) * length, length), masks


def _nt_dot(a, b):
    return lax.dot_general(a, b, (((1,), (1,)), ((), ())), preferred_element_type=F32)


def _scan_kernel(*refs, heads, emit_o, has_init):
    qf_ref, zf_ref, if_ref, qb_ref, zb_ref, ib_ref, tabf_ref, tabb_ref, sums_ref, masks_ref = refs[:10]
    pos = 10
    init_ref = None
    if has_init:
        init_ref = refs[pos]
        pos += 1
    out_ref = refs[pos]
    state_ref = refs[pos + 1]

    c = pl.program_id(1)
    n_chunks = pl.num_programs(1)
    length = qf_ref.shape[1]
    levels = masks_ref.shape[1] - 1

    @pl.when(c == 0)
    def _():
        if has_init:
            state_ref[...] = init_ref[0]
        else:
            state_ref[...] = jnp.zeros_like(state_ref)
        if emit_o:
            out_ref[...] = jnp.zeros_like(out_ref)

    for d, (q_ref, z_ref, i_ref, tab_ref) in enumerate(
            ((qf_ref, zf_ref, if_ref, tabf_ref), (qb_ref, zb_ref, ib_ref, tabb_ref))):
        tab = tab_ref[...]
        ex = jnp.exp(tab - jnp.max(tab, axis=0, keepdims=True))
        lb_all = ex[0:1] / jnp.sum(ex, axis=0, keepdims=True)
        row0 = (c if d == 0 else n_chunks - 1 - c) * length
        total_row = length - 1 if d == 0 else 0
        for h in range(heads):
            cols = slice(h * HEAD_DIM, (h + 1) * HEAD_DIM)
            lb = lb_all[:, cols]
            q = q_ref[0, :, cols].astype(F32)
            v = i_ref[0, :, cols].astype(F32)
            f = lb + (1.0 - lb) * jax.nn.sigmoid(z_ref[0, :, cols].astype(F32))
            k = 1.0 - f
            g = jnp.log(f)
            g_hi = g.astype(BF16)
            g_lo = (g - g_hi.astype(F32)).astype(BF16)
            state = state_ref[d, h]
            if emit_o:
                sm = sums_ref[d]
            else:
                sm = sums_ref[d, levels * length:(levels + 1) * length, :]
            dall = (jnp.dot(sm, g_hi, preferred_element_type=F32)
                    + jnp.dot(sm, g_lo, preferred_element_type=F32))
            bp = dall[-length:]
            btot = bp[total_row:total_row + 1]
            if emit_o:
                scores = masks_ref[d, levels] * _nt_dot(q.astype(BF16), k.astype(BF16))
                for lv in range(levels):
                    e = jnp.exp(dall[lv * length:(lv + 1) * length])
                    scores = scores + masks_ref[d, lv] * _nt_dot((q * e).astype(BF16),
                                                                 (k * e).astype(BF16))
                o = (jnp.dot(scores.astype(BF16), v.astype(BF16), preferred_element_type=F32)
                     + _nt_dot((q * jnp.exp(bp)).astype(BF16), state.astype(BF16)))
                rows = pl.ds(pl.multiple_of(row0, length), length)
                out_ref[0, rows, cols] += o
            kd = (k * jnp.exp(btot - bp)).astype(BF16)
            state_ref[d, h] = state * jnp.exp(btot) + jnp.dot(
                v.T.astype(BF16), kd, preferred_element_type=F32)

    if not emit_o:
        @pl.when(c == n_chunks - 1)
        def _():
            out_ref[0] = state_ref[...]


def _scan(proj, col_q, col_zf, col_zb, col_i, tab_f, tab_b, init, emit_o, name):
    bsz, n, _ = proj.shape
    width = tab_f.shape[1]
    heads = width // HEAD_DIM
    length = SCAN_CHUNK
    n_chunks = n // length
    sums_np, masks_np = _scan_constants(length)
    sums = jnp.asarray(sums_np, BF16)
    masks = jnp.asarray(masks_np, F32)

    def fwd(col):
        return pl.BlockSpec((1, length, width), lambda b, c: (b, c, col))

    def bwd(col):
        return pl.BlockSpec((1, length, width), lambda b, c: (b, n_chunks - 1 - c, col))

    def whole(a):
        nd = a.ndim
        return pl.BlockSpec(a.shape, lambda b, c: (0,) * nd)

    in_specs = [fwd(col_q), fwd(col_zf), fwd(col_i), bwd(col_q), bwd(col_zb), bwd(col_i),
                whole(tab_f), whole(tab_b), whole(sums), whole(masks)]
    args = [proj, proj, proj, proj, proj, proj, tab_f, tab_b, sums, masks]
    state_shape = (2, heads, HEAD_DIM, HEAD_DIM)
    if init is not None:
        in_specs.append(pl.BlockSpec((1,) + state_shape, lambda b, c: (b, 0, 0, 0, 0)))
        args.append(init)
    if emit_o:
        out_spec = pl.BlockSpec((1, n, width), lambda b, c: (b, 0, 0))
        out_shape = jax.ShapeDtypeStruct((bsz, n, width), F32)
    else:
        out_spec = pl.BlockSpec((1,) + state_shape, lambda b, c: (b, 0, 0, 0, 0))
        out_shape = jax.ShapeDtypeStruct((bsz,) + state_shape, F32)
    return pl.pallas_call(
        functools.partial(_scan_kernel, heads=heads, emit_o=emit_o, has_init=init is not None),
        grid=(bsz, n_chunks),
        in_specs=in_specs,
        out_specs=out_spec,
        out_shape=out_shape,
        scratch_shapes=[pltpu.VMEM(state_shape, F32)],
        compiler_params=_params(("arbitrary", "arbitrary")),
        name=name,
    )(*args)


def _postmix_kernel(u_ref, v_ref, g_ref, o_ref, x_ref, g1_ref, sh2_ref, sc2_ref,
                    ln_ref, ws_ref, bs_ref, hn_ref, wout_ref, n2_ref, rw_ref, rb_ref,
                    x1_ref, h2_ref, lg_ref, y_scr):
    rows = x_ref.shape[1]
    heads = ln_ref.shape[0]
    sgu_w = heads * HEAD_DIM
    u = jax.nn.gelu(u_ref[0].astype(F32))
    v = jax.nn.gelu(v_ref[0].astype(F32))
    for ch in range(rows // SGU_CHUNK):
        rs = slice(ch * SGU_CHUNK, (ch + 1) * SGU_CHUNK)
        for h in range(heads):
            cols = slice(h * HEAD_DIM, (h + 1) * HEAD_DIM)
            vh = v[rs, cols]
            mu = jnp.mean(vh, axis=-1, keepdims=True)
            var = jnp.mean(jnp.square(vh - mu), axis=-1, keepdims=True)
            vn = (vh - mu) * lax.rsqrt(var + EPS) * ln_ref[h:h + 1, :]
            z = jnp.dot(ws_ref[h], vn.astype(BF16), preferred_element_type=F32) + bs_ref[h]
            y_scr[rs, cols] = (u[rs, cols] * z).astype(BF16)
    o = o_ref[0]
    gate = g_ref[0].astype(F32)
    gate = gate * jax.nn.sigmoid(gate)
    for h in range(o.shape[1] // HEAD_DIM):
        cols = slice(h * HEAD_DIM, (h + 1) * HEAD_DIM)
        oh = o[:, cols]
        on = oh * lax.rsqrt(jnp.mean(oh * oh, axis=-1, keepdims=True) + EPS)
        y_scr[:, sgu_w + h * HEAD_DIM:sgu_w + (h + 1) * HEAD_DIM] = (
            on * hn_ref[:, cols] * gate[:, cols]).astype(BF16)
    y = jnp.dot(y_scr[...], wout_ref[...], preferred_element_type=F32)
    x1 = x_ref[0] + g1_ref[0] * y
    x1_ref[0] = x1
    h2 = _rms(x1, n2_ref[...]) * (1.0 + sc2_ref[0]) + sh2_ref[0]
    h2_ref[...] = h2.reshape(h2_ref.shape)
    lg_ref[0] = jnp.dot(h2, rw_ref[...], precision=lax.Precision.HIGHEST,
                        preferred_element_type=F32) + rb_ref[...]


def _postmix(proj, o, x, mod3, sgu_ln, sgu_w_bf16, sgu_b3, hgrn_norm, w_out_bf16, norm2,
             router_w, router_b):
    bsz, n, d = x.shape
    width = o.shape[2]
    rows = POSTMIX_ROWS
    n_exp = router_w.shape[1]
    nj = n // rows

    def colgroup(col):
        return pl.BlockSpec((1, rows, width), lambda b, j: (b, j, col))

    def modrow(idx):
        return pl.BlockSpec((1, 1, d), lambda b, j: (b * 6 + idx, 0, 0))

    def whole(a):
        nd = a.ndim
        return pl.BlockSpec(a.shape, lambda b, j: (0,) * nd)

    return pl.pallas_call(
        _postmix_kernel,
        grid=(bsz, nj),
        in_specs=[colgroup(0), colgroup(1), colgroup(6),
                  pl.BlockSpec((1, rows, width), lambda b, j: (b, j, 0)),
                  pl.BlockSpec((1, rows, d), lambda b, j: (b, j, 0)),
                  modrow(2), modrow(3), modrow(4),
                  whole(sgu_ln), whole(sgu_w_bf16), whole(sgu_b3), whole(hgrn_norm),
                  whole(w_out_bf16), whole(norm2), whole(router_w), whole(router_b)],
        out_specs=[pl.BlockSpec((1, rows, d), lambda b, j: (b, j, 0)),
                   pl.BlockSpec((rows, 1, d), lambda b, j: (b * nj + j, 0, 0)),
                   pl.BlockSpec((1, rows, n_exp), lambda b, j: (b, j, 0))],
        out_shape=[jax.ShapeDtypeStruct((bsz, n, d), F32),
                   jax.ShapeDtypeStruct((bsz * n, 1, d), F32),
                   jax.ShapeDtypeStruct((bsz, n, n_exp), F32)],
        scratch_shapes=[pltpu.VMEM((rows, d), BF16)],
        compiler_params=_params(("arbitrary", "arbitrary")),
        name="postmix",
    )(proj, proj, proj, o, x, mod3, mod3, mod3, sgu_ln, sgu_w_bf16, sgu_b3, hgrn_norm,
      w_out_bf16, norm2, router_w, router_b)


def _route_kernel(lt_ref, upper_ref, ones_ref, ltri_ref,
                  dest_ref, gate_ref, be_ref, meta_ref, pads_ref, pos_scr, eidx_scr, *, block_rows):
    n_tiles, n_exp, lanes = lt_ref.shape
    iota_e = lax.broadcasted_iota(jnp.int32, (n_exp, lanes), 0).astype(F32)

    def tile_body(j, carry):
        l = lt_ref[j]
        picks, vals = [], []
        for _ in range(TOP_K):
            m = jnp.max(l, axis=0, keepdims=True)
            idx = jnp.min(jnp.where(l == m, iota_e, float(n_exp)), axis=0, keepdims=True)
            pick = iota_e == idx
            picks.append(pick)
            vals.append(m)
            l = jnp.where(pick, -jnp.inf, l)
            eidx_scr[j, len(picks) - 1:len(picks), :] = idx
        exps = [jnp.exp(v - vals[0]) for v in vals]
        tot = exps[0]
        for e in exps[1:]:
            tot = tot + e
        for kk in range(TOP_K):
            gate_ref[j, kk:kk + 1, :] = exps[kk] / tot
        chosen = picks[0]
        for p in picks[1:]:
            chosen = chosen | p
        cb = jnp.where(chosen, 1.0, 0.0).astype(BF16)
        before = carry + jnp.dot(cb, upper_ref[...], preferred_element_type=F32)
        for kk in range(TOP_K):
            pos_scr[j, kk:kk + 1, :] = jnp.sum(jnp.where(picks[kk], before, 0.0),
                                               axis=0, keepdims=True)
        return carry + jnp.dot(cb, ones_ref[...], preferred_element_type=F32)

    counts = lax.fori_loop(0, n_tiles, tile_body, jnp.zeros((n_exp, lanes), F32))
    n_blk = jnp.ceil(counts * (1.0 / block_rows))
    end_blk = jnp.dot(ltri_ref[...], n_blk.astype(BF16), preferred_element_type=F32)
    start_row = (end_blk - n_blk) * block_rows

    def dest_body(j, _):
        for kk in range(TOP_K):
            idx = eidx_scr[j, kk:kk + 1, :]
            base = jnp.sum(jnp.where(iota_e == idx, start_row, 0.0), axis=0, keepdims=True)
            dest_ref[j, kk:kk + 1, :] = (base + pos_scr[j, kk:kk + 1, :]).astype(jnp.int32)
        return 0

    lax.fori_loop(0, n_tiles, dest_body, 0)

    for piece in range(be_ref.shape[1] // lanes):
        blk = (lax.broadcasted_iota(jnp.int32, (n_exp, lanes), 1) + piece * lanes).astype(F32)
        owner = jnp.sum(jnp.where(blk >= end_blk, 1.0, 0.0), axis=0, keepdims=True)
        be_ref[:, piece * lanes:(piece + 1) * lanes] = jnp.minimum(
            owner, float(n_exp - 1)).astype(jnp.int32)
    meta_ref[...] = end_blk[n_exp - 1:n_exp, :].astype(jnp.int32)
    pads_ref[0] = (start_row + counts).astype(jnp.int32)
    pads_ref[1] = (n_blk * block_rows - counts).astype(jnp.int32)
    pads_ref[2] = end_blk.astype(jnp.int32)


def _route(lt3, n_blocks_pad):
    n_tiles, n_exp, lanes = lt3.shape
    upper = jnp.asarray(np.triu(np.ones((lanes, lanes), np.float32), 1), BF16)
    ones = jnp.ones((lanes, lanes), BF16)
    ltri = jnp.asarray(np.tril(np.ones((n_exp, n_exp), np.float32)), BF16)
    vm = pl.BlockSpec(memory_space=pltpu.VMEM)
    return pl.pallas_call(
        functools.partial(_route_kernel, block_rows=EXPERT_ROWS),
        in_specs=[vm, vm, vm, vm],
        out_specs=[vm, vm, vm, vm, vm],
        out_shape=[jax.ShapeDtypeStruct((n_tiles, TOP_K, lanes), jnp.int32),
                   jax.ShapeDtypeStruct((n_tiles, TOP_K, lanes), F32),
                   jax.ShapeDtypeStruct((1, n_blocks_pad), jnp.int32),
                   jax.ShapeDtypeStruct((1, lanes), jnp.int32),
                   jax.ShapeDtypeStruct((3, n_exp, lanes), jnp.int32)],
        scratch_shapes=[pltpu.VMEM((n_tiles, TOP_K, lanes), F32),
                        pltpu.VMEM((n_tiles, TOP_K, lanes), F32)],
        compiler_params=pltpu.CompilerParams(vmem_limit_bytes=VMEM_LIMIT),
        name="route",
    )(lt3, upper, ones, ltri)


def _pad_sizes():
    return [1 << b for b in reversed(range(int(math.log2(EXPERT_ROWS))))]


def _dispatch_kernel(pads_ref, dest_ref, h2_ref, xs_hbm, zero_scr, sem, zsem):
    j = pl.program_id(0)
    n_steps = pl.num_programs(0)
    tile = dest_ref.shape[2]
    n_exp = pads_ref.shape[1]

    def pad_copies(e, wait):
        off = pads_ref[0, e]
        cnt = pads_ref[1, e]
        for sz in _pad_sizes():
            cp = pltpu.make_async_copy(zero_scr.at[pl.ds(0, sz)], xs_hbm.at[pl.ds(off, sz)], zsem)

            @pl.when((cnt & sz) != 0)
            def _():
                if wait:
                    cp.wait()
                else:
                    cp.start()
            off = off + (cnt & sz)

    half = zero_scr.shape[0]
    n_blocks = xs_hbm.shape[0] // (2 * half)
    n_used = pads_ref[2, n_exp - 1]

    def tail_copies(blk, wait):
        for part in range(2):
            cp = pltpu.make_async_copy(
                zero_scr, xs_hbm.at[pl.ds((blk * 2 + part) * half, half)], zsem)
            if wait:
                cp.wait()
            else:
                cp.start()

    @pl.when(j == 0)
    def _():
        zero_scr[...] = jnp.zeros_like(zero_scr)

        def start(e, _):
            pad_copies(e, False)
            return 0
        lax.fori_loop(0, n_exp, start, 0)

        def start_tail(blk, _):
            tail_copies(blk, False)
            return 0
        lax.fori_loop(n_used, n_blocks, start_tail, 0)

    for sub in range(dest_ref.shape[0]):
        def issue(i, _, sub=sub):
            for kk in range(TOP_K):
                pltpu.make_async_copy(h2_ref.at[sub * tile + i],
                                      xs_hbm.at[dest_ref[sub, kk, i]], sem).start()
            return 0
        lax.fori_loop(0, tile, issue, 0)

    for kk in range(TOP_K):
        pltpu.make_async_copy(h2_ref, xs_hbm.at[pl.ds(0, h2_ref.shape[0])], sem).wait()

    @pl.when(j == n_steps - 1)
    def _():
        def finish(e, _):
            pad_copies(e, True)
            return 0
        lax.fori_loop(0, n_exp, finish, 0)

        def finish_tail(blk, _):
            tail_copies(blk, True)
            return 0
        lax.fori_loop(n_used, n_blocks, finish_tail, 0)


def _dispatch(pads, dest, h2_rows, n_rows):
    n_tok, _, d = h2_rows.shape
    n_tiles = dest.shape[0]
    sub = DISPATCH_SUBTILES
    grid_spec = pltpu.PrefetchScalarGridSpec(
        num_scalar_prefetch=1,
        grid=(n_tiles // sub,),
        in_specs=[pl.BlockSpec((sub, TOP_K, TOKEN_TILE), lambda j, p: (j, 0, 0),
                               memory_space=pltpu.SMEM),
                  pl.BlockSpec((sub * TOKEN_TILE, 1, d), lambda j, p: (j, 0, 0))],
        out_specs=pl.BlockSpec(memory_space=pl.ANY),
        scratch_shapes=[pltpu.VMEM((EXPERT_ROWS // 2, 1, d), F32),
                        pltpu.SemaphoreType.DMA(()),
                        pltpu.SemaphoreType.DMA(())],
    )
    return pl.pallas_call(
        _dispatch_kernel,
        grid_spec=grid_spec,
        out_shape=jax.ShapeDtypeStruct((n_rows, 1, d), F32),
        compiler_params=_params(("arbitrary",)),
        name="dispatch",
    )(pads, dest, h2_rows)


def _experts_kernel(be_ref, nv_ref, x_ref, w1_ref, b1_ref, w2_ref, b2_ref, o_ref,
                    w1_scr, w2_scr, x_scr, act_scr):
    i = pl.program_id(0)
    prev = be_ref[jnp.maximum(i - 1, 0)]
    changed = (i == 0) | (be_ref[i] != prev)
    d, two_f = w1_scr.shape
    f = two_f // 2

    @pl.when(changed)
    def _():
        step = 64

        def cast1(r, _):
            rs = pl.ds(pl.multiple_of(r * step, step), step)
            w1_scr[rs, :] = w1_ref[0, rs, :].astype(BF16)
            return 0
        lax.fori_loop(0, d // step, cast1, 0)

        def cast2(r, _):
            rs = pl.ds(pl.multiple_of(r * step, step), step)
            w2_scr[rs, :] = w2_ref[0, rs, :].astype(BF16)
            return 0
        lax.fori_loop(0, f // step, cast2, 0)

    @pl.when(i < nv_ref[0])
    def _():
        x_scr[...] = x_ref[...].reshape(x_scr.shape)
        xb = x_scr[...].astype(BF16)
        width = 512
        for n in range(f // width):
            cg = slice(n * width, (n + 1) * width)
            cu = slice(f + n * width, f + (n + 1) * width)
            gate = jnp.dot(xb, w1_scr[:, cg], preferred_element_type=F32) + b1_ref[0, :, cg]
            up = jnp.dot(xb, w1_scr[:, cu], preferred_element_type=F32) + b1_ref[0, :, cu]
            gate = jnp.minimum(gate, SWIGLU_LIMIT)
            up = jnp.clip(up, -SWIGLU_LIMIT, SWIGLU_LIMIT)
            act_scr[:, cg] = (gate * jax.nn.sigmoid(SWIGLU_ALPHA * gate) * (up + 1.0)).astype(BF16)
        out = jnp.dot(act_scr[...], w2_scr[...], preferred_element_type=F32) + b2_ref[0]
        o_ref[...] = out.reshape(o_ref.shape)

    @pl.when(i >= nv_ref[0])
    def _():
        o_ref[...] = jnp.zeros_like(o_ref)


def _experts(block_expert, n_valid, xs, w1, b1, w2, b2):
    n_rows, _, d = xs.shape
    n_exp, _, two_f = w1.shape
    f = two_f // 2
    rows = EXPERT_ROWS
    n_blocks = n_rows // rows
    grid_spec = pltpu.PrefetchScalarGridSpec(
        num_scalar_prefetch=2,
        grid=(n_blocks,),
        in_specs=[pl.BlockSpec((rows, 1, d), lambda i, be, nv: (jnp.minimum(i, nv[0] - 1), 0, 0)),
                  pl.BlockSpec((1, d, two_f), lambda i, be, nv: (be[i], 0, 0)),
                  pl.BlockSpec((1, 1, two_f), lambda i, be, nv: (be[i], 0, 0)),
                  pl.BlockSpec((1, f, d), lambda i, be, nv: (be[i], 0, 0)),
                  pl.BlockSpec((1, 1, d), lambda i, be, nv: (be[i], 0, 0))],
        out_specs=pl.BlockSpec((rows, 1, d), lambda i, be, nv: (i, 0, 0)),
        scratch_shapes=[pltpu.VMEM((d, two_f), BF16),
                        pltpu.VMEM((f, d), BF16),
                        pltpu.VMEM((rows, d), F32),
                        pltpu.VMEM((rows, f), BF16)],
    )
    return pl.pallas_call(
        _experts_kernel,
        grid_spec=grid_spec,
        out_shape=jax.ShapeDtypeStruct((n_rows, 1, d), F32),
        compiler_params=_params(("arbitrary",)),
        name="experts",
    )(block_expert, n_valid, xs, w1, b1.reshape(n_exp, 1, two_f), w2, b2.reshape(n_exp, 1, d))


def _combine_kernel(dest_ref, ys_hbm, gate_ref, x1_ref, g2_ref, fn_ref, o_ref,
                    b0, b1, b2, b3, flat_scr, sem):
    tile = dest_ref.shape[2]
    bufs = (b0, b1, b2, b3)

    def issue(i, _):
        for kk in range(TOP_K):
            pltpu.make_async_copy(ys_hbm.at[dest_ref[0, kk, i]], bufs[kk].at[i], sem).start()
        return 0
    lax.fori_loop(0, tile, issue, 0)
    for kk in range(TOP_K):
        pltpu.make_async_copy(ys_hbm.at[pl.ds(0, tile)], bufs[kk], sem).wait()
    y = None
    for kk in range(TOP_K):
        flat_scr[...] = bufs[kk][...].reshape(flat_scr.shape)
        term = gate_ref[:, kk:kk + 1] * flat_scr[...]
        y = term if y is None else y + term
    x2 = x1_ref[...] + g2_ref[0] * y
    o_ref[...] = _rms(x2, fn_ref[...])


def _combine(dest, ys, gates_tok, x1, mod3, final_norm, seq):
    n_tok, d = x1.shape
    n_tiles = dest.shape[0]
    tiles_per_sample = seq // TOKEN_TILE
    return pl.pallas_call(
        _combine_kernel,
        grid=(n_tiles,),
        in_specs=[pl.BlockSpec((1, TOP_K, TOKEN_TILE), lambda j: (j, 0, 0),
                               memory_space=pltpu.SMEM),
                  pl.BlockSpec(memory_space=pl.ANY),
                  pl.BlockSpec((TOKEN_TILE, TOP_K), lambda j: (j, 0)),
                  pl.BlockSpec((TOKEN_TILE, d), lambda j: (j, 0)),
                  pl.BlockSpec((1, 1, d), lambda j: ((j // tiles_per_sample) * 6 + 5, 0, 0)),
                  pl.BlockSpec((1, d), lambda j: (0, 0))],
        out_specs=pl.BlockSpec((TOKEN_TILE, d), lambda j: (j, 0)),
        out_shape=jax.ShapeDtypeStruct((n_tok, d), F32),
        scratch_shapes=[pltpu.VMEM((TOKEN_TILE, 1, d), F32) for _ in range(TOP_K)]
        + [pltpu.VMEM((TOKEN_TILE, d), F32), pltpu.SemaphoreType.DMA(())],
        compiler_params=_params(("arbitrary",)),
        name="combine",
    )(dest, ys, gates_tok, x1, mod3, final_norm)


def kernel(x, c, ctx, c_ctx, w_mod, b_mod, norm1, w_in, sgu_ln, sgu_w, sgu_b, lb_fwd, lb_bwd,
           hgrn_norm, w_out, norm2, router_w, router_b, w1, b1, w2, b2, final_norm):
    depth = w_mod.shape[0]
    assert depth == 1, "single-layer block only"
    bsz, seq, d = x.shape
    assert bsz + 1 <= MOD_ROWS
    width = w_in.shape[2] // N_PROJ
    n_exp = router_w.shape[2]
    n_tok = bsz * seq

    cc = jnp.zeros((MOD_ROWS, d), F32).at[:bsz].set(c).at[bsz].set(c_ctx)
    mod = _modulation(cc, w_mod[0], b_mod[0])
    mod3 = mod.reshape(MOD_ROWS * 6, 1, d)

    w_in_b = w_in[0].astype(BF16)
    norm1_l = norm1[0].reshape(1, d)
    proj_ctx = _inproj(ctx, mod3, lambda b: bsz, norm1_l, w_in_b[:, 2 * width:6 * width],
                       ctx.shape[1], "inproj_ctx")
    states = _scan(proj_ctx, 0, 1, 2, 3, lb_fwd, lb_bwd, None, False, "scan_ctx")
    proj = _inproj(x, mod3, lambda b: b, norm1_l, w_in_b, INPROJ_ROWS, "inproj")
    o = _scan(proj, 2, 3, 4, 5, lb_fwd, lb_bwd, states, True, "scan")

    x1, h2_rows, logits = _postmix(
        proj, o, x, mod3, sgu_ln[0], sgu_w[0].astype(BF16),
        sgu_b[0].reshape(sgu_b.shape[1], sgu_b.shape[2], 1), hgrn_norm[0].reshape(1, -1),
        w_out[0].astype(BF16), norm2[0].reshape(1, d), router_w[0], router_b[0].reshape(1, n_exp))

    n_tiles = n_tok // TOKEN_TILE
    n_blocks = -(-(n_tok * TOP_K + n_exp * (EXPERT_ROWS - 1)) // EXPERT_ROWS)
    n_blocks_pad = -(-n_blocks // 128) * 128
    lt3 = logits.reshape(n_tiles, TOKEN_TILE, n_exp).transpose(0, 2, 1)
    dest, gates, block_expert, meta, pads = _route(lt3, n_blocks_pad)

    xs = _dispatch(pads[:, :, 0], dest, h2_rows, n_blocks * EXPERT_ROWS)
    ys = _experts(block_expert[0], meta[0, :1], xs, w1[0], b1[0], w2[0], b2[0])
    gates_tok = gates.transpose(0, 2, 1).reshape(n_tok, TOP_K)
    out = _combine(dest, ys, gates_tok, x1.reshape(n_tok, d), mod3, final_norm.reshape(1, d), seq)
    return out.reshape(bsz, seq, d)
```

```python
import functools
import math

import jax
import jax.numpy as jnp
import numpy as np
from jax import lax
from jax.experimental import pallas as pl
from jax.experimental.pallas import tpu as pltpu

F32 = jnp.float32
BF16 = jnp.bfloat16
EPS = 1e-6

HEAD_DIM = 128
SGU_CHUNK = 128
N_PROJ = 7
TOP_K = 4
SWIGLU_LIMIT = 7.0
SWIGLU_ALPHA = 1.702

SCAN_CHUNK = 64
INPROJ_ROWS = 512
POSTMIX_ROWS = 256
EXPERT_ROWS = 256
TOKEN_TILE = 128
DISPATCH_SUBTILES = 4
MOD_ROWS = 16
VMEM_LIMIT = 56 * 1024 * 1024


def _params(sem, vmem=None):
    return pltpu.CompilerParams(dimension_semantics=sem, vmem_limit_bytes=vmem or VMEM_LIMIT)


def _mod_kernel(c_ref, w_ref, b_ref, o_ref):
    c = c_ref[...]
    s = c * jax.nn.sigmoid(c)
    o_ref[...] = jnp.dot(s, w_ref[...], precision=lax.Precision.HIGHEST,
                         preferred_element_type=F32) + b_ref[...]


def _modulation(cc, w_mod, b_mod):
    d, n6 = w_mod.shape
    tn = 1024
    return pl.pallas_call(
        _mod_kernel,
        grid=(n6 // tn,),
        in_specs=[pl.BlockSpec((MOD_ROWS, d), lambda j: (0, 0)),
                  pl.BlockSpec((d, tn), lambda j: (0, j)),
                  pl.BlockSpec((1, tn), lambda j: (0, j))],
        out_specs=pl.BlockSpec((MOD_ROWS, tn), lambda j: (0, j)),
        out_shape=jax.ShapeDtypeStruct((MOD_ROWS, n6), F32),
        compiler_params=_params(("arbitrary",)),
        name="mod",
    )(cc, w_mod, b_mod.reshape(1, n6))


def _rms(x, g):
    return x * lax.rsqrt(jnp.mean(x * x, axis=-1, keepdims=True) + EPS) * g


def _inproj_kernel(x_ref, sh_ref, sc_ref, g_ref, w_ref, o_ref, *, width):
    h = _rms(x_ref[0], g_ref[...])
    h = h * (1.0 + sc_ref[0]) + sh_ref[0]
    hb = h.astype(BF16)
    for j in range(w_ref.shape[1] // width):
        cols = slice(j * width, (j + 1) * width)
        o_ref[0, :, cols] = jnp.dot(hb, w_ref[:, cols],
                                    preferred_element_type=F32).astype(o_ref.dtype)


def _inproj(x, mod3, mod_row, norm_g, w_bf16, rows, name):
    bsz, n, d = x.shape
    wcols = w_bf16.shape[1]
    return pl.pallas_call(
        functools.partial(_inproj_kernel, width=512),
        grid=(bsz, n // rows),
        in_specs=[pl.BlockSpec((1, rows, d), lambda b, j: (b, j, 0)),
                  pl.BlockSpec((1, 1, d), lambda b, j: (mod_row(b) * 6 + 0, 0, 0)),
                  pl.BlockSpec((1, 1, d), lambda b, j: (mod_row(b) * 6 + 1, 0, 0)),
                  pl.BlockSpec((1, d), lambda b, j: (0, 0)),
                  pl.BlockSpec((d, wcols), lambda b, j: (0, 0))],
        out_specs=pl.BlockSpec((1, rows, wcols), lambda b, j: (b, j, 0)),
        out_shape=jax.ShapeDtypeStruct((bsz, n, wcols), BF16),
        compiler_params=_params(("arbitrary", "arbitrary")),
        name=name,
    )(x, mod3, mod3, norm_g, w_bf16)


def _scan_constants(length):
    levels = int(math.log2(length))
    assert 1 << levels == length
    t = np.arange(length)
    sums = np.zeros((2, levels + 1, length, length), np.float32)
    masks = np.zeros((2, levels + 1, length, length), np.float32)
    for lv in range(levels):
        half = 1 << lv
        blk = t // (2 * half)
        mid = blk * 2 * half + half
        later = (t % (2 * half)) >= half
        for r in range(length):
            if later[r]:
                sums[0, lv, r, mid[r] + 1:r + 1] = 1.0
            else:
                sums[0, lv, r, r + 1:mid[r] + 1] = 1.0
        masks[0, lv] = (blk[:, None] == blk[None, :]) & later[:, None] & (~later)[None, :]
    sums[0, levels] = np.tril(np.ones((length, length), np.float32))
    masks[0, levels] = np.eye(length, dtype=np.float32)
    sums[1] = sums[0][:, ::-1, ::-1]
    masks[1] = masks[0][:, ::-1, ::-1]
    n_lv = levels + 1
    big = np.zeros((n_lv, 2, length, 2, 2, length), np.float32)
    pair = np.zeros((n_lv, 2, length, 2, length), np.float32)
    for d in range(2):
        big[:, d, :, 0, d, :] = sums[d]
        big[:, d, :, 1, d, :] = sums[d]
        pair[:, d, :, d, :] = masks[d]
    return big.reshape(n_lv * 2 * length, 4 * length), pair.reshape(n_lv, 2 * length, 2 * length)


def _nt_dot(a, b):
    return lax.dot_general(a, b, (((1,), (1,)), ((), ())), preferred_element_type=F32)


def _lower_bound(tab):
    ex = jnp.exp(tab - jnp.max(tab, axis=0, keepdims=True))
    return ex[0:1] / jnp.sum(ex, axis=0, keepdims=True)


def _scan_kernel(*refs, heads, emit_o, has_init):
    qf_ref, zf_ref, if_ref, qb_ref, zb_ref, ib_ref, tabf_ref, tabb_ref, sums_ref, masks_ref = refs[:10]
    pos = 10
    init_ref = None
    if has_init:
        init_ref = refs[pos]
        pos += 1
    n_out = 2 if emit_o else 1
    out_refs = refs[pos:pos + n_out]
    state_ref, d_scr = refs[pos + n_out:]

    c = pl.program_id(1)
    n_chunks = pl.num_programs(1)
    length = qf_ref.shape[1]
    width = qf_ref.shape[2]
    levels = masks_ref.shape[0] - 1
    rows2 = 2 * length

    @pl.when(c == 0)
    def _():
        if has_init:
            state_ref[...] = init_ref[0]
        else:
            state_ref[...] = jnp.zeros_like(state_ref)

    def both(a, b):
        return jnp.concatenate([a, b], axis=0)

    def per_dir(a, b):
        return both(jnp.broadcast_to(a, (length, a.shape[1])),
                    jnp.broadcast_to(b, (length, b.shape[1])))

    is_fwd = lax.broadcasted_iota(jnp.int32, (rows2, HEAD_DIM), 0) < length

    def split_dirs(a):
        return jnp.concatenate([jnp.where(is_fwd, a, 0.0), jnp.where(is_fwd, 0.0, a)], axis=1)

    lb = per_dir(_lower_bound(tabf_ref[...]), _lower_bound(tabb_ref[...]))
    z = both(zf_ref[0], zb_ref[0]).astype(F32)
    q = both(qf_ref[0], qb_ref[0]).astype(F32)
    v = both(if_ref[0], ib_ref[0])
    f = lb + (1.0 - lb) * jax.nn.sigmoid(z)
    k = 1.0 - f
    g = jnp.log(f)
    g_hi = g.astype(BF16)
    g_lo = (g - g_hi.astype(F32)).astype(BF16)
    g_parts = jnp.concatenate([g_hi, g_lo], axis=0)
    if emit_o:
        d_scr[...] = jnp.dot(sums_ref[...], g_parts, preferred_element_type=F32)
    else:
        d_scr[...] = jnp.dot(sums_ref[levels * rows2:(levels + 1) * rows2, :], g_parts,
                             preferred_element_type=F32)
    prefix_rows = slice(levels * rows2, (levels + 1) * rows2) if emit_o else slice(0, rows2)
    bp = d_scr[prefix_rows, :]
    btot_f = bp[length - 1:length]
    btot_b = bp[length:length + 1]

    if emit_o:
        of_ref, ob_ref = out_refs
        scores = [None] * heads
        for lv in range(levels + 1):
            if lv == levels:
                xq, yk = q.astype(BF16), k.astype(BF16)
            else:
                e = jnp.exp(d_scr[lv * rows2:(lv + 1) * rows2, :])
                xq, yk = (q * e).astype(BF16), (k * e).astype(BF16)
            for h in range(heads):
                cols = slice(h * HEAD_DIM, (h + 1) * HEAD_DIM)
                p = masks_ref[lv] * _nt_dot(xq[:, cols], yk[:, cols])
                scores[h] = p if scores[h] is None else scores[h] + p
        qd = q * jnp.exp(bp)
        for h in range(heads):
            cols = slice(h * HEAD_DIM, (h + 1) * HEAD_DIM)
            o = (jnp.dot(scores[h].astype(BF16), v[:, cols], preferred_element_type=F32)
                 + _nt_dot(split_dirs(qd[:, cols]).astype(BF16), state_ref[h].astype(BF16)))
            of_ref[0, :, cols] = o[:length].astype(of_ref.dtype)
            ob_ref[0, :, cols] = o[length:].astype(ob_ref.dtype)

    kd = k * jnp.exp(per_dir(btot_f, btot_b) - bp)
    dec_f = jnp.exp(btot_f)
    dec_b = jnp.exp(btot_b)
    for h in range(heads):
        cols = slice(h * HEAD_DIM, (h + 1) * HEAD_DIM)
        dec = jnp.concatenate([dec_f[:, cols], dec_b[:, cols]], axis=1)
        vt = v[:, cols].astype(F32).T.astype(BF16)
        state_ref[h] = state_ref[h] * dec + jnp.dot(
            vt, split_dirs(kd[:, cols]).astype(BF16), preferred_element_type=F32)

    if not emit_o:
        @pl.when(c == n_chunks - 1)
        def _():
            out_refs[0][0] = state_ref[...]


def _scan(proj, col_q, col_zf, col_zb, col_i, tab_f, tab_b, init, emit_o, name):
    bsz, n, _ = proj.shape
    width = tab_f.shape[1]
    heads = width // HEAD_DIM
    length = SCAN_CHUNK
    n_chunks = n // length
    sums_np, masks_np = _scan_constants(length)
    sums = jnp.asarray(sums_np, BF16)
    masks = jnp.asarray(masks_np, F32)
    d_rows = sums.shape[0] if emit_o else 2 * length

    def fwd(col):
        return pl.BlockSpec((1, length, width), lambda b, c: (b, c, col))

    def bwd(col):
        return pl.BlockSpec((1, length, width), lambda b, c: (b, n_chunks - 1 - c, col))

    def whole(a):
        nd = a.ndim
        return pl.BlockSpec(a.shape, lambda b, c: (0,) * nd)

    in_specs = [fwd(col_q), fwd(col_zf), fwd(col_i), bwd(col_q), bwd(col_zb), bwd(col_i),
                whole(tab_f), whole(tab_b), whole(sums), whole(masks)]
    args = [proj, proj, proj, proj, proj, proj, tab_f, tab_b, sums, masks]
    state_shape = (heads, HEAD_DIM, 2 * HEAD_DIM)
    if init is not None:
        in_specs.append(pl.BlockSpec((1,) + state_shape, lambda b, c: (b, 0, 0, 0)))
        args.append(init)
    if emit_o:
        out_spec = [pl.BlockSpec((1, length, width), lambda b, c: (b, c, 0)),
                    pl.BlockSpec((1, length, width), lambda b, c: (b, n_chunks - 1 - c, 0))]
        out_shape = [jax.ShapeDtypeStruct((bsz, n, width), BF16)] * 2
    else:
        out_spec = pl.BlockSpec((1,) + state_shape, lambda b, c: (b, 0, 0, 0))
        out_shape = jax.ShapeDtypeStruct((bsz,) + state_shape, F32)
    return pl.pallas_call(
        functools.partial(_scan_kernel, heads=heads, emit_o=emit_o, has_init=init is not None),
        grid=(bsz, n_chunks),
        in_specs=in_specs,
        out_specs=out_spec,
        out_shape=out_shape,
        scratch_shapes=[pltpu.VMEM(state_shape, F32), pltpu.VMEM((d_rows, width), F32)],
        compiler_params=_params(("arbitrary", "arbitrary")),
        name=name,
    )(*args)


def _postmix_kernel(u_ref, v_ref, g_ref, of_ref, ob_ref, x_ref, g1_ref, sh2_ref, sc2_ref,
                    ln_ref, ws_ref, bs_ref, hn_ref, wout_ref, n2_ref, rw_ref, rb_ref,
                    x1_ref, h2_ref, lg_ref, y_scr):
    rows = x_ref.shape[1]
    heads = ln_ref.shape[0]
    sgu_w = heads * HEAD_DIM
    u = jax.nn.gelu(u_ref[0].astype(F32))
    v = jax.nn.gelu(v_ref[0].astype(F32))
    for ch in range(rows // SGU_CHUNK):
        rs = slice(ch * SGU_CHUNK, (ch + 1) * SGU_CHUNK)
        for h in range(heads):
            cols = slice(h * HEAD_DIM, (h + 1) * HEAD_DIM)
            vh = v[rs, cols]
            mu = jnp.mean(vh, axis=-1, keepdims=True)
            var = jnp.mean(jnp.square(vh - mu), axis=-1, keepdims=True)
            vn = (vh - mu) * lax.rsqrt(var + EPS) * ln_ref[h:h + 1, :]
            z = jnp.dot(ws_ref[h], vn.astype(BF16), preferred_element_type=F32) + bs_ref[h]
            y_scr[rs, cols] = (u[rs, cols] * z).astype(BF16)
    o = of_ref[0].astype(F32) + ob_ref[0].astype(F32)
    gate = g_ref[0].astype(F32)
    gate = gate * jax.nn.sigmoid(gate)
    for h in range(o.shape[1] // HEAD_DIM):
        cols = slice(h * HEAD_DIM, (h + 1) * HEAD_DIM)
        oh = o[:, cols]
        on = oh * lax.rsqrt(jnp.mean(oh * oh, axis=-1, keepdims=True) + EPS)
        y_scr[:, sgu_w + h * HEAD_DIM:sgu_w + (h + 1) * HEAD_DIM] = (
            on * hn_ref[:, cols] * gate[:, cols]).astype(BF16)
    y = jnp.dot(y_scr[...], wout_ref[...], preferred_element_type=F32)
    x1 = x_ref[0] + g1_ref[0] * y
    x1_ref[0] = x1
    h2 = _rms(x1, n2_ref[...]) * (1.0 + sc2_ref[0]) + sh2_ref[0]
    h2_ref[...] = h2.reshape(h2_ref.shape)
    n_exp = rb_ref.shape[0]
    h_hi = h2.astype(BF16)
    h_lo = (h2 - h_hi.astype(F32)).astype(BF16)
    part = _nt_dot(rw_ref[...], h_hi)
    lg = part[:n_exp] + part[n_exp:] + _nt_dot(rw_ref[:n_exp, :], h_lo) + rb_ref[...]
    for s in range(rows // TOKEN_TILE):
        lg_ref[s] = lg[:, s * TOKEN_TILE:(s + 1) * TOKEN_TILE]


def _postmix(proj, o_fwd, o_bwd, x, mod3, sgu_ln, sgu_w_bf16, sgu_b3, hgrn_norm, w_out_bf16,
             norm2, router_w, router_b):
    bsz, n, d = x.shape
    width = o_fwd.shape[2]
    rows = POSTMIX_ROWS
    n_exp = router_b.shape[0]
    nj = n // rows
    sub = rows // TOKEN_TILE

    def colgroup(col):
        return pl.BlockSpec((1, rows, width), lambda b, j: (b, j, col))

    def modrow(idx):
        return pl.BlockSpec((1, 1, d), lambda b, j: (b * 6 + idx, 0, 0))

    def whole(a):
        nd = a.ndim
        return pl.BlockSpec(a.shape, lambda b, j: (0,) * nd)

    return pl.pallas_call(
        _postmix_kernel,
        grid=(bsz, nj),
        in_specs=[colgroup(0), colgroup(1), colgroup(6), colgroup(0), colgroup(0),
                  pl.BlockSpec((1, rows, d), lambda b, j: (b, j, 0)),
                  modrow(2), modrow(3), modrow(4),
                  whole(sgu_ln), whole(sgu_w_bf16), whole(sgu_b3), whole(hgrn_norm),
                  whole(w_out_bf16), whole(norm2), whole(router_w), whole(router_b)],
        out_specs=[pl.BlockSpec((1, rows, d), lambda b, j: (b, j, 0)),
                   pl.BlockSpec((rows, 1, d), lambda b, j: (b * nj + j, 0, 0)),
                   pl.BlockSpec((sub, n_exp, TOKEN_TILE), lambda b, j: (b * nj + j, 0, 0))],
        out_shape=[jax.ShapeDtypeStruct((bsz, n, d), F32),
                   jax.ShapeDtypeStruct((bsz * n, 1, d), F32),
                   jax.ShapeDtypeStruct((bsz * n // TOKEN_TILE, n_exp, TOKEN_TILE), F32)],
        scratch_shapes=[pltpu.VMEM((rows, d), BF16)],
        compiler_params=_params(("arbitrary", "arbitrary")),
        name="postmix",
    )(proj, proj, proj, o_fwd, o_bwd, x, mod3, mod3, mod3, sgu_ln, sgu_w_bf16, sgu_b3, hgrn_norm,
      w_out_bf16, norm2, router_w, router_b)


def _route_kernel(lt_ref, upper_ref, ones_ref, ltri_ref,
                  dest_ref, gate_ref, be_ref, meta_ref, pads_ref, pos_scr, eidx_scr, *, block_rows):
    n_tiles, n_exp, lanes = lt_ref.shape
    iota_e = lax.broadcasted_iota(jnp.int32, (n_exp, lanes), 0).astype(F32)

    def tile_body(j, carry):
        l = lt_ref[j]
        picks, vals = [], []
        for _ in range(TOP_K):
            m = jnp.max(l, axis=0, keepdims=True)
            idx = jnp.min(jnp.where(l == m, iota_e, float(n_exp)), axis=0, keepdims=True)
            pick = iota_e == idx
            picks.append(pick)
            vals.append(m)
            l = jnp.where(pick, -jnp.inf, l)
            eidx_scr[j, len(picks) - 1:len(picks), :] = idx
        exps = [jnp.exp(v - vals[0]) for v in vals]
        tot = exps[0]
        for e in exps[1:]:
            tot = tot + e
        for kk in range(TOP_K):
            gate_ref[j, kk:kk + 1, :] = exps[kk] / tot
        chosen = picks[0]
        for p in picks[1:]:
            chosen = chosen | p
        cb = jnp.where(chosen, 1.0, 0.0).astype(BF16)
        before = carry + jnp.dot(cb, upper_ref[...], preferred_element_type=F32)
        for kk in range(TOP_K):
            pos_scr[j, kk:kk + 1, :] = jnp.sum(jnp.where(picks[kk], before, 0.0),
                                               axis=0, keepdims=True)
        return carry + jnp.dot(cb, ones_ref[...], preferred_element_type=F32)

    counts = lax.fori_loop(0, n_tiles, tile_body, jnp.zeros((n_exp, lanes), F32))
    n_blk = jnp.ceil(counts * (1.0 / block_rows))
    end_blk = jnp.dot(ltri_ref[...], n_blk.astype(BF16), preferred_element_type=F32)
    start_row = (end_blk - n_blk) * block_rows

    def dest_body(j, _):
        for kk in range(TOP_K):
            idx = eidx_scr[j, kk:kk + 1, :]
            base = jnp.sum(jnp.where(iota_e == idx, start_row, 0.0), axis=0, keepdims=True)
            dest_ref[j, kk:kk + 1, :] = (base + pos_scr[j, kk:kk + 1, :]).astype(jnp.int32)
        return 0

    lax.fori_loop(0, n_tiles, dest_body, 0)

    for piece in range(be_ref.shape[1] // lanes):
        blk = (lax.broadcasted_iota(jnp.int32, (n_exp, lanes), 1) + piece * lanes).astype(F32)
        owner = jnp.sum(jnp.where(blk >= end_blk, 1.0, 0.0), axis=0, keepdims=True)
        be_ref[:, piece * lanes:(piece + 1) * lanes] = jnp.minimum(
            owner, float(n_exp - 1)).astype(jnp.int32)
    meta_ref[...] = end_blk[n_exp - 1:n_exp, :].astype(jnp.int32)
    pads_ref[0] = (start_row + counts).astype(jnp.int32)
    pads_ref[1] = (n_blk * block_rows - counts).astype(jnp.int32)
    pads_ref[2] = end_blk.astype(jnp.int32)


def _route(lt3, n_blocks_pad):
    n_tiles, n_exp, lanes = lt3.shape
    upper = jnp.asarray(np.triu(np.ones((lanes, lanes), np.float32), 1), BF16)
    ones = jnp.ones((lanes, lanes), BF16)
    ltri = jnp.asarray(np.tril(np.ones((n_exp, n_exp), np.float32)), BF16)
    vm = pl.BlockSpec(memory_space=pltpu.VMEM)
    return pl.pallas_call(
        functools.partial(_route_kernel, block_rows=EXPERT_ROWS),
        in_specs=[vm, vm, vm, vm],
        out_specs=[vm, vm, vm, vm, vm],
        out_shape=[jax.ShapeDtypeStruct((n_tiles, TOP_K, lanes), jnp.int32),
                   jax.ShapeDtypeStruct((n_tiles, TOP_K, lanes), F32),
                   jax.ShapeDtypeStruct((1, n_blocks_pad), jnp.int32),
                   jax.ShapeDtypeStruct((1, lanes), jnp.int32),
                   jax.ShapeDtypeStruct((3, n_exp, lanes), jnp.int32)],
        scratch_shapes=[pltpu.VMEM((n_tiles, TOP_K, lanes), F32),
                        pltpu.VMEM((n_tiles, TOP_K, lanes), F32)],
        compiler_params=pltpu.CompilerParams(vmem_limit_bytes=VMEM_LIMIT),
        name="route",
    )(lt3, upper, ones, ltri)


def _pad_sizes():
    return [1 << b for b in reversed(range(int(math.log2(EXPERT_ROWS))))]


def _dispatch_kernel(pads_ref, dest_ref, h2_ref, xs_hbm, zero_scr, sem, zsem):
    j = pl.program_id(0)
    n_steps = pl.num_programs(0)
    tile = dest_ref.shape[2]
    n_exp = pads_ref.shape[1]

    def pad_copies(e, wait):
        off = pads_ref[0, e]
        cnt = pads_ref[1, e]
        for sz in _pad_sizes():
            cp = pltpu.make_async_copy(zero_scr.at[pl.ds(0, sz)], xs_hbm.at[pl.ds(off, sz)], zsem)

            @pl.when((cnt & sz) != 0)
            def _():
                if wait:
                    cp.wait()
                else:
                    cp.start()
            off = off + (cnt & sz)

    half = zero_scr.shape[0]
    n_blocks = xs_hbm.shape[0] // (2 * half)
    n_used = pads_ref[2, n_exp - 1]

    def tail_copies(blk, wait):
        for part in range(2):
            cp = pltpu.make_async_copy(
                zero_scr, xs_hbm.at[pl.ds((blk * 2 + part) * half, half)], zsem)
            if wait:
                cp.wait()
            else:
                cp.start()

    @pl.when(j == 0)
    def _():
        zero_scr[...] = jnp.zeros_like(zero_scr)

        def start(e, _):
            pad_copies(e, False)
            return 0
        lax.fori_loop(0, n_exp, start, 0)

        def start_tail(blk, _):
            tail_copies(blk, False)
            return 0
        lax.fori_loop(n_used, n_blocks, start_tail, 0)

    for sub in range(dest_ref.shape[0]):
        def issue(i, _, sub=sub):
            for kk in range(TOP_K):
                pltpu.make_async_copy(h2_ref.at[sub * tile + i],
                                      xs_hbm.at[dest_ref[sub, kk, i]], sem).start(priority=kk % 2)
            return 0
        lax.fori_loop(0, tile, issue, 0)

    for kk in range(TOP_K):
        pltpu.make_async_copy(h2_ref, xs_hbm.at[pl.ds(0, h2_ref.shape[0])], sem).wait()

    @pl.when(j == n_steps - 1)
    def _():
        def finish(e, _):
            pad_copies(e, True)
            return 0
        lax.fori_loop(0, n_exp, finish, 0)

        def finish_tail(blk, _):
            tail_copies(blk, True)
            return 0
        lax.fori_loop(n_used, n_blocks, finish_tail, 0)


def _dispatch(pads, dest, h2_rows, n_rows):
    n_tok, _, d = h2_rows.shape
    n_tiles = dest.shape[0]
    sub = DISPATCH_SUBTILES
    grid_spec = pltpu.PrefetchScalarGridSpec(
        num_scalar_prefetch=1,
        grid=(n_tiles // sub,),
        in_specs=[pl.BlockSpec((sub, TOP_K, TOKEN_TILE), lambda j, p: (j, 0, 0),
                               memory_space=pltpu.SMEM),
                  pl.BlockSpec((sub * TOKEN_TILE, 1, d), lambda j, p: (j, 0, 0))],
        out_specs=pl.BlockSpec(memory_space=pl.ANY),
        scratch_shapes=[pltpu.VMEM((EXPERT_ROWS // 2, 1, d), F32),
                        pltpu.SemaphoreType.DMA(()),
                        pltpu.SemaphoreType.DMA(())],
    )
    return pl.pallas_call(
        _dispatch_kernel,
        grid_spec=grid_spec,
        out_shape=jax.ShapeDtypeStruct((n_rows, 1, d), F32),
        compiler_params=_params(("arbitrary",)),
        name="dispatch",
    )(pads, dest, h2_rows)


def _experts_kernel(be_ref, nv_ref, x_ref, w1_ref, b1_ref, w2_ref, b2_ref, o_ref,
                    w1_scr, w2_scr, x_scr, act_scr):
    i = pl.program_id(0)
    prev = be_ref[jnp.maximum(i - 1, 0)]
    changed = (i == 0) | (be_ref[i] != prev)
    d, two_f = w1_scr.shape
    f = two_f // 2

    @pl.when(changed)
    def _():
        step = 64

        def cast1(r, _):
            rs = pl.ds(pl.multiple_of(r * step, step), step)
            w1_scr[rs, :] = w1_ref[0, rs, :].astype(BF16)
            return 0
        lax.fori_loop(0, d // step, cast1, 0)

        def cast2(r, _):
            rs = pl.ds(pl.multiple_of(r * step, step), step)
            w2_scr[rs, :] = w2_ref[0, rs, :].astype(BF16)
            return 0
        lax.fori_loop(0, f // step, cast2, 0)

    @pl.when(i < nv_ref[0])
    def _():
        x_scr[...] = x_ref[...].reshape(x_scr.shape)
        xb = x_scr[...].astype(BF16)
        width = 512
        for n in range(f // width):
            cg = slice(n * width, (n + 1) * width)
            cu = slice(f + n * width, f + (n + 1) * width)
            gate = jnp.dot(xb, w1_scr[:, cg], preferred_element_type=F32) + b1_ref[0, :, cg]
            up = jnp.dot(xb, w1_scr[:, cu], preferred_element_type=F32) + b1_ref[0, :, cu]
            gate = jnp.minimum(gate, SWIGLU_LIMIT)
            up = jnp.clip(up, -SWIGLU_LIMIT, SWIGLU_LIMIT)
            act_scr[:, cg] = (gate * jax.nn.sigmoid(SWIGLU_ALPHA * gate) * (up + 1.0)).astype(BF16)
        out = jnp.dot(act_scr[...], w2_scr[...], preferred_element_type=F32) + b2_ref[0]
        o_ref[...] = out.reshape(o_ref.shape)

    @pl.when(i >= nv_ref[0])
    def _():
        o_ref[...] = jnp.zeros_like(o_ref)


def _experts(block_expert, n_valid, xs, w1, b1, w2, b2):
    n_rows, _, d = xs.shape
    n_exp, _, two_f = w1.shape
    f = two_f // 2
    rows = EXPERT_ROWS
    n_blocks = n_rows // rows
    grid_spec = pltpu.PrefetchScalarGridSpec(
        num_scalar_prefetch=2,
        grid=(n_blocks,),
        in_specs=[pl.BlockSpec((rows, 1, d), lambda i, be, nv: (jnp.minimum(i, nv[0] - 1), 0, 0)),
                  pl.BlockSpec((1, d, two_f), lambda i, be, nv: (be[i], 0, 0)),
                  pl.BlockSpec((1, 1, two_f), lambda i, be, nv: (be[i], 0, 0)),
                  pl.BlockSpec((1, f, d), lambda i, be, nv: (be[i], 0, 0)),
                  pl.BlockSpec((1, 1, d), lambda i, be, nv: (be[i], 0, 0))],
        out_specs=pl.BlockSpec((rows, 1, d), lambda i, be, nv: (i, 0, 0)),
        scratch_shapes=[pltpu.VMEM((d, two_f), BF16),
                        pltpu.VMEM((f, d), BF16),
                        pltpu.VMEM((rows, d), F32),
                        pltpu.VMEM((rows, f), BF16)],
    )
    return pl.pallas_call(
        _experts_kernel,
        grid_spec=grid_spec,
        out_shape=jax.ShapeDtypeStruct((n_rows, 1, d), F32),
        compiler_params=_params(("arbitrary",)),
        name="experts",
    )(block_expert, n_valid, xs, w1, b1.reshape(n_exp, 1, two_f), w2, b2.reshape(n_exp, 1, d))


def _combine_kernel(dest_ref, next_ref, ys_hbm, gate_ref, x1_ref, g2_ref, fn_ref, o_ref, *scratch):
    bufs = (scratch[:TOP_K], scratch[TOP_K:2 * TOP_K])
    flat_scr, sem = scratch[2 * TOP_K:]
    j = pl.program_id(0)
    n_steps = pl.num_programs(0)
    tile = dest_ref.shape[2]

    def gather(idx_ref, s):
        def issue(i, _):
            for kk in range(TOP_K):
                pltpu.make_async_copy(ys_hbm.at[idx_ref[0, kk, i]], bufs[s][kk].at[i],
                                      sem.at[s]).start(priority=kk % 2)
            return 0
        lax.fori_loop(0, tile, issue, 0)

    @pl.when(j == 0)
    def _():
        gather(dest_ref, 0)

    for s in range(2):
        @pl.when((j % 2 == s) & (j + 1 < n_steps))
        def _(s=s):
            gather(next_ref, 1 - s)

    for s in range(2):
        @pl.when(j % 2 == s)
        def _(s=s):
            for kk in range(TOP_K):
                pltpu.make_async_copy(ys_hbm.at[pl.ds(0, tile)], bufs[s][kk], sem.at[s]).wait()
            y = None
            for kk in range(TOP_K):
                flat_scr[...] = bufs[s][kk][...].reshape(flat_scr.shape)
                term = gate_ref[:, kk:kk + 1] * flat_scr[...]
                y = term if y is None else y + term
            x2 = x1_ref[...] + g2_ref[0] * y
            o_ref[...] = _rms(x2, fn_ref[...])


def _combine(dest, ys, gates_tok, x1, mod3, final_norm, seq):
    n_tok, d = x1.shape
    n_tiles = dest.shape[0]
    tiles_per_sample = seq // TOKEN_TILE
    return pl.pallas_call(
        _combine_kernel,
        grid=(n_tiles,),
        in_specs=[pl.BlockSpec((1, TOP_K, TOKEN_TILE), lambda j: (j, 0, 0),
                               memory_space=pltpu.SMEM),
                  pl.BlockSpec((1, TOP_K, TOKEN_TILE),
                               lambda j: (jnp.minimum(j + 1, n_tiles - 1), 0, 0),
                               memory_space=pltpu.SMEM),
                  pl.BlockSpec(memory_space=pl.ANY),
                  pl.BlockSpec((TOKEN_TILE, TOP_K), lambda j: (j, 0)),
                  pl.BlockSpec((TOKEN_TILE, d), lambda j: (j, 0)),
                  pl.BlockSpec((1, 1, d), lambda j: ((j // tiles_per_sample) * 6 + 5, 0, 0)),
                  pl.BlockSpec((1, d), lambda j: (0, 0))],
        out_specs=pl.BlockSpec((TOKEN_TILE, d), lambda j: (j, 0)),
        out_shape=jax.ShapeDtypeStruct((n_tok, d), F32),
        scratch_shapes=[pltpu.VMEM((TOKEN_TILE, 1, d), F32) for _ in range(2 * TOP_K)]
        + [pltpu.VMEM((TOKEN_TILE, d), F32), pltpu.SemaphoreType.DMA((2,))],
        compiler_params=_params(("arbitrary",)),
        name="combine",
    )(dest, dest, ys, gates_tok, x1, mod3, final_norm)


def kernel(x, c, ctx, c_ctx, w_mod, b_mod, norm1, w_in, sgu_ln, sgu_w, sgu_b, lb_fwd, lb_bwd,
           hgrn_norm, w_out, norm2, router_w, router_b, w1, b1, w2, b2, final_norm):
    depth = w_mod.shape[0]
    assert depth == 1, "single-layer block only"
    bsz, seq, d = x.shape
    assert bsz + 1 <= MOD_ROWS
    width = w_in.shape[2] // N_PROJ
    n_exp = router_w.shape[2]
    n_tok = bsz * seq

    cc = jnp.zeros((MOD_ROWS, d), F32).at[:bsz].set(c).at[bsz].set(c_ctx)
    mod = _modulation(cc, w_mod[0], b_mod[0])
    mod3 = mod.reshape(MOD_ROWS * 6, 1, d)

    w_in_b = w_in[0].astype(BF16)
    norm1_l = norm1[0].reshape(1, d)
    proj_ctx = _inproj(ctx, mod3, lambda b: bsz, norm1_l, w_in_b[:, 2 * width:6 * width],
                       ctx.shape[1], "inproj_ctx")
    states = _scan(proj_ctx, 0, 1, 2, 3, lb_fwd, lb_bwd, None, False, "scan_ctx")
    proj = _inproj(x, mod3, lambda b: b, norm1_l, w_in_b, INPROJ_ROWS, "inproj")
    o_fwd, o_bwd = _scan(proj, 2, 3, 4, 5, lb_fwd, lb_bwd, states, True, "scan")

    rw_t = router_w[0].T
    rw_hi = rw_t.astype(BF16)
    rw_parts = jnp.concatenate([rw_hi, (rw_t - rw_hi.astype(F32)).astype(BF16)], axis=0)
    x1, h2_rows, lt3 = _postmix(
        proj, o_fwd, o_bwd, x, mod3, sgu_ln[0], sgu_w[0].astype(BF16),
        sgu_b[0].reshape(sgu_b.shape[1], sgu_b.shape[2], 1), hgrn_norm[0].reshape(1, -1),
        w_out[0].astype(BF16), norm2[0].reshape(1, d), rw_parts, router_b[0].reshape(n_exp, 1))

    n_blocks = -(-(n_tok * TOP_K + n_exp * (EXPERT_ROWS - 1)) // EXPERT_ROWS)
    n_blocks_pad = -(-n_blocks // 128) * 128
    dest, gates, block_expert, meta, pads = _route(lt3, n_blocks_pad)

    xs = _dispatch(pads[:, :, 0], dest, h2_rows, n_blocks * EXPERT_ROWS)
    ys = _experts(block_expert[0], meta[0, :1], xs, w1[0], b1[0], w2[0], b2[0])
    gates_tok = gates.transpose(0, 2, 1).reshape(n_tok, TOP_K)
    out = _combine(dest, ys, gates_tok, x1.reshape(n_tok, d), mod3, final_norm.reshape(1, d), seq)
    return out.reshape(bsz, seq, d)
```

```python
import functools
import math

import jax
import jax.numpy as jnp
import numpy as np
from jax import lax
from jax.experimental import pallas as pl
from jax.experimental.pallas import tpu as pltpu

F32 = jnp.float32
BF16 = jnp.bfloat16
EPS = 1e-6
LOG2_E = 1.4426950408889634
SUBLANES = 8

HEAD_DIM = 128
SGU_CHUNK = 128
N_PROJ = 7
TOP_K = 4
SWIGLU_LIMIT = 7.0
SWIGLU_ALPHA = 1.702

SCAN_CHUNK = 64
SCAN_SAMPLES = 2
INPROJ_ROWS = 512
POSTMIX_ROWS = 256
EXPERT_ROWS = 256
TOKEN_TILE = 128
DISPATCH_SUBTILES = 4
MOD_ROWS = 16
VMEM_LIMIT = 56 * 1024 * 1024


def _params(sem, vmem=None):
    return pltpu.CompilerParams(dimension_semantics=sem, vmem_limit_bytes=vmem or VMEM_LIMIT)


def _sigmoid(x):
    return 1.0 / (1.0 + jnp.exp(-x))


def _mod_kernel(c_ref, w_ref, b_ref, o_ref):
    c = c_ref[...]
    s = c * _sigmoid(c)
    o_ref[...] = jnp.dot(s, w_ref[...], precision=lax.Precision.HIGHEST,
                         preferred_element_type=F32) + b_ref[...]


def _modulation(cc, w_mod, b_mod):
    d, n6 = w_mod.shape
    tn = 1024
    return pl.pallas_call(
        _mod_kernel,
        grid=(n6 // tn,),
        in_specs=[pl.BlockSpec((MOD_ROWS, d), lambda j: (0, 0)),
                  pl.BlockSpec((d, tn), lambda j: (0, j)),
                  pl.BlockSpec((1, tn), lambda j: (0, j))],
        out_specs=pl.BlockSpec((MOD_ROWS, tn), lambda j: (0, j)),
        out_shape=jax.ShapeDtypeStruct((MOD_ROWS, n6), F32),
        compiler_params=_params(("arbitrary",)),
        name="mod",
    )(cc, w_mod, b_mod.reshape(1, n6))


def _rms(x, g):
    return x * lax.rsqrt(jnp.mean(x * x, axis=-1, keepdims=True) + EPS) * g


def _inproj_kernel(x_ref, sh_ref, sc_ref, g_ref, w_ref, o_ref, *, width):
    h = _rms(x_ref[0], g_ref[...])
    h = h * (1.0 + sc_ref[0]) + sh_ref[0]
    hb = h.astype(BF16)
    for j in range(w_ref.shape[1] // width):
        cols = slice(j * width, (j + 1) * width)
        o_ref[0, :, cols] = jnp.dot(hb, w_ref[:, cols],
                                    preferred_element_type=F32).astype(o_ref.dtype)


def _inproj(x, mod3, mod_row, norm_g, w_bf16, rows, name):
    bsz, n, d = x.shape
    wcols = w_bf16.shape[1]
    return pl.pallas_call(
        functools.partial(_inproj_kernel, width=512),
        grid=(bsz, n // rows),
        in_specs=[pl.BlockSpec((1, rows, d), lambda b, j: (b, j, 0)),
                  pl.BlockSpec((1, 1, d), lambda b, j: (mod_row(b) * 6 + 0, 0, 0)),
                  pl.BlockSpec((1, 1, d), lambda b, j: (mod_row(b) * 6 + 1, 0, 0)),
                  pl.BlockSpec((1, d), lambda b, j: (0, 0)),
                  pl.BlockSpec((d, wcols), lambda b, j: (0, 0))],
        out_specs=pl.BlockSpec((1, rows, wcols), lambda b, j: (b, j, 0)),
        out_shape=jax.ShapeDtypeStruct((bsz, n, wcols), BF16),
        compiler_params=_params(("arbitrary", "arbitrary")),
        name=name,
    )(x, mod3, mod3, norm_g, w_bf16)


def _scan_constants(length):
    levels = int(math.log2(length))
    assert 1 << levels == length
    t = np.arange(length)
    sums = np.zeros((2, levels + 1, length, length), np.float32)
    masks = np.zeros((2, levels + 1, length, length), np.float32)
    for lv in range(levels):
        half = 1 << lv
        blk = t // (2 * half)
        mid = blk * 2 * half + half
        later = (t % (2 * half)) >= half
        for r in range(length):
            if later[r]:
                sums[0, lv, r, mid[r] + 1:r + 1] = 1.0
            else:
                sums[0, lv, r, r + 1:mid[r] + 1] = 1.0
        masks[0, lv] = (blk[:, None] == blk[None, :]) & later[:, None] & (~later)[None, :]
    sums[0, levels] = np.tril(np.ones((length, length), np.float32))
    masks[0, levels] = np.eye(length, dtype=np.float32)
    sums[1] = sums[0][:, ::-1, ::-1]
    masks[1] = masks[0][:, ::-1, ::-1]
    n_lv = levels + 1
    kept = [lv for lv in range(levels) if (1 << lv) < SUBLANES] + [levels]
    big = np.zeros((len(kept), 2, length, 2, 2, length), np.float32)
    pair = np.zeros((n_lv, 2, length, 2, length), np.float32)
    for d in range(2):
        big[:, d, :, 0, d, :] = sums[d][kept]
        big[:, d, :, 1, d, :] = sums[d][kept]
        pair[:, d, :, d, :] = masks[d]
    return (big.reshape(len(kept) * 2 * length, 4 * length),
            pair.reshape(n_lv, 2 * length, 2 * length), len(kept) - 1)


def _coarse_exponent(bp, half, length):
    parts = []
    for d in range(2):
        for start in range(0, length, 2 * half):
            lo = d * length + start
            mid = lo + half
            ref_row = mid if d == 0 else mid - 1
            ref = jnp.broadcast_to(bp[ref_row:ref_row + 1, :], (half, bp.shape[1]))
            first, second = bp[lo:mid, :], bp[mid:mid + half, :]
            parts += [ref - first, second - ref] if d == 0 else [first - ref, ref - second]
    return jnp.concatenate(parts, axis=0)


def _nt_dot(a, b):
    return lax.dot_general(a, b, (((1,), (1,)), ((), ())), preferred_element_type=F32)


def _lower_bound(tab):
    ex = jnp.exp(tab - jnp.max(tab, axis=0, keepdims=True))
    return ex[0:1] / jnp.sum(ex, axis=0, keepdims=True)


def _scan_kernel(*refs, heads, n_fine, emit_o, has_init):
    qf_ref, zf_ref, if_ref, qb_ref, zb_ref, ib_ref, tabf_ref, tabb_ref, sums_ref, masks_ref = refs[:10]
    pos = 10
    init_ref = None
    if has_init:
        init_ref = refs[pos]
        pos += 1
    n_out = 2 if emit_o else 1
    out_refs = refs[pos:pos + n_out]
    state_ref, d_scr = refs[pos + n_out:]

    c = pl.program_id(1)
    n_chunks = pl.num_programs(1)
    n_samples, length, width = qf_ref.shape
    levels = masks_ref.shape[0] - 1
    rows2 = 2 * length
    groups = n_samples * heads

    @pl.when(c == 0)
    def _():
        if has_init:
            state_ref[...] = init_ref[...].reshape(state_ref.shape)
        else:
            state_ref[...] = jnp.zeros_like(state_ref)

    def both(f_ref, b_ref):
        return jnp.concatenate(
            [jnp.concatenate([r[s] for s in range(n_samples)], axis=1) for r in (f_ref, b_ref)],
            axis=0)

    def per_dir(a, b):
        return jnp.concatenate([jnp.broadcast_to(a, (length, a.shape[1])),
                                jnp.broadcast_to(b, (length, b.shape[1]))], axis=0)

    def per_sample(a):
        return jnp.concatenate([a] * n_samples, axis=1)

    is_fwd = lax.broadcasted_iota(jnp.int32, (rows2, HEAD_DIM), 0) < length

    def split_dirs(a):
        return jnp.concatenate([jnp.where(is_fwd, a, 0.0), jnp.where(is_fwd, 0.0, a)], axis=1)

    lb = per_dir(per_sample(_lower_bound(tabf_ref[...])), per_sample(_lower_bound(tabb_ref[...])))
    z = both(zf_ref, zb_ref).astype(F32)
    q = both(qf_ref, qb_ref).astype(F32)
    v = both(if_ref, ib_ref)
    f = lb + (1.0 - lb) * _sigmoid(z)
    k = 1.0 - f
    g = jnp.log(f) * LOG2_E
    g_hi = g.astype(BF16)
    g_lo = (g - g_hi.astype(F32)).astype(BF16)
    g_parts = jnp.concatenate([g_hi, g_lo], axis=0)
    if emit_o:
        d_scr[...] = jnp.dot(sums_ref[...], g_parts, preferred_element_type=F32)
    else:
        d_scr[...] = jnp.dot(sums_ref[n_fine * rows2:(n_fine + 1) * rows2, :], g_parts,
                             preferred_element_type=F32)
    prefix_rows = slice(n_fine * rows2, (n_fine + 1) * rows2) if emit_o else slice(0, rows2)
    bp = d_scr[prefix_rows, :]
    btot_f = bp[length - 1:length]
    btot_b = bp[length:length + 1]

    if emit_o:
        qd = q * jnp.exp2(bp)
        o_carry = []
        for gi in range(groups):
            cols = slice(gi * HEAD_DIM, (gi + 1) * HEAD_DIM)
            o_carry.append(_nt_dot(split_dirs(qd[:, cols]).astype(BF16),
                                   state_ref[gi].astype(BF16)))
    kd = k * jnp.exp2(per_dir(btot_f, btot_b) - bp)
    dec_f = jnp.exp2(btot_f)
    dec_b = jnp.exp2(btot_b)
    for gi in range(groups):
        cols = slice(gi * HEAD_DIM, (gi + 1) * HEAD_DIM)
        dec = jnp.concatenate([dec_f[:, cols], dec_b[:, cols]], axis=1)
        vt = v[:, cols].astype(F32).T.astype(BF16)
        state_ref[gi] = state_ref[gi] * dec + jnp.dot(
            vt, split_dirs(kd[:, cols]).astype(BF16), preferred_element_type=F32)

    if emit_o:
        of_ref, ob_ref = out_refs
        scores = [None] * groups
        for lv in range(levels + 1):
            if lv == levels:
                xq, yk = q.astype(BF16), k.astype(BF16)
            else:
                if lv < n_fine:
                    dl = d_scr[lv * rows2:(lv + 1) * rows2, :]
                else:
                    dl = _coarse_exponent(bp, 1 << lv, length)
                e = jnp.exp2(dl)
                xq, yk = (q * e).astype(BF16), (k * e).astype(BF16)
            for gi in range(groups):
                cols = slice(gi * HEAD_DIM, (gi + 1) * HEAD_DIM)
                p = masks_ref[lv] * _nt_dot(xq[:, cols], yk[:, cols])
                scores[gi] = p if scores[gi] is None else scores[gi] + p
        for gi in range(groups):
            cols = slice(gi * HEAD_DIM, (gi + 1) * HEAD_DIM)
            out_cols = slice((gi % heads) * HEAD_DIM, (gi % heads + 1) * HEAD_DIM)
            o = o_carry[gi] + jnp.dot(scores[gi].astype(BF16), v[:, cols],
                                      preferred_element_type=F32)
            of_ref[gi // heads, :, out_cols] = o[:length].astype(of_ref.dtype)
            ob_ref[gi // heads, :, out_cols] = o[length:].astype(ob_ref.dtype)

    if not emit_o:
        @pl.when(c == n_chunks - 1)
        def _():
            out_refs[0][...] = state_ref[...].reshape(out_refs[0].shape)


def _scan(proj, col_q, col_zf, col_zb, col_i, tab_f, tab_b, init, emit_o, name):
    bsz, n, _ = proj.shape
    width = tab_f.shape[1]
    heads = width // HEAD_DIM
    length = SCAN_CHUNK
    n_chunks = n // length
    sums_np, masks_np, n_fine = _scan_constants(length)
    sums = jnp.asarray(sums_np, BF16)
    masks = jnp.asarray(masks_np, F32)
    d_rows = sums.shape[0] if emit_o else 2 * length
    ns = SCAN_SAMPLES

    def fwd(col):
        return pl.BlockSpec((ns, length, width), lambda b, c: (b, c, col))

    def bwd(col):
        return pl.BlockSpec((ns, length, width), lambda b, c: (b, n_chunks - 1 - c, col))

    def whole(a):
        nd = a.ndim
        return pl.BlockSpec(a.shape, lambda b, c: (0,) * nd)

    in_specs = [fwd(col_q), fwd(col_zf), fwd(col_i), bwd(col_q), bwd(col_zb), bwd(col_i),
                whole(tab_f), whole(tab_b), whole(sums), whole(masks)]
    args = [proj, proj, proj, proj, proj, proj, tab_f, tab_b, sums, masks]
    state_shape = (heads, HEAD_DIM, 2 * HEAD_DIM)
    if init is not None:
        in_specs.append(pl.BlockSpec((ns,) + state_shape, lambda b, c: (b, 0, 0, 0)))
        args.append(init)
    if emit_o:
        out_spec = [pl.BlockSpec((ns, length, width), lambda b, c: (b, c, 0)),
                    pl.BlockSpec((ns, length, width), lambda b, c: (b, n_chunks - 1 - c, 0))]
        out_shape = [jax.ShapeDtypeStruct((bsz, n, width), BF16)] * 2
    else:
        out_spec = pl.BlockSpec((ns,) + state_shape, lambda b, c: (b, 0, 0, 0))
        out_shape = jax.ShapeDtypeStruct((bsz,) + state_shape, F32)
    return pl.pallas_call(
        functools.partial(_scan_kernel, heads=heads, n_fine=n_fine, emit_o=emit_o,
                          has_init=init is not None),
        grid=(bsz // ns, n_chunks),
        in_specs=in_specs,
        out_specs=out_spec,
        out_shape=out_shape,
        scratch_shapes=[pltpu.VMEM((ns * heads,) + state_shape[1:], F32),
                        pltpu.VMEM((d_rows, ns * width), F32)],
        compiler_params=_params(("arbitrary", "arbitrary")),
        name=name,
    )(*args)


def _postmix_kernel(u_ref, v_ref, g_ref, of_ref, ob_ref, x_ref, g1_ref, sh2_ref, sc2_ref,
                    ln_ref, ws_ref, bs_ref, hn_ref, wout_ref, n2_ref, rw_ref, rb_ref,
                    x1_ref, h2_ref, lg_ref, y_scr):
    rows = x_ref.shape[1]
    heads = ln_ref.shape[0]
    sgu_w = heads * HEAD_DIM
    u = jax.nn.gelu(u_ref[0].astype(F32))
    v = jax.nn.gelu(v_ref[0].astype(F32))
    for ch in range(rows // SGU_CHUNK):
        rs = slice(ch * SGU_CHUNK, (ch + 1) * SGU_CHUNK)
        for h in range(heads):
            cols = slice(h * HEAD_DIM, (h + 1) * HEAD_DIM)
            vh = v[rs, cols]
            mu = jnp.mean(vh, axis=-1, keepdims=True)
            var = jnp.mean(jnp.square(vh - mu), axis=-1, keepdims=True)
            vn = (vh - mu) * lax.rsqrt(var + EPS) * ln_ref[h:h + 1, :]
            z = jnp.dot(ws_ref[h], vn.astype(BF16), preferred_element_type=F32) + bs_ref[h]
            y_scr[rs, cols] = (u[rs, cols] * z).astype(BF16)
    o = of_ref[0].astype(F32) + ob_ref[0].astype(F32)
    gate = g_ref[0].astype(F32)
    gate = gate * _sigmoid(gate)
    for h in range(o.shape[1] // HEAD_DIM):
        cols = slice(h * HEAD_DIM, (h + 1) * HEAD_DIM)
        oh = o[:, cols]
        on = oh * lax.rsqrt(jnp.mean(oh * oh, axis=-1, keepdims=True) + EPS)
        y_scr[:, sgu_w + h * HEAD_DIM:sgu_w + (h + 1) * HEAD_DIM] = (
            on * hn_ref[:, cols] * gate[:, cols]).astype(BF16)
    y = jnp.dot(y_scr[...], wout_ref[...], preferred_element_type=F32)
    x1 = x_ref[0] + g1_ref[0] * y
    x1_ref[0] = x1
    h2 = _rms(x1, n2_ref[...]) * (1.0 + sc2_ref[0]) + sh2_ref[0]
    h2_ref[...] = h2.reshape(h2_ref.shape)
    n_exp = rb_ref.shape[0]
    h_hi = h2.astype(BF16)
    h_lo = (h2 - h_hi.astype(F32)).astype(BF16)
    part = _nt_dot(rw_ref[...], h_hi)
    lg = part[:n_exp] + part[n_exp:] + _nt_dot(rw_ref[:n_exp, :], h_lo) + rb_ref[...]
    for s in range(rows // TOKEN_TILE):
        lg_ref[s] = lg[:, s * TOKEN_TILE:(s + 1) * TOKEN_TILE]


def _postmix(proj, o_fwd, o_bwd, x, mod3, sgu_ln, sgu_w_bf16, sgu_b3, hgrn_norm, w_out_bf16,
             norm2, router_w, router_b):
    bsz, n, d = x.shape
    width = o_fwd.shape[2]
    rows = POSTMIX_ROWS
    n_exp = router_b.shape[0]
    nj = n // rows
    sub = rows // TOKEN_TILE

    def colgroup(col):
        return pl.BlockSpec((1, rows, width), lambda b, j: (b, j, col))

    def modrow(idx):
        return pl.BlockSpec((1, 1, d), lambda b, j: (b * 6 + idx, 0, 0))

    def whole(a):
        nd = a.ndim
        return pl.BlockSpec(a.shape, lambda b, j: (0,) * nd)

    return pl.pallas_call(
        _postmix_kernel,
        grid=(bsz, nj),
        in_specs=[colgroup(0), colgroup(1), colgroup(6), colgroup(0), colgroup(0),
                  pl.BlockSpec((1, rows, d), lambda b, j: (b, j, 0)),
                  modrow(2), modrow(3), modrow(4),
                  whole(sgu_ln), whole(sgu_w_bf16), whole(sgu_b3), whole(hgrn_norm),
                  whole(w_out_bf16), whole(norm2), whole(router_w), whole(router_b)],
        out_specs=[pl.BlockSpec((1, rows, d), lambda b, j: (b, j, 0)),
                   pl.BlockSpec((rows, 1, d), lambda b, j: (b * nj + j, 0, 0)),
                   pl.BlockSpec((sub, n_exp, TOKEN_TILE), lambda b, j: (b * nj + j, 0, 0))],
        out_shape=[jax.ShapeDtypeStruct((bsz, n, d), F32),
                   jax.ShapeDtypeStruct((bsz * n, 1, d), F32),
                   jax.ShapeDtypeStruct((bsz * n // TOKEN_TILE, n_exp, TOKEN_TILE), F32)],
        scratch_shapes=[pltpu.VMEM((rows, d), BF16)],
        compiler_params=_params(("arbitrary", "arbitrary")),
        name="postmix",
    )(proj, proj, proj, o_fwd, o_bwd, x, mod3, mod3, mod3, sgu_ln, sgu_w_bf16, sgu_b3, hgrn_norm,
      w_out_bf16, norm2, router_w, router_b)


def _route_kernel(lt_ref, upper_ref, ones_ref, ltri_ref,
                  dest_ref, gate_ref, pads_ref, pos_scr, eidx_scr, *, block_rows):
    n_tiles, n_exp, lanes = lt_ref.shape
    iota_e = lax.broadcasted_iota(jnp.int32, (n_exp, lanes), 0).astype(F32)

    def tile_body(j, carry):
        l = lt_ref[j]
        picks, vals = [], []
        for _ in range(TOP_K):
            m = jnp.max(l, axis=0, keepdims=True)
            idx = jnp.min(jnp.where(l == m, iota_e, float(n_exp)), axis=0, keepdims=True)
            pick = iota_e == idx
            picks.append(pick)
            vals.append(m)
            l = jnp.where(pick, -jnp.inf, l)
            eidx_scr[j, len(picks) - 1:len(picks), :] = idx
        exps = [jnp.exp(v - vals[0]) for v in vals]
        tot = exps[0]
        for e in exps[1:]:
            tot = tot + e
        for kk in range(TOP_K):
            gate_ref[j, kk:kk + 1, :] = exps[kk] / tot
        chosen = picks[0]
        for p in picks[1:]:
            chosen = chosen | p
        cb = jnp.where(chosen, 1.0, 0.0).astype(BF16)
        before = carry + jnp.dot(cb, upper_ref[...], preferred_element_type=F32)
        for kk in range(TOP_K):
            pos_scr[j, kk:kk + 1, :] = jnp.sum(jnp.where(picks[kk], before, 0.0),
                                               axis=0, keepdims=True)
        return carry + jnp.dot(cb, ones_ref[...], preferred_element_type=F32)

    counts = lax.fori_loop(0, n_tiles, tile_body, jnp.zeros((n_exp, lanes), F32))
    n_blk = jnp.ceil(counts * (1.0 / block_rows))
    end_blk = jnp.dot(ltri_ref[...], n_blk.astype(BF16), preferred_element_type=F32)
    start_row = (end_blk - n_blk) * block_rows

    def dest_body(j, _):
        for kk in range(TOP_K):
            idx = eidx_scr[j, kk:kk + 1, :]
            base = jnp.sum(jnp.where(iota_e == idx, start_row, 0.0), axis=0, keepdims=True)
            dest_ref[j, kk:kk + 1, :] = (base + pos_scr[j, kk:kk + 1, :]).astype(jnp.int32)
        return 0

    lax.fori_loop(0, n_tiles, dest_body, 0)

    pads_ref[0] = (start_row + counts).astype(jnp.int32)
    pads_ref[1] = (n_blk * block_rows - counts).astype(jnp.int32)
    pads_ref[2] = end_blk.astype(jnp.int32)
    pads_ref[3] = start_row.astype(jnp.int32)
    pads_ref[4] = n_blk.astype(jnp.int32)


def _route(lt3):
    n_tiles, n_exp, lanes = lt3.shape
    upper = jnp.asarray(np.triu(np.ones((lanes, lanes), np.float32), 1), BF16)
    ones = jnp.ones((lanes, lanes), BF16)
    ltri = jnp.asarray(np.tril(np.ones((n_exp, n_exp), np.float32)), BF16)
    vm = pl.BlockSpec(memory_space=pltpu.VMEM)
    return pl.pallas_call(
        functools.partial(_route_kernel, block_rows=EXPERT_ROWS),
        in_specs=[vm, vm, vm, vm],
        out_specs=[vm, vm, vm],
        out_shape=[jax.ShapeDtypeStruct((n_tiles, TOP_K, lanes), jnp.int32),
                   jax.ShapeDtypeStruct((n_tiles, TOP_K, lanes), F32),
                   jax.ShapeDtypeStruct((5, n_exp, lanes), jnp.int32)],
        scratch_shapes=[pltpu.VMEM((n_tiles, TOP_K, lanes), F32),
                        pltpu.VMEM((n_tiles, TOP_K, lanes), F32)],
        compiler_params=pltpu.CompilerParams(vmem_limit_bytes=VMEM_LIMIT),
        name="route",
    )(lt3, upper, ones, ltri)


def _pad_sizes():
    return [1 << b for b in reversed(range(int(math.log2(EXPERT_ROWS))))]


def _dispatch_kernel(pads_ref, dest_ref, h2_ref, xs_hbm, zero_scr, sem, zsem):
    j = pl.program_id(0)
    n_steps = pl.num_programs(0)
    tile = dest_ref.shape[2]
    n_exp = pads_ref.shape[1]

    def pad_copies(e, wait):
        off = pads_ref[0, e]
        cnt = pads_ref[1, e]
        for sz in _pad_sizes():
            cp = pltpu.make_async_copy(zero_scr.at[pl.ds(0, sz)], xs_hbm.at[pl.ds(off, sz)], zsem)

            @pl.when((cnt & sz) != 0)
            def _():
                if wait:
                    cp.wait()
                else:
                    cp.start()
            off = off + (cnt & sz)

    half = zero_scr.shape[0]
    n_blocks = xs_hbm.shape[0] // (2 * half)
    n_used = pads_ref[2, n_exp - 1]

    def tail_copies(blk, wait):
        for part in range(2):
            cp = pltpu.make_async_copy(
                zero_scr, xs_hbm.at[pl.ds((blk * 2 + part) * half, half)], zsem)
            if wait:
                cp.wait()
            else:
                cp.start()

    @pl.when(j == 0)
    def _():
        zero_scr[...] = jnp.zeros_like(zero_scr)

        def start(e, _):
            pad_copies(e, False)
            return 0
        lax.fori_loop(0, n_exp, start, 0)

        def start_tail(blk, _):
            tail_copies(blk, False)
            return 0
        lax.fori_loop(n_used, n_blocks, start_tail, 0)

    for sub in range(dest_ref.shape[0]):
        def issue(i, _, sub=sub):
            for kk in range(TOP_K):
                pltpu.make_async_copy(h2_ref.at[sub * tile + i],
                                      xs_hbm.at[dest_ref[sub, kk, i]], sem).start(priority=kk % 2)
            return 0
        lax.fori_loop(0, tile, issue, 0)

    for kk in range(TOP_K):
        pltpu.make_async_copy(h2_ref, xs_hbm.at[pl.ds(0, h2_ref.shape[0])], sem).wait()

    @pl.when(j == n_steps - 1)
    def _():
        def finish(e, _):
            pad_copies(e, True)
            return 0
        lax.fori_loop(0, n_exp, finish, 0)

        def finish_tail(blk, _):
            tail_copies(blk, True)
            return 0
        lax.fori_loop(n_used, n_blocks, finish_tail, 0)


def _dispatch(pads, dest, h2_rows, n_rows):
    n_tok, _, d = h2_rows.shape
    n_tiles = dest.shape[0]
    sub = DISPATCH_SUBTILES
    grid_spec = pltpu.PrefetchScalarGridSpec(
        num_scalar_prefetch=1,
        grid=(n_tiles // sub,),
        in_specs=[pl.BlockSpec((sub, TOP_K, TOKEN_TILE), lambda j, p: (j, 0, 0),
                               memory_space=pltpu.SMEM),
                  pl.BlockSpec((sub * TOKEN_TILE, 1, d), lambda j, p: (j, 0, 0))],
        out_specs=pl.BlockSpec(memory_space=pl.ANY),
        scratch_shapes=[pltpu.VMEM((EXPERT_ROWS // 2, 1, d), F32),
                        pltpu.SemaphoreType.DMA(()),
                        pltpu.SemaphoreType.DMA(())],
    )
    return pl.pallas_call(
        _dispatch_kernel,
        grid_spec=grid_spec,
        out_shape=jax.ShapeDtypeStruct((n_rows, 1, d), F32),
        compiler_params=_params(("arbitrary",)),
        name="dispatch",
    )(pads, dest, h2_rows)


def _experts_kernel(start_ref, nblk_ref, xs_hbm, w1_ref, b1_ref, w2_ref, b2_ref, ys_hbm,
                    w1_scr, w2_scr, xb0, xb1, yb0, yb1, x_scr, act_scr, load_sem, store_sem):
    e = pl.program_id(0)
    n_blk = nblk_ref[e]
    row0 = start_ref[e]
    rows = xb0.shape[0]
    d, two_f = w1_scr.shape
    f = two_f // 2
    xbufs, ybufs = (xb0, xb1), (yb0, yb1)

    def load(i, s):
        return pltpu.make_async_copy(xs_hbm.at[pl.ds(row0 + i * rows, rows)], xbufs[s],
                                     load_sem.at[s])

    def store(i, s):
        return pltpu.make_async_copy(ybufs[s], ys_hbm.at[pl.ds(row0 + i * rows, rows)],
                                     store_sem.at[s])

    @pl.when(n_blk > 0)
    def _():
        load(0, 0).start()
        step = 64

        def cast1(r, _):
            rs = pl.ds(pl.multiple_of(r * step, step), step)
            w1_scr[rs, :] = w1_ref[0, rs, :].astype(BF16)
            return 0
        lax.fori_loop(0, d // step, cast1, 0)

        def cast2(r, _):
            rs = pl.ds(pl.multiple_of(r * step, step), step)
            w2_scr[rs, :] = w2_ref[0, rs, :].astype(BF16)
            return 0
        lax.fori_loop(0, f // step, cast2, 0)

    def block(i, s):
        load(i, s).wait()

        @pl.when(i + 1 < n_blk)
        def _():
            load(i + 1, 1 - s).start()

        x_scr[...] = xbufs[s][...].reshape(x_scr.shape)
        xb = x_scr[...].astype(BF16)
        width = 512
        for n in range(f // width):
            cg = slice(n * width, (n + 1) * width)
            cu = slice(f + n * width, f + (n + 1) * width)
            gate = jnp.dot(xb, w1_scr[:, cg], preferred_element_type=F32) + b1_ref[0, :, cg]
            up = jnp.dot(xb, w1_scr[:, cu], preferred_element_type=F32) + b1_ref[0, :, cu]
            gate = jnp.minimum(gate, SWIGLU_LIMIT)
            up = jnp.clip(up, -SWIGLU_LIMIT, SWIGLU_LIMIT)
            act_scr[:, cg] = (gate * _sigmoid(SWIGLU_ALPHA * gate) * (up + 1.0)).astype(BF16)
        out = jnp.dot(act_scr[...], w2_scr[...], preferred_element_type=F32) + b2_ref[0]

        @pl.when(i >= 2)
        def _():
            store(i - 2, s).wait()

        ybufs[s][...] = out.reshape(ybufs[s].shape)
        store(i, s).start()

    def pair(p, _):
        block(2 * p, 0)

        @pl.when(2 * p + 1 < n_blk)
        def _():
            block(2 * p + 1, 1)
        return 0
    lax.fori_loop(0, lax.shift_right_logical(n_blk + 1, 1), pair, 0)

    for back in (1, 2):
        for s in range(2):
            @pl.when((n_blk >= back) & ((n_blk - back) % 2 == s))
            def _(back=back, s=s):
                store(n_blk - back, s).wait()


def _experts(start_row, n_blk, xs, w1, b1, w2, b2):
    n_rows, _, d = xs.shape
    n_exp, _, two_f = w1.shape
    f = two_f // 2
    rows = EXPERT_ROWS
    grid_spec = pltpu.PrefetchScalarGridSpec(
        num_scalar_prefetch=2,
        grid=(n_exp,),
        in_specs=[pl.BlockSpec(memory_space=pl.ANY),
                  pl.BlockSpec((1, d, two_f), lambda e, st, nb: (e, 0, 0)),
                  pl.BlockSpec((1, 1, two_f), lambda e, st, nb: (e, 0, 0)),
                  pl.BlockSpec((1, f, d), lambda e, st, nb: (e, 0, 0)),
                  pl.BlockSpec((1, 1, d), lambda e, st, nb: (e, 0, 0))],
        out_specs=pl.BlockSpec(memory_space=pl.ANY),
        scratch_shapes=[pltpu.VMEM((d, two_f), BF16),
                        pltpu.VMEM((f, d), BF16)]
        + [pltpu.VMEM((rows, 1, d), F32) for _ in range(4)]
        + [pltpu.VMEM((rows, d), F32),
           pltpu.VMEM((rows, f), BF16),
           pltpu.SemaphoreType.DMA((2,)),
           pltpu.SemaphoreType.DMA((2,))],
    )
    return pl.pallas_call(
        _experts_kernel,
        grid_spec=grid_spec,
        out_shape=jax.ShapeDtypeStruct((n_rows, 1, d), F32),
        input_output_aliases={2: 0},
        compiler_params=_params(("arbitrary",)),
        name="experts",
    )(start_row, n_blk, xs, w1, b1.reshape(n_exp, 1, two_f), w2, b2.reshape(n_exp, 1, d))


def _combine_kernel(dest_ref, next_ref, ys_hbm, gate_ref, x1_ref, g2_ref, fn_ref, o_ref, *scratch):
    bufs = (scratch[:TOP_K], scratch[TOP_K:2 * TOP_K])
    flat_scr, sem = scratch[2 * TOP_K:]
    j = pl.program_id(0)
    n_steps = pl.num_programs(0)
    tile = dest_ref.shape[2]

    def gather(idx_ref, s):
        def issue(i, _):
            for kk in range(TOP_K):
                pltpu.make_async_copy(ys_hbm.at[idx_ref[0, kk, i]], bufs[s][kk].at[i],
                                      sem.at[s]).start(priority=kk % 2)
            return 0
        lax.fori_loop(0, tile, issue, 0)

    @pl.when(j == 0)
    def _():
        gather(dest_ref, 0)

    for s in range(2):
        @pl.when((j % 2 == s) & (j + 1 < n_steps))
        def _(s=s):
            gather(next_ref, 1 - s)

    for s in range(2):
        @pl.when(j % 2 == s)
        def _(s=s):
            for kk in range(TOP_K):
                pltpu.make_async_copy(ys_hbm.at[pl.ds(0, tile)], bufs[s][kk], sem.at[s]).wait()
            y = None
            for kk in range(TOP_K):
                flat_scr[...] = bufs[s][kk][...].reshape(flat_scr.shape)
                term = gate_ref[:, kk:kk + 1] * flat_scr[...]
                y = term if y is None else y + term
            x2 = x1_ref[...] + g2_ref[0] * y
            o_ref[...] = _rms(x2, fn_ref[...])


def _combine(dest, ys, gates_tok, x1, mod3, final_norm, seq):
    n_tok, d = x1.shape
    n_tiles = dest.shape[0]
    tiles_per_sample = seq // TOKEN_TILE
    return pl.pallas_call(
        _combine_kernel,
        grid=(n_tiles,),
        in_specs=[pl.BlockSpec((1, TOP_K, TOKEN_TILE), lambda j: (j, 0, 0),
                               memory_space=pltpu.SMEM),
                  pl.BlockSpec((1, TOP_K, TOKEN_TILE),
                               lambda j: (jnp.minimum(j + 1, n_tiles - 1), 0, 0),
                               memory_space=pltpu.SMEM),
                  pl.BlockSpec(memory_space=pl.ANY),
                  pl.BlockSpec((TOKEN_TILE, TOP_K), lambda j: (j, 0)),
                  pl.BlockSpec((TOKEN_TILE, d), lambda j: (j, 0)),
                  pl.BlockSpec((1, 1, d), lambda j: ((j // tiles_per_sample) * 6 + 5, 0, 0)),
                  pl.BlockSpec((1, d), lambda j: (0, 0))],
        out_specs=pl.BlockSpec((TOKEN_TILE, d), lambda j: (j, 0)),
        out_shape=jax.ShapeDtypeStruct((n_tok, d), F32),
        scratch_shapes=[pltpu.VMEM((TOKEN_TILE, 1, d), F32) for _ in range(2 * TOP_K)]
        + [pltpu.VMEM((TOKEN_TILE, d), F32), pltpu.SemaphoreType.DMA((2,))],
        compiler_params=_params(("arbitrary",)),
        name="combine",
    )(dest, dest, ys, gates_tok, x1, mod3, final_norm)


def kernel(x, c, ctx, c_ctx, w_mod, b_mod, norm1, w_in, sgu_ln, sgu_w, sgu_b, lb_fwd, lb_bwd,
           hgrn_norm, w_out, norm2, router_w, router_b, w1, b1, w2, b2, final_norm):
    depth = w_mod.shape[0]
    assert depth == 1, "single-layer block only"
    bsz, seq, d = x.shape
    assert bsz + 1 <= MOD_ROWS
    width = w_in.shape[2] // N_PROJ
    n_exp = router_w.shape[2]
    n_tok = bsz * seq

    cc = jnp.zeros((MOD_ROWS, d), F32).at[:bsz].set(c).at[bsz].set(c_ctx)
    mod = _modulation(cc, w_mod[0], b_mod[0])
    mod3 = mod.reshape(MOD_ROWS * 6, 1, d)

    w_in_b = w_in[0].astype(BF16)
    norm1_l = norm1[0].reshape(1, d)
    proj_ctx = _inproj(ctx, mod3, lambda b: bsz, norm1_l, w_in_b[:, 2 * width:6 * width],
                       ctx.shape[1], "inproj_ctx")
    states = _scan(proj_ctx, 0, 1, 2, 3, lb_fwd, lb_bwd, None, False, "scan_ctx")
    proj = _inproj(x, mod3, lambda b: b, norm1_l, w_in_b, INPROJ_ROWS, "inproj")
    o_fwd, o_bwd = _scan(proj, 2, 3, 4, 5, lb_fwd, lb_bwd, states, True, "scan")

    rw_t = router_w[0].T
    rw_hi = rw_t.astype(BF16)
    rw_parts = jnp.concatenate([rw_hi, (rw_t - rw_hi.astype(F32)).astype(BF16)], axis=0)
    x1, h2_rows, lt3 = _postmix(
        proj, o_fwd, o_bwd, x, mod3, sgu_ln[0], sgu_w[0].astype(BF16),
        sgu_b[0].reshape(sgu_b.shape[1], sgu_b.shape[2], 1), hgrn_norm[0].reshape(1, -1),
        w_out[0].astype(BF16), norm2[0].reshape(1, d), rw_parts, router_b[0].reshape(n_exp, 1))

    n_blocks = -(-(n_tok * TOP_K + n_exp * (EXPERT_ROWS - 1)) // EXPERT_ROWS)
    dest, gates, pads = _route(lt3)
    tables = pads[:, :, 0]

    xs = _dispatch(tables[:3], dest, h2_rows, n_blocks * EXPERT_ROWS)
    ys = _experts(tables[3], tables[4], xs, w1[0], b1[0], w2[0], b2[0])
    gates_tok = gates.transpose(0, 2, 1).reshape(n_tok, TOP_K)
    out = _combine(dest, ys, gates_tok, x1.reshape(n_tok, d), mod3, final_norm.reshape(1, d), seq)
    return out.reshape(bsz, seq, d)
```

```python
import functools
import math

import jax
import jax.numpy as jnp
import numpy as np
from jax import lax
from jax.experimental import pallas as pl
from jax.experimental.pallas import tpu as pltpu

F32 = jnp.float32
BF16 = jnp.bfloat16
EPS = 1e-6
LOG2_E = 1.4426950408889634
SUBLANES = 8

HEAD_DIM = 128
SGU_CHUNK = 128
N_PROJ = 7
TOP_K = 4
SWIGLU_LIMIT = 7.0
SWIGLU_ALPHA = 1.702

SCAN_CHUNK = 64
SCAN_SAMPLES = 2
INPROJ_ROWS = 512
POSTMIX_ROWS = 512
EXPERT_ROWS = 512
TOKEN_TILE = 128
DISPATCH_SUBTILES = 4
MOD_ROWS = 16
VMEM_LIMIT = 56 * 1024 * 1024


def _params(sem, vmem=None):
    return pltpu.CompilerParams(dimension_semantics=sem, vmem_limit_bytes=vmem or VMEM_LIMIT)


def _sigmoid(x):
    return 1.0 / (1.0 + jnp.exp(-x))


def _mod_kernel(c_ref, w_ref, b_ref, o_ref):
    c = c_ref[...]
    s = c * _sigmoid(c)
    o_ref[...] = jnp.dot(s, w_ref[...], precision=lax.Precision.HIGHEST,
                         preferred_element_type=F32) + b_ref[...]


def _modulation(cc, w_mod, b_mod):
    d, n6 = w_mod.shape
    tn = 1024
    return pl.pallas_call(
        _mod_kernel,
        grid=(n6 // tn,),
        in_specs=[pl.BlockSpec((MOD_ROWS, d), lambda j: (0, 0)),
                  pl.BlockSpec((d, tn), lambda j: (0, j)),
                  pl.BlockSpec((1, tn), lambda j: (0, j))],
        out_specs=pl.BlockSpec((MOD_ROWS, tn), lambda j: (0, j)),
        out_shape=jax.ShapeDtypeStruct((MOD_ROWS, n6), F32),
        compiler_params=_params(("arbitrary",)),
        name="mod",
    )(cc, w_mod, b_mod.reshape(1, n6))


def _rms(x, g):
    return x * lax.rsqrt(jnp.mean(x * x, axis=-1, keepdims=True) + EPS) * g


def _inproj_kernel(x_ref, sh_ref, sc_ref, g_ref, w_ref, o_ref, *, width):
    h = _rms(x_ref[0], g_ref[...])
    h = h * (1.0 + sc_ref[0]) + sh_ref[0]
    hb = h.astype(BF16)
    for j in range(w_ref.shape[1] // width):
        cols = slice(j * width, (j + 1) * width)
        o_ref[0, :, cols] = jnp.dot(hb, w_ref[:, cols],
                                    preferred_element_type=F32).astype(o_ref.dtype)


def _inproj(x, mod3, mod_row, norm_g, w_bf16, rows, name):
    bsz, n, d = x.shape
    wcols = w_bf16.shape[1]
    return pl.pallas_call(
        functools.partial(_inproj_kernel, width=512),
        grid=(bsz, n // rows),
        in_specs=[pl.BlockSpec((1, rows, d), lambda b, j: (b, j, 0)),
                  pl.BlockSpec((1, 1, d), lambda b, j: (mod_row(b) * 6 + 0, 0, 0)),
                  pl.BlockSpec((1, 1, d), lambda b, j: (mod_row(b) * 6 + 1, 0, 0)),
                  pl.BlockSpec((1, d), lambda b, j: (0, 0)),
                  pl.BlockSpec((d, wcols), lambda b, j: (0, 0))],
        out_specs=pl.BlockSpec((1, rows, wcols), lambda b, j: (b, j, 0)),
        out_shape=jax.ShapeDtypeStruct((bsz, n, wcols), BF16),
        compiler_params=_params(("arbitrary", "arbitrary")),
        name=name,
    )(x, mod3, mod3, norm_g, w_bf16)


def _scan_constants(length):
    levels = int(math.log2(length))
    assert 1 << levels == length
    t = np.arange(length)
    sums = np.zeros((2, levels + 1, length, length), np.float32)
    masks = np.zeros((2, levels + 1, length, length), np.float32)
    for lv in range(levels):
        half = 1 << lv
        blk = t // (2 * half)
        mid = blk * 2 * half + half
        later = (t % (2 * half)) >= half
        for r in range(length):
            if later[r]:
                sums[0, lv, r, mid[r] + 1:r + 1] = 1.0
            else:
                sums[0, lv, r, r + 1:mid[r] + 1] = 1.0
        masks[0, lv] = (blk[:, None] == blk[None, :]) & later[:, None] & (~later)[None, :]
    sums[0, levels] = np.tril(np.ones((length, length), np.float32))
    masks[0, levels] = np.eye(length, dtype=np.float32)
    sums[1] = sums[0][:, ::-1, ::-1]
    masks[1] = masks[0][:, ::-1, ::-1]
    n_lv = levels + 1
    kept = [lv for lv in range(levels) if (1 << lv) < SUBLANES] + [levels]
    big = np.zeros((len(kept), 2, length, 2, 2, length), np.float32)
    pair = np.zeros((n_lv, 2, length, 2, length), np.float32)
    for d in range(2):
        big[:, d, :, 0, d, :] = sums[d][kept]
        big[:, d, :, 1, d, :] = sums[d][kept]
        pair[:, d, :, d, :] = masks[d]
    return (big.reshape(len(kept) * 2 * length, 4 * length),
            pair.reshape(n_lv, 2 * length, 2 * length), len(kept) - 1)


def _coarse_exponent(bp, half, length):
    parts = []
    for d in range(2):
        for start in range(0, length, 2 * half):
            lo = d * length + start
            mid = lo + half
            ref_row = mid if d == 0 else mid - 1
            ref = jnp.broadcast_to(bp[ref_row:ref_row + 1, :], (half, bp.shape[1]))
            first, second = bp[lo:mid, :], bp[mid:mid + half, :]
            parts += [ref - first, second - ref] if d == 0 else [first - ref, ref - second]
    return jnp.concatenate(parts, axis=0)


def _nt_dot(a, b):
    return lax.dot_general(a, b, (((1,), (1,)), ((), ())), preferred_element_type=F32)


def _lower_bound(tab):
    ex = jnp.exp(tab - jnp.max(tab, axis=0, keepdims=True))
    return ex[0:1] / jnp.sum(ex, axis=0, keepdims=True)


def _scan_kernel(*refs, heads, n_fine, emit_o, has_init):
    qf_ref, zf_ref, if_ref, qb_ref, zb_ref, ib_ref, tabf_ref, tabb_ref, sums_ref, masks_ref = refs[:10]
    pos = 10
    init_ref = None
    if has_init:
        init_ref = refs[pos]
        pos += 1
    n_out = 2 if emit_o else 1
    out_refs = refs[pos:pos + n_out]
    state_ref, d_scr = refs[pos + n_out:]

    c = pl.program_id(1)
    n_chunks = pl.num_programs(1)
    n_samples, length, width = qf_ref.shape
    levels = masks_ref.shape[0] - 1
    rows2 = 2 * length
    groups = n_samples * heads

    @pl.when(c == 0)
    def _():
        if has_init:
            state_ref[...] = init_ref[...].reshape(state_ref.shape)
        else:
            state_ref[...] = jnp.zeros_like(state_ref)

    def both(f_ref, b_ref):
        return jnp.concatenate(
            [jnp.concatenate([r[s] for s in range(n_samples)], axis=1) for r in (f_ref, b_ref)],
            axis=0)

    def per_dir(a, b):
        return jnp.concatenate([jnp.broadcast_to(a, (length, a.shape[1])),
                                jnp.broadcast_to(b, (length, b.shape[1]))], axis=0)

    def per_sample(a):
        return jnp.concatenate([a] * n_samples, axis=1)

    is_fwd = lax.broadcasted_iota(jnp.int32, (rows2, HEAD_DIM), 0) < length

    def split_dirs(a):
        return jnp.concatenate([jnp.where(is_fwd, a, 0.0), jnp.where(is_fwd, 0.0, a)], axis=1)

    lb = per_dir(per_sample(_lower_bound(tabf_ref[...])), per_sample(_lower_bound(tabb_ref[...])))
    z = both(zf_ref, zb_ref).astype(F32)
    q = both(qf_ref, qb_ref).astype(F32)
    v = both(if_ref, ib_ref)
    f = lb + (1.0 - lb) * _sigmoid(z)
    k = 1.0 - f
    g = jnp.log(f) * LOG2_E
    g_hi = g.astype(BF16)
    g_lo = (g - g_hi.astype(F32)).astype(BF16)
    g_parts = jnp.concatenate([g_hi, g_lo], axis=0)
    if emit_o:
        d_scr[...] = jnp.dot(sums_ref[...], g_parts, preferred_element_type=F32)
    else:
        d_scr[...] = jnp.dot(sums_ref[n_fine * rows2:(n_fine + 1) * rows2, :], g_parts,
                             preferred_element_type=F32)
    prefix_rows = slice(n_fine * rows2, (n_fine + 1) * rows2) if emit_o else slice(0, rows2)
    bp = d_scr[prefix_rows, :]
    btot_f = bp[length - 1:length]
    btot_b = bp[length:length + 1]

    if emit_o:
        qd = q * jnp.exp2(bp)
        o_carry = []
        for gi in range(groups):
            cols = slice(gi * HEAD_DIM, (gi + 1) * HEAD_DIM)
            o_carry.append(_nt_dot(split_dirs(qd[:, cols]).astype(BF16),
                                   state_ref[gi].astype(BF16)))
    kd = k * jnp.exp2(per_dir(btot_f, btot_b) - bp)
    dec_f = jnp.exp2(btot_f)
    dec_b = jnp.exp2(btot_b)
    for gi in range(groups):
        cols = slice(gi * HEAD_DIM, (gi + 1) * HEAD_DIM)
        dec = jnp.concatenate([dec_f[:, cols], dec_b[:, cols]], axis=1)
        vt = v[:, cols].astype(F32).T.astype(BF16)
        state_ref[gi] = state_ref[gi] * dec + jnp.dot(
            vt, split_dirs(kd[:, cols]).astype(BF16), preferred_element_type=F32)

    if emit_o:
        of_ref, ob_ref = out_refs
        scores = [None] * groups
        for lv in range(levels + 1):
            if lv == levels:
                xq, yk = q.astype(BF16), k.astype(BF16)
            else:
                if lv < n_fine:
                    dl = d_scr[lv * rows2:(lv + 1) * rows2, :]
                else:
                    dl = _coarse_exponent(bp, 1 << lv, length)
                e = jnp.exp2(dl)
                xq, yk = (q * e).astype(BF16), (k * e).astype(BF16)
            for gi in range(groups):
                cols = slice(gi * HEAD_DIM, (gi + 1) * HEAD_DIM)
                p = masks_ref[lv] * _nt_dot(xq[:, cols], yk[:, cols])
                scores[gi] = p if scores[gi] is None else scores[gi] + p
        for gi in range(groups):
            cols = slice(gi * HEAD_DIM, (gi + 1) * HEAD_DIM)
            out_cols = slice((gi % heads) * HEAD_DIM, (gi % heads + 1) * HEAD_DIM)
            o = o_carry[gi] + jnp.dot(scores[gi].astype(BF16), v[:, cols],
                                      preferred_element_type=F32)
            of_ref[gi // heads, :, out_cols] = o[:length].astype(of_ref.dtype)
            ob_ref[gi // heads, :, out_cols] = o[length:].astype(ob_ref.dtype)

    if not emit_o:
        @pl.when(c == n_chunks - 1)
        def _():
            out_refs[0][...] = state_ref[...].reshape(out_refs[0].shape)


def _scan(proj, col_q, col_zf, col_zb, col_i, tab_f, tab_b, init, emit_o, name):
    bsz, n, _ = proj.shape
    width = tab_f.shape[1]
    heads = width // HEAD_DIM
    length = SCAN_CHUNK
    n_chunks = n // length
    sums_np, masks_np, n_fine = _scan_constants(length)
    sums = jnp.asarray(sums_np, BF16)
    masks = jnp.asarray(masks_np, F32)
    d_rows = sums.shape[0] if emit_o else 2 * length
    ns = SCAN_SAMPLES

    def fwd(col):
        return pl.BlockSpec((ns, length, width), lambda b, c: (b, c, col))

    def bwd(col):
        return pl.BlockSpec((ns, length, width), lambda b, c: (b, n_chunks - 1 - c, col))

    def whole(a):
        nd = a.ndim
        return pl.BlockSpec(a.shape, lambda b, c: (0,) * nd)

    in_specs = [fwd(col_q), fwd(col_zf), fwd(col_i), bwd(col_q), bwd(col_zb), bwd(col_i),
                whole(tab_f), whole(tab_b), whole(sums), whole(masks)]
    args = [proj, proj, proj, proj, proj, proj, tab_f, tab_b, sums, masks]
    state_shape = (heads, HEAD_DIM, 2 * HEAD_DIM)
    if init is not None:
        in_specs.append(pl.BlockSpec((ns,) + state_shape, lambda b, c: (b, 0, 0, 0)))
        args.append(init)
    if emit_o:
        out_spec = [pl.BlockSpec((ns, length, width), lambda b, c: (b, c, 0)),
                    pl.BlockSpec((ns, length, width), lambda b, c: (b, n_chunks - 1 - c, 0))]
        out_shape = [jax.ShapeDtypeStruct((bsz, n, width), BF16)] * 2
    else:
        out_spec = pl.BlockSpec((ns,) + state_shape, lambda b, c: (b, 0, 0, 0))
        out_shape = jax.ShapeDtypeStruct((bsz,) + state_shape, F32)
    return pl.pallas_call(
        functools.partial(_scan_kernel, heads=heads, n_fine=n_fine, emit_o=emit_o,
                          has_init=init is not None),
        grid=(bsz // ns, n_chunks),
        in_specs=in_specs,
        out_specs=out_spec,
        out_shape=out_shape,
        scratch_shapes=[pltpu.VMEM((ns * heads,) + state_shape[1:], F32),
                        pltpu.VMEM((d_rows, ns * width), F32)],
        compiler_params=_params(("arbitrary", "arbitrary")),
        name=name,
    )(*args)


def _postmix_kernel(u_ref, v_ref, g_ref, of_ref, ob_ref, x_ref, g1_ref, sh2_ref, sc2_ref,
                    ln_ref, ws_ref, bs_ref, hn_ref, wout_ref, n2_ref, rw_ref, rb_ref,
                    x1_ref, h2_ref, lg_ref, y_scr):
    rows = x_ref.shape[1]
    heads = ln_ref.shape[0]
    sgu_w = heads * HEAD_DIM
    u = jax.nn.gelu(u_ref[0].astype(F32))
    v = jax.nn.gelu(v_ref[0].astype(F32))
    for ch in range(rows // SGU_CHUNK):
        rs = slice(ch * SGU_CHUNK, (ch + 1) * SGU_CHUNK)
        for h in range(heads):
            cols = slice(h * HEAD_DIM, (h + 1) * HEAD_DIM)
            vh = v[rs, cols]
            mu = jnp.mean(vh, axis=-1, keepdims=True)
            var = jnp.mean(jnp.square(vh - mu), axis=-1, keepdims=True)
            vn = (vh - mu) * lax.rsqrt(var + EPS) * ln_ref[h:h + 1, :]
            z = jnp.dot(ws_ref[h], vn.astype(BF16), preferred_element_type=F32) + bs_ref[h]
            y_scr[rs, cols] = (u[rs, cols] * z).astype(BF16)
    o = of_ref[0].astype(F32) + ob_ref[0].astype(F32)
    gate = g_ref[0].astype(F32)
    gate = gate * _sigmoid(gate)
    for h in range(o.shape[1] // HEAD_DIM):
        cols = slice(h * HEAD_DIM, (h + 1) * HEAD_DIM)
        oh = o[:, cols]
        on = oh * lax.rsqrt(jnp.mean(oh * oh, axis=-1, keepdims=True) + EPS)
        y_scr[:, sgu_w + h * HEAD_DIM:sgu_w + (h + 1) * HEAD_DIM] = (
            on * hn_ref[:, cols] * gate[:, cols]).astype(BF16)
    y = jnp.dot(y_scr[...], wout_ref[...], preferred_element_type=F32)
    x1 = x_ref[0] + g1_ref[0] * y
    x1_ref[0] = x1
    h2 = _rms(x1, n2_ref[...]) * (1.0 + sc2_ref[0]) + sh2_ref[0]
    h2_ref[...] = h2.reshape(h2_ref.shape)
    n_exp = rb_ref.shape[0]
    h_hi = h2.astype(BF16)
    h_lo = (h2 - h_hi.astype(F32)).astype(BF16)
    part = _nt_dot(rw_ref[...], h_hi)
    lg = part[:n_exp] + part[n_exp:] + _nt_dot(rw_ref[:n_exp, :], h_lo) + rb_ref[...]
    for s in range(rows // TOKEN_TILE):
        lg_ref[s] = lg[:, s * TOKEN_TILE:(s + 1) * TOKEN_TILE]


def _postmix(proj, o_fwd, o_bwd, x, mod3, sgu_ln, sgu_w_bf16, sgu_b3, hgrn_norm, w_out_bf16,
             norm2, router_w, router_b):
    bsz, n, d = x.shape
    width = o_fwd.shape[2]
    rows = POSTMIX_ROWS
    n_exp = router_b.shape[0]
    nj = n // rows
    sub = rows // TOKEN_TILE

    def colgroup(col):
        return pl.BlockSpec((1, rows, width), lambda b, j: (b, j, col))

    def modrow(idx):
        return pl.BlockSpec((1, 1, d), lambda b, j: (b * 6 + idx, 0, 0))

    def whole(a):
        nd = a.ndim
        return pl.BlockSpec(a.shape, lambda b, j: (0,) * nd)

    return pl.pallas_call(
        _postmix_kernel,
        grid=(bsz, nj),
        in_specs=[colgroup(0), colgroup(1), colgroup(6), colgroup(0), colgroup(0),
                  pl.BlockSpec((1, rows, d), lambda b, j: (b, j, 0)),
                  modrow(2), modrow(3), modrow(4),
                  whole(sgu_ln), whole(sgu_w_bf16), whole(sgu_b3), whole(hgrn_norm),
                  whole(w_out_bf16), whole(norm2), whole(router_w), whole(router_b)],
        out_specs=[pl.BlockSpec((1, rows, d), lambda b, j: (b, j, 0)),
                   pl.BlockSpec((rows, 1, d), lambda b, j: (b * nj + j, 0, 0)),
                   pl.BlockSpec((sub, n_exp, TOKEN_TILE), lambda b, j: (b * nj + j, 0, 0))],
        out_shape=[jax.ShapeDtypeStruct((bsz, n, d), F32),
                   jax.ShapeDtypeStruct((bsz * n, 1, d), F32),
                   jax.ShapeDtypeStruct((bsz * n // TOKEN_TILE, n_exp, TOKEN_TILE), F32)],
        scratch_shapes=[pltpu.VMEM((rows, d), BF16)],
        compiler_params=_params(("arbitrary", "arbitrary")),
        name="postmix",
    )(proj, proj, proj, o_fwd, o_bwd, x, mod3, mod3, mod3, sgu_ln, sgu_w_bf16, sgu_b3, hgrn_norm,
      w_out_bf16, norm2, router_w, router_b)


def _route_kernel(lt_ref, upper_ref, ones_ref, ltri_ref,
                  dest_ref, gate_ref, pads_ref, pos_scr, eidx_scr, *, block_rows):
    n_tiles, n_exp, lanes = lt_ref.shape
    iota_e = lax.broadcasted_iota(jnp.int32, (n_exp, lanes), 0).astype(F32)

    def tile_body(j, carry):
        l = lt_ref[j]
        picks, vals = [], []
        for _ in range(TOP_K):
            m = jnp.max(l, axis=0, keepdims=True)
            idx = jnp.min(jnp.where(l == m, iota_e, float(n_exp)), axis=0, keepdims=True)
            pick = iota_e == idx
            picks.append(pick)
            vals.append(m)
            l = jnp.where(pick, -jnp.inf, l)
            eidx_scr[j, len(picks) - 1:len(picks), :] = idx
        exps = [jnp.exp(v - vals[0]) for v in vals]
        tot = exps[0]
        for e in exps[1:]:
            tot = tot + e
        for kk in range(TOP_K):
            gate_ref[j, kk:kk + 1, :] = exps[kk] / tot
        chosen = picks[0]
        for p in picks[1:]:
            chosen = chosen | p
        cb = jnp.where(chosen, 1.0, 0.0).astype(BF16)
        before = carry + jnp.dot(cb, upper_ref[...], preferred_element_type=F32)
        for kk in range(TOP_K):
            pos_scr[j, kk:kk + 1, :] = jnp.sum(jnp.where(picks[kk], before, 0.0),
                                               axis=0, keepdims=True)
        return carry + jnp.dot(cb, ones_ref[...], preferred_element_type=F32)

    counts = lax.fori_loop(0, n_tiles, tile_body, jnp.zeros((n_exp, lanes), F32))
    n_blk = jnp.ceil(counts * (1.0 / block_rows))
    end_blk = jnp.dot(ltri_ref[...], n_blk.astype(BF16), preferred_element_type=F32)
    start_row = (end_blk - n_blk) * block_rows

    def dest_body(j, _):
        for kk in range(TOP_K):
            idx = eidx_scr[j, kk:kk + 1, :]
            base = jnp.sum(jnp.where(iota_e == idx, start_row, 0.0), axis=0, keepdims=True)
            dest_ref[j, kk:kk + 1, :] = (base + pos_scr[j, kk:kk + 1, :]).astype(jnp.int32)
        return 0

    lax.fori_loop(0, n_tiles, dest_body, 0)

    pads_ref[0] = (start_row + counts).astype(jnp.int32)
    pads_ref[1] = (n_blk * block_rows - counts).astype(jnp.int32)
    pads_ref[2] = end_blk.astype(jnp.int32)
    pads_ref[3] = start_row.astype(jnp.int32)
    pads_ref[4] = n_blk.astype(jnp.int32)


def _route(lt3):
    n_tiles, n_exp, lanes = lt3.shape
    upper = jnp.asarray(np.triu(np.ones((lanes, lanes), np.float32), 1), BF16)
    ones = jnp.ones((lanes, lanes), BF16)
    ltri = jnp.asarray(np.tril(np.ones((n_exp, n_exp), np.float32)), BF16)
    vm = pl.BlockSpec(memory_space=pltpu.VMEM)
    return pl.pallas_call(
        functools.partial(_route_kernel, block_rows=EXPERT_ROWS),
        in_specs=[vm, vm, vm, vm],
        out_specs=[vm, vm, vm],
        out_shape=[jax.ShapeDtypeStruct((n_tiles, TOP_K, lanes), jnp.int32),
                   jax.ShapeDtypeStruct((n_tiles, TOP_K, lanes), F32),
                   jax.ShapeDtypeStruct((5, n_exp, lanes), jnp.int32)],
        scratch_shapes=[pltpu.VMEM((n_tiles, TOP_K, lanes), F32),
                        pltpu.VMEM((n_tiles, TOP_K, lanes), F32)],
        compiler_params=pltpu.CompilerParams(vmem_limit_bytes=VMEM_LIMIT),
        name="route",
    )(lt3, upper, ones, ltri)


def _pad_sizes():
    return [1 << b for b in reversed(range(int(math.log2(EXPERT_ROWS))))]


def _dispatch_kernel(pads_ref, dest_ref, h2_ref, xs_hbm, zero_scr, sem, zsem):
    j = pl.program_id(0)
    n_steps = pl.num_programs(0)
    tile = dest_ref.shape[2]
    n_exp = pads_ref.shape[1]

    def pad_copies(e, wait):
        off = pads_ref[0, e]
        cnt = pads_ref[1, e]
        for sz in _pad_sizes():
            cp = pltpu.make_async_copy(zero_scr.at[pl.ds(0, sz)], xs_hbm.at[pl.ds(off, sz)], zsem)

            @pl.when((cnt & sz) != 0)
            def _():
                if wait:
                    cp.wait()
                else:
                    cp.start()
            off = off + (cnt & sz)

    half = zero_scr.shape[0]
    n_blocks = xs_hbm.shape[0] // (2 * half)
    n_used = pads_ref[2, n_exp - 1]

    def tail_copies(blk, wait):
        for part in range(2):
            cp = pltpu.make_async_copy(
                zero_scr, xs_hbm.at[pl.ds((blk * 2 + part) * half, half)], zsem)
            if wait:
                cp.wait()
            else:
                cp.start()

    @pl.when(j == 0)
    def _():
        zero_scr[...] = jnp.zeros_like(zero_scr)

        def start(e, _):
            pad_copies(e, False)
            return 0
        lax.fori_loop(0, n_exp, start, 0)

        def start_tail(blk, _):
            tail_copies(blk, False)
            return 0
        lax.fori_loop(n_used, n_blocks, start_tail, 0)

    for sub in range(dest_ref.shape[0]):
        def issue(i, _, sub=sub):
            for kk in range(TOP_K):
                pltpu.make_async_copy(h2_ref.at[sub * tile + i],
                                      xs_hbm.at[dest_ref[sub, kk, i]], sem).start(priority=kk % 2)
            return 0
        lax.fori_loop(0, tile, issue, 0)

    for kk in range(TOP_K):
        pltpu.make_async_copy(h2_ref, xs_hbm.at[pl.ds(0, h2_ref.shape[0])], sem).wait()

    @pl.when(j == n_steps - 1)
    def _():
        def finish(e, _):
            pad_copies(e, True)
            return 0
        lax.fori_loop(0, n_exp, finish, 0)

        def finish_tail(blk, _):
            tail_copies(blk, True)
            return 0
        lax.fori_loop(n_used, n_blocks, finish_tail, 0)


def _dispatch(pads, dest, h2_rows, n_rows):
    n_tok, _, d = h2_rows.shape
    n_tiles = dest.shape[0]
    sub = DISPATCH_SUBTILES
    grid_spec = pltpu.PrefetchScalarGridSpec(
        num_scalar_prefetch=1,
        grid=(n_tiles // sub,),
        in_specs=[pl.BlockSpec((sub, TOP_K, TOKEN_TILE), lambda j, p: (j, 0, 0),
                               memory_space=pltpu.SMEM),
                  pl.BlockSpec((sub * TOKEN_TILE, 1, d), lambda j, p: (j, 0, 0))],
        out_specs=pl.BlockSpec(memory_space=pl.ANY),
        scratch_shapes=[pltpu.VMEM((EXPERT_ROWS // 2, 1, d), F32),
                        pltpu.SemaphoreType.DMA(()),
                        pltpu.SemaphoreType.DMA(())],
    )
    return pl.pallas_call(
        _dispatch_kernel,
        grid_spec=grid_spec,
        out_shape=jax.ShapeDtypeStruct((n_rows, 1, d), F32),
        compiler_params=_params(("arbitrary",)),
        name="dispatch",
    )(pads, dest, h2_rows)


def _experts_kernel(start_ref, nblk_ref, xs_hbm, w1_ref, b1_ref, w2_ref, b2_ref, ys_hbm,
                    w1_scr, w2_scr, xb0, xb1, yb0, yb1, x_scr, act_scr, load_sem, store_sem):
    e = pl.program_id(0)
    n_blk = nblk_ref[e]
    row0 = start_ref[e]
    rows = xb0.shape[0]
    d, two_f = w1_scr.shape
    f = two_f // 2
    xbufs, ybufs = (xb0, xb1), (yb0, yb1)

    def load(i, s):
        return pltpu.make_async_copy(xs_hbm.at[pl.ds(row0 + i * rows, rows)], xbufs[s],
                                     load_sem.at[s])

    def store(i, s):
        return pltpu.make_async_copy(ybufs[s], ys_hbm.at[pl.ds(row0 + i * rows, rows)],
                                     store_sem.at[s])

    @pl.when(n_blk > 0)
    def _():
        load(0, 0).start()
        step = 64

        def cast1(r, _):
            rs = pl.ds(pl.multiple_of(r * step, step), step)
            w1_scr[rs, :] = w1_ref[0, rs, :].astype(BF16)
            return 0
        lax.fori_loop(0, d // step, cast1, 0)

        def cast2(r, _):
            rs = pl.ds(pl.multiple_of(r * step, step), step)
            w2_scr[rs, :] = w2_ref[0, rs, :].astype(BF16)
            return 0
        lax.fori_loop(0, f // step, cast2, 0)

    def block(i, s):
        load(i, s).wait()

        @pl.when(i + 1 < n_blk)
        def _():
            load(i + 1, 1 - s).start()

        x_scr[...] = xbufs[s][...].reshape(x_scr.shape)
        xb = x_scr[...].astype(BF16)
        width = 512
        for n in range(f // width):
            cg = slice(n * width, (n + 1) * width)
            cu = slice(f + n * width, f + (n + 1) * width)
            gate = jnp.dot(xb, w1_scr[:, cg], preferred_element_type=F32) + b1_ref[0, :, cg]
            up = jnp.dot(xb, w1_scr[:, cu], preferred_element_type=F32) + b1_ref[0, :, cu]
            gate = jnp.minimum(gate, SWIGLU_LIMIT)
            up = jnp.clip(up, -SWIGLU_LIMIT, SWIGLU_LIMIT)
            act_scr[:, cg] = (gate * _sigmoid(SWIGLU_ALPHA * gate) * (up + 1.0)).astype(BF16)
        out = jnp.dot(act_scr[...], w2_scr[...], preferred_element_type=F32) + b2_ref[0]

        @pl.when(i >= 2)
        def _():
            store(i - 2, s).wait()

        ybufs[s][...] = out.reshape(ybufs[s].shape)
        store(i, s).start()

    def pair(p, _):
        block(2 * p, 0)

        @pl.when(2 * p + 1 < n_blk)
        def _():
            block(2 * p + 1, 1)
        return 0
    lax.fori_loop(0, lax.shift_right_logical(n_blk + 1, 1), pair, 0)

    for back in (1, 2):
        for s in range(2):
            @pl.when((n_blk >= back) & ((n_blk - back) % 2 == s))
            def _(back=back, s=s):
                store(n_blk - back, s).wait()


def _experts(start_row, n_blk, xs, w1, b1, w2, b2):
    n_rows, _, d = xs.shape
    n_exp, _, two_f = w1.shape
    f = two_f // 2
    rows = EXPERT_ROWS
    grid_spec = pltpu.PrefetchScalarGridSpec(
        num_scalar_prefetch=2,
        grid=(n_exp,),
        in_specs=[pl.BlockSpec(memory_space=pl.ANY),
                  pl.BlockSpec((1, d, two_f), lambda e, st, nb: (e, 0, 0)),
                  pl.BlockSpec((1, 1, two_f), lambda e, st, nb: (e, 0, 0)),
                  pl.BlockSpec((1, f, d), lambda e, st, nb: (e, 0, 0)),
                  pl.BlockSpec((1, 1, d), lambda e, st, nb: (e, 0, 0))],
        out_specs=pl.BlockSpec(memory_space=pl.ANY),
        scratch_shapes=[pltpu.VMEM((d, two_f), BF16),
                        pltpu.VMEM((f, d), BF16)]
        + [pltpu.VMEM((rows, 1, d), F32) for _ in range(4)]
        + [pltpu.VMEM((rows, d), F32),
           pltpu.VMEM((rows, f), BF16),
           pltpu.SemaphoreType.DMA((2,)),
           pltpu.SemaphoreType.DMA((2,))],
    )
    return pl.pallas_call(
        _experts_kernel,
        grid_spec=grid_spec,
        out_shape=jax.ShapeDtypeStruct((n_rows, 1, d), F32),
        input_output_aliases={2: 0},
        compiler_params=_params(("arbitrary",)),
        name="experts",
    )(start_row, n_blk, xs, w1, b1.reshape(n_exp, 1, two_f), w2, b2.reshape(n_exp, 1, d))


def _combine_kernel(dest_ref, next_ref, ys_hbm, gate_ref, x1_ref, g2_ref, fn_ref, o_ref, *scratch):
    bufs = (scratch[:TOP_K], scratch[TOP_K:2 * TOP_K])
    flat_scr, sem = scratch[2 * TOP_K:]
    j = pl.program_id(0)
    n_steps = pl.num_programs(0)
    tile = dest_ref.shape[2]

    def gather(idx_ref, s):
        def issue(i, _):
            for kk in range(TOP_K):
                pltpu.make_async_copy(ys_hbm.at[idx_ref[0, kk, i]], bufs[s][kk].at[i],
                                      sem.at[s]).start(priority=kk % 2)
            return 0
        lax.fori_loop(0, tile, issue, 0)

    @pl.when(j == 0)
    def _():
        gather(dest_ref, 0)

    for s in range(2):
        @pl.when((j % 2 == s) & (j + 1 < n_steps))
        def _(s=s):
            gather(next_ref, 1 - s)

    for s in range(2):
        @pl.when(j % 2 == s)
        def _(s=s):
            for kk in range(TOP_K):
                pltpu.make_async_copy(ys_hbm.at[pl.ds(0, tile)], bufs[s][kk], sem.at[s]).wait()
            y = None
            for kk in range(TOP_K):
                flat_scr[...] = bufs[s][kk][...].reshape(flat_scr.shape)
                term = gate_ref[:, kk:kk + 1] * flat_scr[...]
                y = term if y is None else y + term
            x2 = x1_ref[...] + g2_ref[0] * y
            o_ref[...] = _rms(x2, fn_ref[...])


def _combine(dest, ys, gates_tok, x1, mod3, final_norm, seq):
    n_tok, d = x1.shape
    n_tiles = dest.shape[0]
    tiles_per_sample = seq // TOKEN_TILE
    return pl.pallas_call(
        _combine_kernel,
        grid=(n_tiles,),
        in_specs=[pl.BlockSpec((1, TOP_K, TOKEN_TILE), lambda j: (j, 0, 0),
                               memory_space=pltpu.SMEM),
                  pl.BlockSpec((1, TOP_K, TOKEN_TILE),
                               lambda j: (jnp.minimum(j + 1, n_tiles - 1), 0, 0),
                               memory_space=pltpu.SMEM),
                  pl.BlockSpec(memory_space=pl.ANY),
                  pl.BlockSpec((TOKEN_TILE, TOP_K), lambda j: (j, 0)),
                  pl.BlockSpec((TOKEN_TILE, d), lambda j: (j, 0)),
                  pl.BlockSpec((1, 1, d), lambda j: ((j // tiles_per_sample) * 6 + 5, 0, 0)),
                  pl.BlockSpec((1, d), lambda j: (0, 0))],
        out_specs=pl.BlockSpec((TOKEN_TILE, d), lambda j: (j, 0)),
        out_shape=jax.ShapeDtypeStruct((n_tok, d), F32),
        scratch_shapes=[pltpu.VMEM((TOKEN_TILE, 1, d), F32) for _ in range(2 * TOP_K)]
        + [pltpu.VMEM((TOKEN_TILE, d), F32), pltpu.SemaphoreType.DMA((2,))],
        compiler_params=_params(("arbitrary",)),
        name="combine",
    )(dest, dest, ys, gates_tok, x1, mod3, final_norm)


def kernel(x, c, ctx, c_ctx, w_mod, b_mod, norm1, w_in, sgu_ln, sgu_w, sgu_b, lb_fwd, lb_bwd,
           hgrn_norm, w_out, norm2, router_w, router_b, w1, b1, w2, b2, final_norm):
    depth = w_mod.shape[0]
    assert depth == 1, "single-layer block only"
    bsz, seq, d = x.shape
    assert bsz + 1 <= MOD_ROWS
    width = w_in.shape[2] // N_PROJ
    n_exp = router_w.shape[2]
    n_tok = bsz * seq

    cc = jnp.zeros((MOD_ROWS, d), F32).at[:bsz].set(c).at[bsz].set(c_ctx)
    mod = _modulation(cc, w_mod[0], b_mod[0])
    mod3 = mod.reshape(MOD_ROWS * 6, 1, d)

    w_in_b = w_in[0].astype(BF16)
    norm1_l = norm1[0].reshape(1, d)
    proj_ctx = _inproj(ctx, mod3, lambda b: bsz, norm1_l, w_in_b[:, 2 * width:6 * width],
                       ctx.shape[1], "inproj_ctx")
    states = _scan(proj_ctx, 0, 1, 2, 3, lb_fwd, lb_bwd, None, False, "scan_ctx")
    proj = _inproj(x, mod3, lambda b: b, norm1_l, w_in_b, INPROJ_ROWS, "inproj")
    o_fwd, o_bwd = _scan(proj, 2, 3, 4, 5, lb_fwd, lb_bwd, states, True, "scan")

    rw_t = router_w[0].T
    rw_hi = rw_t.astype(BF16)
    rw_parts = jnp.concatenate([rw_hi, (rw_t - rw_hi.astype(F32)).astype(BF16)], axis=0)
    x1, h2_rows, lt3 = _postmix(
        proj, o_fwd, o_bwd, x, mod3, sgu_ln[0], sgu_w[0].astype(BF16),
        sgu_b[0].reshape(sgu_b.shape[1], sgu_b.shape[2], 1), hgrn_norm[0].reshape(1, -1),
        w_out[0].astype(BF16), norm2[0].reshape(1, d), rw_parts, router_b[0].reshape(n_exp, 1))

    n_blocks = -(-(n_tok * TOP_K + n_exp * (EXPERT_ROWS - 1)) // EXPERT_ROWS)
    dest, gates, pads = _route(lt3)
    tables = pads[:, :, 0]

    xs = _dispatch(tables[:3], dest, h2_rows, n_blocks * EXPERT_ROWS)
    ys = _experts(tables[3], tables[4], xs, w1[0], b1[0], w2[0], b2[0])
    gates_tok = gates.transpose(0, 2, 1).reshape(n_tok, TOP_K)
    out = _combine(dest, ys, gates_tok, x1.reshape(n_tok, d), mod3, final_norm.reshape(1, d), seq)
    return out.reshape(bsz, seq, d)
```

```python
import functools
import math

import jax
import jax.numpy as jnp
import numpy as np
from jax import lax
from jax.experimental import pallas as pl
from jax.experimental.pallas import tpu as pltpu

F32 = jnp.float32
BF16 = jnp.bfloat16
EPS = 1e-6
LOG2_E = 1.4426950408889634
SUBLANES = 8

HEAD_DIM = 128
SGU_CHUNK = 128
N_PROJ = 7
TOP_K = 4
SWIGLU_LIMIT = 7.0
SWIGLU_ALPHA = 1.702

SCAN_CHUNK = 64
SCAN_SAMPLES = 2
INPROJ_ROWS = 512
POSTMIX_ROWS = 512
EXPERT_ROWS = 512
TOKEN_TILE = 128
DISPATCH_RING = 3
COMBINE_RING = 3
MOD_ROWS = 16
VMEM_LIMIT = 56 * 1024 * 1024


def _params(sem, vmem=None):
    return pltpu.CompilerParams(dimension_semantics=sem, vmem_limit_bytes=vmem or VMEM_LIMIT)


def _sigmoid(x):
    return 1.0 / (1.0 + jnp.exp(-x))


def _mod_kernel(c_ref, w_ref, b_ref, o_ref):
    c = c_ref[...]
    s = c * _sigmoid(c)
    o_ref[...] = jnp.dot(s, w_ref[...], precision=lax.Precision.HIGHEST,
                         preferred_element_type=F32) + b_ref[...]


def _modulation(cc, w_mod, b_mod):
    d, n6 = w_mod.shape
    tn = 1024
    return pl.pallas_call(
        _mod_kernel,
        grid=(n6 // tn,),
        in_specs=[pl.BlockSpec((MOD_ROWS, d), lambda j: (0, 0)),
                  pl.BlockSpec((d, tn), lambda j: (0, j)),
                  pl.BlockSpec((1, tn), lambda j: (0, j))],
        out_specs=pl.BlockSpec((MOD_ROWS, tn), lambda j: (0, j)),
        out_shape=jax.ShapeDtypeStruct((MOD_ROWS, n6), F32),
        compiler_params=_params(("arbitrary",)),
        name="mod",
    )(cc, w_mod, b_mod.reshape(1, n6))


def _rms(x, g):
    return x * lax.rsqrt(jnp.mean(x * x, axis=-1, keepdims=True) + EPS) * g


def _inproj_kernel(x_ref, sh_ref, sc_ref, g_ref, w_ref, o_ref, *, width):
    h = _rms(x_ref[0], g_ref[...])
    h = h * (1.0 + sc_ref[0]) + sh_ref[0]
    hb = h.astype(BF16)
    for j in range(w_ref.shape[1] // width):
        cols = slice(j * width, (j + 1) * width)
        o_ref[0, :, cols] = jnp.dot(hb, w_ref[:, cols],
                                    preferred_element_type=F32).astype(o_ref.dtype)


def _inproj(x, mod3, mod_row, norm_g, w_bf16, rows, name):
    bsz, n, d = x.shape
    wcols = w_bf16.shape[1]
    return pl.pallas_call(
        functools.partial(_inproj_kernel, width=512),
        grid=(bsz, n // rows),
        in_specs=[pl.BlockSpec((1, rows, d), lambda b, j: (b, j, 0)),
                  pl.BlockSpec((1, 1, d), lambda b, j: (mod_row(b) * 6 + 0, 0, 0)),
                  pl.BlockSpec((1, 1, d), lambda b, j: (mod_row(b) * 6 + 1, 0, 0)),
                  pl.BlockSpec((1, d), lambda b, j: (0, 0)),
                  pl.BlockSpec((d, wcols), lambda b, j: (0, 0))],
        out_specs=pl.BlockSpec((1, rows, wcols), lambda b, j: (b, j, 0)),
        out_shape=jax.ShapeDtypeStruct((bsz, n, wcols), BF16),
        compiler_params=_params(("arbitrary", "arbitrary")),
        name=name,
    )(x, mod3, mod3, norm_g, w_bf16)


def _scan_constants(length):
    levels = int(math.log2(length))
    assert 1 << levels == length
    t = np.arange(length)
    sums = np.zeros((2, levels + 1, length, length), np.float32)
    masks = np.zeros((2, levels + 1, length, length), np.float32)
    for lv in range(levels):
        half = 1 << lv
        blk = t // (2 * half)
        mid = blk * 2 * half + half
        later = (t % (2 * half)) >= half
        for r in range(length):
            if later[r]:
                sums[0, lv, r, mid[r] + 1:r + 1] = 1.0
            else:
                sums[0, lv, r, r + 1:mid[r] + 1] = 1.0
        masks[0, lv] = (blk[:, None] == blk[None, :]) & later[:, None] & (~later)[None, :]
    sums[0, levels] = np.tril(np.ones((length, length), np.float32))
    masks[0, levels] = np.eye(length, dtype=np.float32)
    sums[1] = sums[0][:, ::-1, ::-1]
    masks[1] = masks[0][:, ::-1, ::-1]
    n_lv = levels + 1
    kept = [lv for lv in range(levels) if (1 << lv) < SUBLANES] + [levels]
    big = np.zeros((len(kept), 2, length, 2, 2, length), np.float32)
    pair = np.zeros((n_lv, 2, length, 2, length), np.float32)
    for d in range(2):
        big[:, d, :, 0, d, :] = sums[d][kept]
        big[:, d, :, 1, d, :] = sums[d][kept]
        pair[:, d, :, d, :] = masks[d]
    return (big.reshape(len(kept) * 2 * length, 4 * length),
            pair.reshape(n_lv, 2 * length, 2 * length), len(kept) - 1)


def _coarse_exponent(bp, half, length):
    parts = []
    for d in range(2):
        for start in range(0, length, 2 * half):
            lo = d * length + start
            mid = lo + half
            ref_row = mid if d == 0 else mid - 1
            ref = jnp.broadcast_to(bp[ref_row:ref_row + 1, :], (half, bp.shape[1]))
            first, second = bp[lo:mid, :], bp[mid:mid + half, :]
            parts += [ref - first, second - ref] if d == 0 else [first - ref, ref - second]
    return jnp.concatenate(parts, axis=0)


def _nt_dot(a, b):
    return lax.dot_general(a, b, (((1,), (1,)), ((), ())), preferred_element_type=F32)


def _lower_bound(tab):
    ex = jnp.exp(tab - jnp.max(tab, axis=0, keepdims=True))
    return ex[0:1] / jnp.sum(ex, axis=0, keepdims=True)


def _scan_kernel(*refs, heads, n_fine, emit_o, has_init):
    qf_ref, zf_ref, if_ref, qb_ref, zb_ref, ib_ref, tabf_ref, tabb_ref, sums_ref, masks_ref = refs[:10]
    pos = 10
    init_ref = None
    if has_init:
        init_ref = refs[pos]
        pos += 1
    n_out = 2 if emit_o else 1
    out_refs = refs[pos:pos + n_out]
    state_ref, d_scr = refs[pos + n_out:]

    c = pl.program_id(1)
    n_chunks = pl.num_programs(1)
    n_samples, length, width = qf_ref.shape
    levels = masks_ref.shape[0] - 1
    rows2 = 2 * length
    groups = n_samples * heads

    @pl.when(c == 0)
    def _():
        if has_init:
            state_ref[...] = init_ref[...].reshape(state_ref.shape)
        else:
            state_ref[...] = jnp.zeros_like(state_ref)

    def both(f_ref, b_ref):
        return jnp.concatenate(
            [jnp.concatenate([r[s] for s in range(n_samples)], axis=1) for r in (f_ref, b_ref)],
            axis=0)

    def per_dir(a, b):
        return jnp.concatenate([jnp.broadcast_to(a, (length, a.shape[1])),
                                jnp.broadcast_to(b, (length, b.shape[1]))], axis=0)

    def per_sample(a):
        return jnp.concatenate([a] * n_samples, axis=1)

    is_fwd = lax.broadcasted_iota(jnp.int32, (rows2, HEAD_DIM), 0) < length

    def split_dirs(a):
        return jnp.concatenate([jnp.where(is_fwd, a, 0.0), jnp.where(is_fwd, 0.0, a)], axis=1)

    lb = per_dir(per_sample(_lower_bound(tabf_ref[...])), per_sample(_lower_bound(tabb_ref[...])))
    z = both(zf_ref, zb_ref).astype(F32)
    q = both(qf_ref, qb_ref).astype(F32)
    v = both(if_ref, ib_ref)
    f = lb + (1.0 - lb) * _sigmoid(z)
    k = 1.0 - f
    g = jnp.log(f) * LOG2_E
    g_hi = g.astype(BF16)
    g_lo = (g - g_hi.astype(F32)).astype(BF16)
    g_parts = jnp.concatenate([g_hi, g_lo], axis=0)
    if emit_o:
        d_scr[...] = jnp.dot(sums_ref[...], g_parts, preferred_element_type=F32)
    else:
        d_scr[...] = jnp.dot(sums_ref[n_fine * rows2:(n_fine + 1) * rows2, :], g_parts,
                             preferred_element_type=F32)
    prefix_rows = slice(n_fine * rows2, (n_fine + 1) * rows2) if emit_o else slice(0, rows2)
    bp = d_scr[prefix_rows, :]
    btot_f = bp[length - 1:length]
    btot_b = bp[length:length + 1]

    if emit_o:
        qd = q * jnp.exp2(bp)
        o_carry = []
        for gi in range(groups):
            cols = slice(gi * HEAD_DIM, (gi + 1) * HEAD_DIM)
            o_carry.append(_nt_dot(split_dirs(qd[:, cols]).astype(BF16),
                                   state_ref[gi].astype(BF16)))
    kd = k * jnp.exp2(per_dir(btot_f, btot_b) - bp)
    dec_f = jnp.exp2(btot_f)
    dec_b = jnp.exp2(btot_b)
    for gi in range(groups):
        cols = slice(gi * HEAD_DIM, (gi + 1) * HEAD_DIM)
        dec = jnp.concatenate([dec_f[:, cols], dec_b[:, cols]], axis=1)
        vt = v[:, cols].astype(F32).T.astype(BF16)
        state_ref[gi] = state_ref[gi] * dec + jnp.dot(
            vt, split_dirs(kd[:, cols]).astype(BF16), preferred_element_type=F32)

    if emit_o:
        of_ref, ob_ref = out_refs
        scores = [None] * groups
        for lv in range(levels + 1):
            if lv == levels:
                xq, yk = q.astype(BF16), k.astype(BF16)
            else:
                if lv < n_fine:
                    dl = d_scr[lv * rows2:(lv + 1) * rows2, :]
                else:
                    dl = _coarse_exponent(bp, 1 << lv, length)
                e = jnp.exp2(dl)
                xq, yk = (q * e).astype(BF16), (k * e).astype(BF16)
            for gi in range(groups):
                cols = slice(gi * HEAD_DIM, (gi + 1) * HEAD_DIM)
                p = masks_ref[lv] * _nt_dot(xq[:, cols], yk[:, cols])
                scores[gi] = p if scores[gi] is None else scores[gi] + p
        for gi in range(groups):
            cols = slice(gi * HEAD_DIM, (gi + 1) * HEAD_DIM)
            out_cols = slice((gi % heads) * HEAD_DIM, (gi % heads + 1) * HEAD_DIM)
            o = o_carry[gi] + jnp.dot(scores[gi].astype(BF16), v[:, cols],
                                      preferred_element_type=F32)
            of_ref[gi // heads, :, out_cols] = o[:length].astype(of_ref.dtype)
            ob_ref[gi // heads, :, out_cols] = o[length:].astype(ob_ref.dtype)

    if not emit_o:
        @pl.when(c == n_chunks - 1)
        def _():
            out_refs[0][...] = state_ref[...].reshape(out_refs[0].shape)


def _scan(proj, col_q, col_zf, col_zb, col_i, tab_f, tab_b, init, emit_o, name):
    bsz, n, _ = proj.shape
    width = tab_f.shape[1]
    heads = width // HEAD_DIM
    length = SCAN_CHUNK
    n_chunks = n // length
    sums_np, masks_np, n_fine = _scan_constants(length)
    sums = jnp.asarray(sums_np, BF16)
    masks = jnp.asarray(masks_np, F32)
    d_rows = sums.shape[0] if emit_o else 2 * length
    ns = SCAN_SAMPLES

    def fwd(col):
        return pl.BlockSpec((ns, length, width), lambda b, c: (b, c, col))

    def bwd(col):
        return pl.BlockSpec((ns, length, width), lambda b, c: (b, n_chunks - 1 - c, col))

    def whole(a):
        nd = a.ndim
        return pl.BlockSpec(a.shape, lambda b, c: (0,) * nd)

    in_specs = [fwd(col_q), fwd(col_zf), fwd(col_i), bwd(col_q), bwd(col_zb), bwd(col_i),
                whole(tab_f), whole(tab_b), whole(sums), whole(masks)]
    args = [proj, proj, proj, proj, proj, proj, tab_f, tab_b, sums, masks]
    state_shape = (heads, HEAD_DIM, 2 * HEAD_DIM)
    if init is not None:
        in_specs.append(pl.BlockSpec((ns,) + state_shape, lambda b, c: (b, 0, 0, 0)))
        args.append(init)
    if emit_o:
        out_spec = [pl.BlockSpec((ns, length, width), lambda b, c: (b, c, 0)),
                    pl.BlockSpec((ns, length, width), lambda b, c: (b, n_chunks - 1 - c, 0))]
        out_shape = [jax.ShapeDtypeStruct((bsz, n, width), BF16)] * 2
    else:
        out_spec = pl.BlockSpec((ns,) + state_shape, lambda b, c: (b, 0, 0, 0))
        out_shape = jax.ShapeDtypeStruct((bsz,) + state_shape, F32)
    return pl.pallas_call(
        functools.partial(_scan_kernel, heads=heads, n_fine=n_fine, emit_o=emit_o,
                          has_init=init is not None),
        grid=(bsz // ns, n_chunks),
        in_specs=in_specs,
        out_specs=out_spec,
        out_shape=out_shape,
        scratch_shapes=[pltpu.VMEM((ns * heads,) + state_shape[1:], F32),
                        pltpu.VMEM((d_rows, ns * width), F32)],
        compiler_params=_params(("arbitrary", "arbitrary")),
        name=name,
    )(*args)


def _postmix_kernel(u_ref, v_ref, g_ref, of_ref, ob_ref, x_ref, g1_ref, sh2_ref, sc2_ref,
                    ln_ref, ws_ref, bs_ref, hn_ref, wout_ref, n2_ref, rw_ref, rb_ref,
                    x1_ref, h2_ref, lg_ref, y_scr):
    rows = x_ref.shape[1]
    heads = ln_ref.shape[0]
    sgu_w = heads * HEAD_DIM
    u = jax.nn.gelu(u_ref[0].astype(F32))
    v = jax.nn.gelu(v_ref[0].astype(F32))
    for ch in range(rows // SGU_CHUNK):
        rs = slice(ch * SGU_CHUNK, (ch + 1) * SGU_CHUNK)
        for h in range(heads):
            cols = slice(h * HEAD_DIM, (h + 1) * HEAD_DIM)
            vh = v[rs, cols]
            mu = jnp.mean(vh, axis=-1, keepdims=True)
            var = jnp.mean(jnp.square(vh - mu), axis=-1, keepdims=True)
            vn = (vh - mu) * lax.rsqrt(var + EPS) * ln_ref[h:h + 1, :]
            z = jnp.dot(ws_ref[h], vn.astype(BF16), preferred_element_type=F32) + bs_ref[h]
            y_scr[rs, cols] = (u[rs, cols] * z).astype(BF16)
    o = of_ref[0].astype(F32) + ob_ref[0].astype(F32)
    gate = g_ref[0].astype(F32)
    gate = gate * _sigmoid(gate)
    for h in range(o.shape[1] // HEAD_DIM):
        cols = slice(h * HEAD_DIM, (h + 1) * HEAD_DIM)
        oh = o[:, cols]
        on = oh * lax.rsqrt(jnp.mean(oh * oh, axis=-1, keepdims=True) + EPS)
        y_scr[:, sgu_w + h * HEAD_DIM:sgu_w + (h + 1) * HEAD_DIM] = (
            on * hn_ref[:, cols] * gate[:, cols]).astype(BF16)
    y = jnp.dot(y_scr[...], wout_ref[...], preferred_element_type=F32)
    x1 = x_ref[0] + g1_ref[0] * y
    x1_ref[0] = x1
    h2 = _rms(x1, n2_ref[...]) * (1.0 + sc2_ref[0]) + sh2_ref[0]
    h2_ref[...] = h2.reshape(h2_ref.shape)
    n_exp = rb_ref.shape[0]
    h_hi = h2.astype(BF16)
    h_lo = (h2 - h_hi.astype(F32)).astype(BF16)
    part = _nt_dot(rw_ref[...], h_hi)
    lg = part[:n_exp] + part[n_exp:] + _nt_dot(rw_ref[:n_exp, :], h_lo) + rb_ref[...]
    for s in range(rows // TOKEN_TILE):
        lg_ref[s] = lg[:, s * TOKEN_TILE:(s + 1) * TOKEN_TILE]


def _postmix(proj, o_fwd, o_bwd, x, mod3, sgu_ln, sgu_w_bf16, sgu_b3, hgrn_norm, w_out_bf16,
             norm2, router_w, router_b):
    bsz, n, d = x.shape
    width = o_fwd.shape[2]
    rows = POSTMIX_ROWS
    n_exp = router_b.shape[0]
    nj = n // rows
    sub = rows // TOKEN_TILE

    def colgroup(col):
        return pl.BlockSpec((1, rows, width), lambda b, j: (b, j, col))

    def modrow(idx):
        return pl.BlockSpec((1, 1, d), lambda b, j: (b * 6 + idx, 0, 0))

    def whole(a):
        nd = a.ndim
        return pl.BlockSpec(a.shape, lambda b, j: (0,) * nd)

    return pl.pallas_call(
        _postmix_kernel,
        grid=(bsz, nj),
        in_specs=[colgroup(0), colgroup(1), colgroup(6), colgroup(0), colgroup(0),
                  pl.BlockSpec((1, rows, d), lambda b, j: (b, j, 0)),
                  modrow(2), modrow(3), modrow(4),
                  whole(sgu_ln), whole(sgu_w_bf16), whole(sgu_b3), whole(hgrn_norm),
                  whole(w_out_bf16), whole(norm2), whole(router_w), whole(router_b)],
        out_specs=[pl.BlockSpec((1, rows, d), lambda b, j: (b, j, 0)),
                   pl.BlockSpec((rows, 1, d), lambda b, j: (b * nj + j, 0, 0)),
                   pl.BlockSpec((sub, n_exp, TOKEN_TILE), lambda b, j: (b * nj + j, 0, 0))],
        out_shape=[jax.ShapeDtypeStruct((bsz, n, d), F32),
                   jax.ShapeDtypeStruct((bsz * n, 1, d), F32),
                   jax.ShapeDtypeStruct((bsz * n // TOKEN_TILE, n_exp, TOKEN_TILE), F32)],
        scratch_shapes=[pltpu.VMEM((rows, d), BF16)],
        compiler_params=_params(("arbitrary", "arbitrary")),
        name="postmix",
    )(proj, proj, proj, o_fwd, o_bwd, x, mod3, mod3, mod3, sgu_ln, sgu_w_bf16, sgu_b3, hgrn_norm,
      w_out_bf16, norm2, router_w, router_b)


def _route_kernel(lt_ref, upper_ref, ones_ref, ltri_ref,
                  dest_ref, gate_ref, pads_ref, pos_scr, eidx_scr, *, block_rows):
    n_tiles, n_exp, lanes = lt_ref.shape
    iota_e = lax.broadcasted_iota(jnp.int32, (n_exp, lanes), 0).astype(F32)

    def tile_body(j, carry):
        l = lt_ref[j]
        picks, vals = [], []
        for _ in range(TOP_K):
            m = jnp.max(l, axis=0, keepdims=True)
            idx = jnp.min(jnp.where(l == m, iota_e, float(n_exp)), axis=0, keepdims=True)
            pick = iota_e == idx
            picks.append(pick)
            vals.append(m)
            l = jnp.where(pick, -jnp.inf, l)
            eidx_scr[j, len(picks) - 1:len(picks), :] = idx
        exps = [jnp.exp(v - vals[0]) for v in vals]
        tot = exps[0]
        for e in exps[1:]:
            tot = tot + e
        for kk in range(TOP_K):
            gate_ref[j, kk:kk + 1, :] = exps[kk] / tot
        chosen = picks[0]
        for p in picks[1:]:
            chosen = chosen | p
        cb = jnp.where(chosen, 1.0, 0.0).astype(BF16)
        before = carry + jnp.dot(cb, upper_ref[...], preferred_element_type=F32)
        for kk in range(TOP_K):
            pos_scr[j, kk:kk + 1, :] = jnp.sum(jnp.where(picks[kk], before, 0.0),
                                               axis=0, keepdims=True)
        return carry + jnp.dot(cb, ones_ref[...], preferred_element_type=F32)

    counts = lax.fori_loop(0, n_tiles, tile_body, jnp.zeros((n_exp, lanes), F32))
    n_blk = jnp.ceil(counts * (1.0 / block_rows))
    end_blk = jnp.dot(ltri_ref[...], n_blk.astype(BF16), preferred_element_type=F32)
    start_row = (end_blk - n_blk) * block_rows

    def dest_body(j, _):
        for kk in range(TOP_K):
            idx = eidx_scr[j, kk:kk + 1, :]
            base = jnp.sum(jnp.where(iota_e == idx, start_row, 0.0), axis=0, keepdims=True)
            dest_ref[j, kk:kk + 1, :] = (base + pos_scr[j, kk:kk + 1, :]).astype(jnp.int32)
        return 0

    lax.fori_loop(0, n_tiles, dest_body, 0)

    pads_ref[0] = (start_row + counts).astype(jnp.int32)
    pads_ref[1] = (n_blk * block_rows - counts).astype(jnp.int32)
    pads_ref[2] = end_blk.astype(jnp.int32)
    pads_ref[3] = start_row.astype(jnp.int32)
    pads_ref[4] = n_blk.astype(jnp.int32)


def _route(lt3):
    n_tiles, n_exp, lanes = lt3.shape
    upper = jnp.asarray(np.triu(np.ones((lanes, lanes), np.float32), 1), BF16)
    ones = jnp.ones((lanes, lanes), BF16)
    ltri = jnp.asarray(np.tril(np.ones((n_exp, n_exp), np.float32)), BF16)
    vm = pl.BlockSpec(memory_space=pltpu.VMEM)
    return pl.pallas_call(
        functools.partial(_route_kernel, block_rows=EXPERT_ROWS),
        in_specs=[vm, vm, vm, vm],
        out_specs=[vm, vm, vm],
        out_shape=[jax.ShapeDtypeStruct((n_tiles, TOP_K, lanes), jnp.int32),
                   jax.ShapeDtypeStruct((n_tiles, TOP_K, lanes), F32),
                   jax.ShapeDtypeStruct((5, n_exp, lanes), jnp.int32)],
        scratch_shapes=[pltpu.VMEM((n_tiles, TOP_K, lanes), F32),
                        pltpu.VMEM((n_tiles, TOP_K, lanes), F32)],
        compiler_params=pltpu.CompilerParams(vmem_limit_bytes=VMEM_LIMIT),
        name="route",
    )(lt3, upper, ones, ltri)


def _pad_sizes():
    return [1 << b for b in reversed(range(int(math.log2(EXPERT_ROWS))))]


def _dispatch_kernel(pads_ref, dest_ref, h2_hbm, xs_hbm, zero_scr, ring_scr, sem, load_sem, zsem):
    j = pl.program_id(0)
    n_steps = pl.num_programs(0)
    tile = dest_ref.shape[2]
    n_exp = pads_ref.shape[1]

    def pad_copies(e, wait):
        off = pads_ref[0, e]
        cnt = pads_ref[1, e]
        for sz in _pad_sizes():
            cp = pltpu.make_async_copy(zero_scr.at[pl.ds(0, sz)], xs_hbm.at[pl.ds(off, sz)], zsem)

            @pl.when((cnt & sz) != 0)
            def _():
                if wait:
                    cp.wait()
                else:
                    cp.start()
            off = off + (cnt & sz)

    half = zero_scr.shape[0]
    n_blocks = xs_hbm.shape[0] // (2 * half)
    n_used = pads_ref[2, n_exp - 1]

    def tail_copies(blk, wait):
        for part in range(2):
            cp = pltpu.make_async_copy(
                zero_scr, xs_hbm.at[pl.ds((blk * 2 + part) * half, half)], zsem)
            if wait:
                cp.wait()
            else:
                cp.start()

    @pl.when(j == 0)
    def _():
        zero_scr[...] = jnp.zeros_like(zero_scr)

        def start(e, _):
            pad_copies(e, False)
            return 0
        lax.fori_loop(0, n_exp, start, 0)

        def start_tail(blk, _):
            tail_copies(blk, False)
            return 0
        lax.fori_loop(n_used, n_blocks, start_tail, 0)

    ring = ring_scr.shape[0] // tile

    def load(step):
        slot = step % ring
        return pltpu.make_async_copy(h2_hbm.at[pl.ds(step * tile, tile)],
                                     ring_scr.at[pl.ds(slot * tile, tile)], load_sem.at[slot])

    def finish_rows(step):
        for kk in range(TOP_K):
            pltpu.make_async_copy(ring_scr.at[pl.ds(0, tile)], xs_hbm.at[pl.ds(0, tile)],
                                  sem.at[step % 2]).wait()

    @pl.when(j == 0)
    def _():
        load(0).start()
        load(1).start()

    load(j).wait()
    base = (j % ring) * tile
    for i in range(tile):
        for kk in range(TOP_K):
            pltpu.make_async_copy(ring_scr.at[base + i], xs_hbm.at[dest_ref[0, kk, i]],
                                  sem.at[j % 2]).start(priority=kk % 2)

    @pl.when(j > 0)
    def _():
        finish_rows(j - 1)

    @pl.when(j + 2 < n_steps)
    def _():
        load(j + 2).start()

    @pl.when(j == n_steps - 1)
    def _():
        finish_rows(j)

        def finish(e, _):
            pad_copies(e, True)
            return 0
        lax.fori_loop(0, n_exp, finish, 0)

        def finish_tail(blk, _):
            tail_copies(blk, True)
            return 0
        lax.fori_loop(n_used, n_blocks, finish_tail, 0)


def _dispatch(pads, dest, h2_rows, n_rows):
    n_tok, _, d = h2_rows.shape
    n_tiles = dest.shape[0]
    assert n_tiles >= 2
    grid_spec = pltpu.PrefetchScalarGridSpec(
        num_scalar_prefetch=1,
        grid=(n_tiles,),
        in_specs=[pl.BlockSpec((1, TOP_K, TOKEN_TILE), lambda j, p: (j, 0, 0),
                               memory_space=pltpu.SMEM),
                  pl.BlockSpec(memory_space=pl.ANY)],
        out_specs=pl.BlockSpec(memory_space=pl.ANY),
        scratch_shapes=[pltpu.VMEM((EXPERT_ROWS // 2, 1, d), F32),
                        pltpu.VMEM((DISPATCH_RING * TOKEN_TILE, 1, d), F32),
                        pltpu.SemaphoreType.DMA((2,)),
                        pltpu.SemaphoreType.DMA((DISPATCH_RING,)),
                        pltpu.SemaphoreType.DMA(())],
    )
    return pl.pallas_call(
        _dispatch_kernel,
        grid_spec=grid_spec,
        out_shape=jax.ShapeDtypeStruct((n_rows, 1, d), F32),
        compiler_params=_params(("arbitrary",)),
        name="dispatch",
    )(pads, dest, h2_rows)


def _experts_kernel(start_ref, nblk_ref, xs_hbm, w1_ref, b1_ref, w2_ref, b2_ref, ys_hbm,
                    w1_scr, w2_scr, xb0, xb1, yb0, yb1, x_scr, act_scr, load_sem, store_sem):
    e = pl.program_id(0)
    n_blk = nblk_ref[e]
    row0 = start_ref[e]
    rows = xb0.shape[0]
    d, two_f = w1_scr.shape
    f = two_f // 2
    xbufs, ybufs = (xb0, xb1), (yb0, yb1)

    def load(i, s):
        return pltpu.make_async_copy(xs_hbm.at[pl.ds(row0 + i * rows, rows)], xbufs[s],
                                     load_sem.at[s])

    def store(i, s):
        return pltpu.make_async_copy(ybufs[s], ys_hbm.at[pl.ds(row0 + i * rows, rows)],
                                     store_sem.at[s])

    @pl.when(n_blk > 0)
    def _():
        load(0, 0).start()
        step = 64

        def cast1(r, _):
            rs = pl.ds(pl.multiple_of(r * step, step), step)
            w1_scr[rs, :] = w1_ref[0, rs, :].astype(BF16)
            return 0
        lax.fori_loop(0, d // step, cast1, 0)

        def cast2(r, _):
            rs = pl.ds(pl.multiple_of(r * step, step), step)
            w2_scr[rs, :] = w2_ref[0, rs, :].astype(BF16)
            return 0
        lax.fori_loop(0, f // step, cast2, 0)

    def block(i, s):
        load(i, s).wait()

        @pl.when(i + 1 < n_blk)
        def _():
            load(i + 1, 1 - s).start()

        x_scr[...] = xbufs[s][...].reshape(x_scr.shape)
        xb = x_scr[...].astype(BF16)
        width = 512
        for n in range(f // width):
            cg = slice(n * width, (n + 1) * width)
            cu = slice(f + n * width, f + (n + 1) * width)
            gate = jnp.dot(xb, w1_scr[:, cg], preferred_element_type=F32) + b1_ref[0, :, cg]
            up = jnp.dot(xb, w1_scr[:, cu], preferred_element_type=F32) + b1_ref[0, :, cu]
            gate = jnp.minimum(gate, SWIGLU_LIMIT)
            up = jnp.clip(up, -SWIGLU_LIMIT, SWIGLU_LIMIT)
            act_scr[:, cg] = (gate * _sigmoid(SWIGLU_ALPHA * gate) * (up + 1.0)).astype(BF16)
        out = jnp.dot(act_scr[...], w2_scr[...], preferred_element_type=F32) + b2_ref[0]

        @pl.when(i >= 2)
        def _():
            store(i - 2, s).wait()

        ybufs[s][...] = out.reshape(ybufs[s].shape)
        store(i, s).start()

    def pair(p, _):
        block(2 * p, 0)

        @pl.when(2 * p + 1 < n_blk)
        def _():
            block(2 * p + 1, 1)
        return 0
    lax.fori_loop(0, lax.shift_right_logical(n_blk + 1, 1), pair, 0)

    for back in (1, 2):
        for s in range(2):
            @pl.when((n_blk >= back) & ((n_blk - back) % 2 == s))
            def _(back=back, s=s):
                store(n_blk - back, s).wait()


def _experts(start_row, n_blk, xs, w1, b1, w2, b2):
    n_rows, _, d = xs.shape
    n_exp, _, two_f = w1.shape
    f = two_f // 2
    rows = EXPERT_ROWS
    grid_spec = pltpu.PrefetchScalarGridSpec(
        num_scalar_prefetch=2,
        grid=(n_exp,),
        in_specs=[pl.BlockSpec(memory_space=pl.ANY),
                  pl.BlockSpec((1, d, two_f), lambda e, st, nb: (e, 0, 0)),
                  pl.BlockSpec((1, 1, two_f), lambda e, st, nb: (e, 0, 0)),
                  pl.BlockSpec((1, f, d), lambda e, st, nb: (e, 0, 0)),
                  pl.BlockSpec((1, 1, d), lambda e, st, nb: (e, 0, 0))],
        out_specs=pl.BlockSpec(memory_space=pl.ANY),
        scratch_shapes=[pltpu.VMEM((d, two_f), BF16),
                        pltpu.VMEM((f, d), BF16)]
        + [pltpu.VMEM((rows, 1, d), F32) for _ in range(4)]
        + [pltpu.VMEM((rows, d), F32),
           pltpu.VMEM((rows, f), BF16),
           pltpu.SemaphoreType.DMA((2,)),
           pltpu.SemaphoreType.DMA((2,))],
    )
    return pl.pallas_call(
        _experts_kernel,
        grid_spec=grid_spec,
        out_shape=jax.ShapeDtypeStruct((n_rows, 1, d), F32),
        input_output_aliases={2: 0},
        compiler_params=_params(("arbitrary",)),
        name="experts",
    )(start_row, n_blk, xs, w1, b1.reshape(n_exp, 1, two_f), w2, b2.reshape(n_exp, 1, d))


def _combine_kernel(dest_ref, ahead1_ref, ahead2_ref, ys_hbm, gate_ref, x1_ref, g2_ref, fn_ref,
                    o_ref, ring_scr, flat_scr, sem):
    j = pl.program_id(0)
    n_steps = pl.num_programs(0)
    tile = dest_ref.shape[2]
    ring = ring_scr.shape[0] // (TOP_K * tile)

    def slot_of(step):
        return step % ring

    def issue(idx_ref, slot, i):
        for kk in range(TOP_K):
            pltpu.make_async_copy(ys_hbm.at[idx_ref[0, kk, i]],
                                  ring_scr.at[(slot * TOP_K + kk) * tile + i],
                                  sem.at[slot]).start(priority=kk % 2)

    def drain(slot):
        for kk in range(TOP_K):
            pltpu.make_async_copy(ys_hbm.at[pl.ds(0, tile)], ring_scr.at[pl.ds(0, tile)],
                                  sem.at[slot]).wait()

    @pl.when(j == 0)
    def _():
        for idx_ref, slot in ((dest_ref, 0), (ahead1_ref, 1)):
            def body(i, _, idx_ref=idx_ref, slot=slot):
                issue(idx_ref, slot, i)
                return 0
            lax.fori_loop(0, tile, body, 0)

    slot = slot_of(j)
    drain(slot)
    ahead_slot = slot_of(j + 2)
    for i in range(tile):
        issue(ahead2_ref, ahead_slot, i)
    y = None
    for kk in range(TOP_K):
        rows = pl.ds(pl.multiple_of((slot * TOP_K + kk) * tile, tile), tile)
        flat_scr[...] = ring_scr[rows].reshape(flat_scr.shape)
        term = gate_ref[:, kk:kk + 1] * flat_scr[...]
        y = term if y is None else y + term
    x2 = x1_ref[...] + g2_ref[0] * y
    o_ref[...] = _rms(x2, fn_ref[...])

    @pl.when(j == n_steps - 1)
    def _():
        drain(slot_of(j + 1))
        drain(slot_of(j + 2))


def _combine(dest, ys, gates_tok, x1, mod3, final_norm, seq):
    n_tok, d = x1.shape
    n_tiles = dest.shape[0]
    tiles_per_sample = seq // TOKEN_TILE
    return pl.pallas_call(
        _combine_kernel,
        grid=(n_tiles,),
        in_specs=[pl.BlockSpec((1, TOP_K, TOKEN_TILE), lambda j: (j, 0, 0),
                               memory_space=pltpu.SMEM),
                  pl.BlockSpec((1, TOP_K, TOKEN_TILE),
                               lambda j: (jnp.minimum(j + 1, n_tiles - 1), 0, 0),
                               memory_space=pltpu.SMEM),
                  pl.BlockSpec((1, TOP_K, TOKEN_TILE),
                               lambda j: (jnp.minimum(j + 2, n_tiles - 1), 0, 0),
                               memory_space=pltpu.SMEM),
                  pl.BlockSpec(memory_space=pl.ANY),
                  pl.BlockSpec((TOKEN_TILE, TOP_K), lambda j: (j, 0)),
                  pl.BlockSpec((TOKEN_TILE, d), lambda j: (j, 0)),
                  pl.BlockSpec((1, 1, d), lambda j: ((j // tiles_per_sample) * 6 + 5, 0, 0)),
                  pl.BlockSpec((1, d), lambda j: (0, 0))],
        out_specs=pl.BlockSpec((TOKEN_TILE, d), lambda j: (j, 0)),
        out_shape=jax.ShapeDtypeStruct((n_tok, d), F32),
        scratch_shapes=[pltpu.VMEM((COMBINE_RING * TOP_K * TOKEN_TILE, 1, d), F32),
                        pltpu.VMEM((TOKEN_TILE, d), F32),
                        pltpu.SemaphoreType.DMA((COMBINE_RING,))],
        compiler_params=_params(("arbitrary",)),
        name="combine",
    )(dest, dest, dest, ys, gates_tok, x1, mod3, final_norm)


def kernel(x, c, ctx, c_ctx, w_mod, b_mod, norm1, w_in, sgu_ln, sgu_w, sgu_b, lb_fwd, lb_bwd,
           hgrn_norm, w_out, norm2, router_w, router_b, w1, b1, w2, b2, final_norm):
    depth = w_mod.shape[0]
    assert depth == 1, "single-layer block only"
    bsz, seq, d = x.shape
    assert bsz + 1 <= MOD_ROWS
    width = w_in.shape[2] // N_PROJ
    n_exp = router_w.shape[2]
    n_tok = bsz * seq

    cc = jnp.zeros((MOD_ROWS, d), F32).at[:bsz].set(c).at[bsz].set(c_ctx)
    mod = _modulation(cc, w_mod[0], b_mod[0])
    mod3 = mod.reshape(MOD_ROWS * 6, 1, d)

    w_in_b = w_in[0].astype(BF16)
    norm1_l = norm1[0].reshape(1, d)
    proj_ctx = _inproj(ctx, mod3, lambda b: bsz, norm1_l, w_in_b[:, 2 * width:6 * width],
                       ctx.shape[1], "inproj_ctx")
    states = _scan(proj_ctx, 0, 1, 2, 3, lb_fwd, lb_bwd, None, False, "scan_ctx")
    proj = _inproj(x, mod3, lambda b: b, norm1_l, w_in_b, INPROJ_ROWS, "inproj")
    o_fwd, o_bwd = _scan(proj, 2, 3, 4, 5, lb_fwd, lb_bwd, states, True, "scan")

    rw_t = router_w[0].T
    rw_hi = rw_t.astype(BF16)
    rw_parts = jnp.concatenate([rw_hi, (rw_t - rw_hi.astype(F32)).astype(BF16)], axis=0)
    x1, h2_rows, lt3 = _postmix(
        proj, o_fwd, o_bwd, x, mod3, sgu_ln[0], sgu_w[0].astype(BF16),
        sgu_b[0].reshape(sgu_b.shape[1], sgu_b.shape[2], 1), hgrn_norm[0].reshape(1, -1),
        w_out[0].astype(BF16), norm2[0].reshape(1, d), rw_parts, router_b[0].reshape(n_exp, 1))

    n_blocks = -(-(n_tok * TOP_K + n_exp * (EXPERT_ROWS - 1)) // EXPERT_ROWS)
    dest, gates, pads = _route(lt3)
    tables = pads[:, :, 0]

    xs = _dispatch(tables[:3], dest, h2_rows, n_blocks * EXPERT_ROWS)
    ys = _experts(tables[3], tables[4], xs, w1[0], b1[0], w2[0], b2[0])
    gates_tok = gates.transpose(0, 2, 1).reshape(n_tok, TOP_K)
    out = _combine(dest, ys, gates_tok, x1.reshape(n_tok, d), mod3, final_norm.reshape(1, d), seq)
    return out.reshape(bsz, seq, d)
```

```python
import functools
import math

import jax
import jax.numpy as jnp
import numpy as np
from jax import lax
from jax.experimental import pallas as pl
from jax.experimental.pallas import tpu as pltpu

F32 = jnp.float32
BF16 = jnp.bfloat16
EPS = 1e-6
LOG2_E = 1.4426950408889634
SUBLANES = 8

HEAD_DIM = 128
SGU_CHUNK = 128
N_PROJ = 7
TOP_K = 4
SWIGLU_LIMIT = 7.0
SWIGLU_ALPHA = 1.702

SCAN_CHUNK = 64
SCAN_SAMPLES = 4
INPROJ_ROWS = 512
POSTMIX_ROWS = 512
EXPERT_ROWS = 512
TOKEN_TILE = 128
DISPATCH_RING = 3
COMBINE_RING = 3
MOD_ROWS = 16
VMEM_LIMIT = 56 * 1024 * 1024


def _params(sem, vmem=None):
    return pltpu.CompilerParams(dimension_semantics=sem, vmem_limit_bytes=vmem or VMEM_LIMIT)


def _sigmoid(x):
    return 1.0 / (1.0 + jnp.exp(-x))


def _mod_kernel(c_ref, w_ref, b_ref, o_ref):
    c = c_ref[...]
    s = c * _sigmoid(c)
    o_ref[...] = jnp.dot(s, w_ref[...], precision=lax.Precision.HIGHEST,
                         preferred_element_type=F32) + b_ref[...]


def _modulation(cc, w_mod, b_mod):
    d, n6 = w_mod.shape
    tn = 1024
    return pl.pallas_call(
        _mod_kernel,
        grid=(n6 // tn,),
        in_specs=[pl.BlockSpec((MOD_ROWS, d), lambda j: (0, 0)),
                  pl.BlockSpec((d, tn), lambda j: (0, j)),
                  pl.BlockSpec((1, tn), lambda j: (0, j))],
        out_specs=pl.BlockSpec((MOD_ROWS, tn), lambda j: (0, j)),
        out_shape=jax.ShapeDtypeStruct((MOD_ROWS, n6), F32),
        compiler_params=_params(("arbitrary",)),
        name="mod",
    )(cc, w_mod, b_mod.reshape(1, n6))


def _rms(x, g):
    return x * lax.rsqrt(jnp.mean(x * x, axis=-1, keepdims=True) + EPS) * g


def _inproj_kernel(x_ref, sh_ref, sc_ref, g_ref, w_ref, o_ref, *, width):
    h = _rms(x_ref[0], g_ref[...])
    h = h * (1.0 + sc_ref[0]) + sh_ref[0]
    hb = h.astype(BF16)
    for j in range(w_ref.shape[1] // width):
        cols = slice(j * width, (j + 1) * width)
        o_ref[0, :, cols] = jnp.dot(hb, w_ref[:, cols],
                                    preferred_element_type=F32).astype(o_ref.dtype)


def _inproj(x, mod3, mod_row, norm_g, w_bf16, rows, name):
    bsz, n, d = x.shape
    wcols = w_bf16.shape[1]
    return pl.pallas_call(
        functools.partial(_inproj_kernel, width=512),
        grid=(bsz, n // rows),
        in_specs=[pl.BlockSpec((1, rows, d), lambda b, j: (b, j, 0)),
                  pl.BlockSpec((1, 1, d), lambda b, j: (mod_row(b) * 6 + 0, 0, 0)),
                  pl.BlockSpec((1, 1, d), lambda b, j: (mod_row(b) * 6 + 1, 0, 0)),
                  pl.BlockSpec((1, d), lambda b, j: (0, 0)),
                  pl.BlockSpec((d, wcols), lambda b, j: (0, 0))],
        out_specs=pl.BlockSpec((1, rows, wcols), lambda b, j: (b, j, 0)),
        out_shape=jax.ShapeDtypeStruct((bsz, n, wcols), BF16),
        compiler_params=_params(("arbitrary", "arbitrary")),
        name=name,
    )(x, mod3, mod3, norm_g, w_bf16)


def _scan_constants(length):
    levels = int(math.log2(length))
    assert 1 << levels == length
    t = np.arange(length)
    sums = np.zeros((2, levels + 1, length, length), np.float32)
    masks = np.zeros((2, levels + 1, length, length), np.float32)
    for lv in range(levels):
        half = 1 << lv
        blk = t // (2 * half)
        mid = blk * 2 * half + half
        later = (t % (2 * half)) >= half
        for r in range(length):
            if later[r]:
                sums[0, lv, r, mid[r] + 1:r + 1] = 1.0
            else:
                sums[0, lv, r, r + 1:mid[r] + 1] = 1.0
        masks[0, lv] = (blk[:, None] == blk[None, :]) & later[:, None] & (~later)[None, :]
    sums[0, levels] = np.tril(np.ones((length, length), np.float32))
    masks[0, levels] = np.eye(length, dtype=np.float32)
    sums[1] = sums[0][:, ::-1, ::-1]
    masks[1] = masks[0][:, ::-1, ::-1]
    n_lv = levels + 1
    kept = [lv for lv in range(levels) if (1 << lv) < SUBLANES] + [levels]
    big = np.zeros((len(kept), 2, length, 2, 2, length), np.float32)
    pair = np.zeros((n_lv, 2, length, 2, length), np.float32)
    for d in range(2):
        big[:, d, :, 0, d, :] = sums[d][kept]
        big[:, d, :, 1, d, :] = sums[d][kept]
        pair[:, d, :, d, :] = masks[d]
    return (big.reshape(len(kept) * 2 * length, 4 * length),
            pair.reshape(n_lv, 2 * length, 2 * length), len(kept) - 1)


def _coarse_exponent(bp, half, length):
    parts = []
    for d in range(2):
        for start in range(0, length, 2 * half):
            lo = d * length + start
            mid = lo + half
            ref_row = mid if d == 0 else mid - 1
            ref = jnp.broadcast_to(bp[ref_row:ref_row + 1, :], (half, bp.shape[1]))
            first, second = bp[lo:mid, :], bp[mid:mid + half, :]
            parts += [ref - first, second - ref] if d == 0 else [first - ref, ref - second]
    return jnp.concatenate(parts, axis=0)


def _nt_dot(a, b):
    return lax.dot_general(a, b, (((1,), (1,)), ((), ())), preferred_element_type=F32)


def _lower_bound(tab):
    ex = jnp.exp(tab - jnp.max(tab, axis=0, keepdims=True))
    return ex[0:1] / jnp.sum(ex, axis=0, keepdims=True)


def _scan_kernel(*refs, heads, n_fine, emit_o, has_init):
    qf_ref, zf_ref, if_ref, qb_ref, zb_ref, ib_ref, tabf_ref, tabb_ref, sums_ref, masks_ref = refs[:10]
    pos = 10
    init_ref = None
    if has_init:
        init_ref = refs[pos]
        pos += 1
    n_out = 2 if emit_o else 1
    out_refs = refs[pos:pos + n_out]
    state_ref, d_scr = refs[pos + n_out:]

    c = pl.program_id(1)
    n_chunks = pl.num_programs(1)
    n_samples, length, width = qf_ref.shape
    levels = masks_ref.shape[0] - 1
    rows2 = 2 * length
    groups = n_samples * heads

    @pl.when(c == 0)
    def _():
        if has_init:
            state_ref[...] = init_ref[...].reshape(state_ref.shape)
        else:
            state_ref[...] = jnp.zeros_like(state_ref)

    def both(f_ref, b_ref):
        return jnp.concatenate(
            [jnp.concatenate([r[s] for s in range(n_samples)], axis=1) for r in (f_ref, b_ref)],
            axis=0)

    def per_dir(a, b):
        return jnp.concatenate([jnp.broadcast_to(a, (length, a.shape[1])),
                                jnp.broadcast_to(b, (length, b.shape[1]))], axis=0)

    def per_sample(a):
        return jnp.concatenate([a] * n_samples, axis=1)

    is_fwd = lax.broadcasted_iota(jnp.int32, (rows2, HEAD_DIM), 0) < length

    def split_dirs(a):
        return jnp.concatenate([jnp.where(is_fwd, a, 0.0), jnp.where(is_fwd, 0.0, a)], axis=1)

    lb = per_dir(per_sample(_lower_bound(tabf_ref[...])), per_sample(_lower_bound(tabb_ref[...])))
    z = both(zf_ref, zb_ref).astype(F32)
    q = both(qf_ref, qb_ref).astype(F32)
    v = both(if_ref, ib_ref)
    f = lb + (1.0 - lb) * _sigmoid(z)
    k = 1.0 - f
    g = jnp.log(f) * LOG2_E
    g_hi = g.astype(BF16)
    g_lo = (g - g_hi.astype(F32)).astype(BF16)
    g_parts = jnp.concatenate([g_hi, g_lo], axis=0)
    if emit_o:
        d_scr[...] = jnp.dot(sums_ref[...], g_parts, preferred_element_type=F32)
    else:
        d_scr[...] = jnp.dot(sums_ref[n_fine * rows2:(n_fine + 1) * rows2, :], g_parts,
                             preferred_element_type=F32)
    prefix_rows = slice(n_fine * rows2, (n_fine + 1) * rows2) if emit_o else slice(0, rows2)
    bp = d_scr[prefix_rows, :]
    btot_f = bp[length - 1:length]
    btot_b = bp[length:length + 1]

    if emit_o:
        qd = q * jnp.exp2(bp)
        o_carry = []
        for gi in range(groups):
            cols = slice(gi * HEAD_DIM, (gi + 1) * HEAD_DIM)
            o_carry.append(_nt_dot(split_dirs(qd[:, cols]).astype(BF16),
                                   state_ref[gi].astype(BF16)))
    kd = k * jnp.exp2(per_dir(btot_f, btot_b) - bp)
    dec_f = jnp.exp2(btot_f)
    dec_b = jnp.exp2(btot_b)
    for gi in range(groups):
        cols = slice(gi * HEAD_DIM, (gi + 1) * HEAD_DIM)
        dec = jnp.concatenate([dec_f[:, cols], dec_b[:, cols]], axis=1)
        vt = v[:, cols].astype(F32).T.astype(BF16)
        state_ref[gi] = state_ref[gi] * dec + jnp.dot(
            vt, split_dirs(kd[:, cols]).astype(BF16), preferred_element_type=F32)

    if emit_o:
        of_ref, ob_ref = out_refs
        scores = [None] * groups
        q_bf, k_bf = q.astype(BF16), k.astype(BF16)
        for lv in range(levels + 1):
            if lv == levels:
                xq, yk = q_bf, k_bf
            else:
                if lv < n_fine:
                    dl = d_scr[lv * rows2:(lv + 1) * rows2, :]
                else:
                    dl = _coarse_exponent(bp, 1 << lv, length)
                e = jnp.exp2(dl).astype(BF16)
                xq, yk = q_bf * e, k_bf * e
            keep = masks_ref[lv] > 0.5
            for gi in range(groups):
                cols = slice(gi * HEAD_DIM, (gi + 1) * HEAD_DIM)
                p = _nt_dot(xq[:, cols], yk[:, cols])
                scores[gi] = jnp.where(keep, p, 0.0 if scores[gi] is None else scores[gi])
        for gi in range(groups):
            cols = slice(gi * HEAD_DIM, (gi + 1) * HEAD_DIM)
            out_cols = slice((gi % heads) * HEAD_DIM, (gi % heads + 1) * HEAD_DIM)
            o = o_carry[gi] + jnp.dot(scores[gi].astype(BF16), v[:, cols],
                                      preferred_element_type=F32)
            of_ref[gi // heads, :, out_cols] = o[:length].astype(of_ref.dtype)
            ob_ref[gi // heads, :, out_cols] = o[length:].astype(ob_ref.dtype)

    if not emit_o:
        @pl.when(c == n_chunks - 1)
        def _():
            out_refs[0][...] = state_ref[...].reshape(out_refs[0].shape)


def _scan(proj, col_q, col_zf, col_zb, col_i, tab_f, tab_b, init, emit_o, name):
    bsz, n, _ = proj.shape
    width = tab_f.shape[1]
    heads = width // HEAD_DIM
    length = SCAN_CHUNK
    n_chunks = n // length
    sums_np, masks_np, n_fine = _scan_constants(length)
    sums = jnp.asarray(sums_np, BF16)
    masks = jnp.asarray(masks_np, F32)
    d_rows = sums.shape[0] if emit_o else 2 * length
    ns = SCAN_SAMPLES

    def fwd(col):
        return pl.BlockSpec((ns, length, width), lambda b, c: (b, c, col))

    def bwd(col):
        return pl.BlockSpec((ns, length, width), lambda b, c: (b, n_chunks - 1 - c, col))

    def whole(a):
        nd = a.ndim
        return pl.BlockSpec(a.shape, lambda b, c: (0,) * nd)

    in_specs = [fwd(col_q), fwd(col_zf), fwd(col_i), bwd(col_q), bwd(col_zb), bwd(col_i),
                whole(tab_f), whole(tab_b), whole(sums), whole(masks)]
    args = [proj, proj, proj, proj, proj, proj, tab_f, tab_b, sums, masks]
    state_shape = (heads, HEAD_DIM, 2 * HEAD_DIM)
    if init is not None:
        in_specs.append(pl.BlockSpec((ns,) + state_shape, lambda b, c: (b, 0, 0, 0)))
        args.append(init)
    if emit_o:
        out_spec = [pl.BlockSpec((ns, length, width), lambda b, c: (b, c, 0)),
                    pl.BlockSpec((ns, length, width), lambda b, c: (b, n_chunks - 1 - c, 0))]
        out_shape = [jax.ShapeDtypeStruct((bsz, n, width), BF16)] * 2
    else:
        out_spec = pl.BlockSpec((ns,) + state_shape, lambda b, c: (b, 0, 0, 0))
        out_shape = jax.ShapeDtypeStruct((bsz,) + state_shape, F32)
    return pl.pallas_call(
        functools.partial(_scan_kernel, heads=heads, n_fine=n_fine, emit_o=emit_o,
                          has_init=init is not None),
        grid=(bsz // ns, n_chunks),
        in_specs=in_specs,
        out_specs=out_spec,
        out_shape=out_shape,
        scratch_shapes=[pltpu.VMEM((ns * heads,) + state_shape[1:], F32),
                        pltpu.VMEM((d_rows, ns * width), F32)],
        compiler_params=_params(("arbitrary", "arbitrary")),
        name=name,
    )(*args)


def _postmix_kernel(u_ref, v_ref, g_ref, of_ref, ob_ref, x_ref, g1_ref, sh2_ref, sc2_ref,
                    ln_ref, ws_ref, bs_ref, hn_ref, wout_ref, n2_ref, rw_ref, rb_ref,
                    x1_ref, h2_ref, lg_ref, y_scr):
    rows = x_ref.shape[1]
    heads = ln_ref.shape[0]
    sgu_w = heads * HEAD_DIM
    u = jax.nn.gelu(u_ref[0].astype(F32))
    v = jax.nn.gelu(v_ref[0].astype(F32))
    for ch in range(rows // SGU_CHUNK):
        rs = slice(ch * SGU_CHUNK, (ch + 1) * SGU_CHUNK)
        for h in range(heads):
            cols = slice(h * HEAD_DIM, (h + 1) * HEAD_DIM)
            vh = v[rs, cols]
            mu = jnp.mean(vh, axis=-1, keepdims=True)
            var = jnp.mean(jnp.square(vh - mu), axis=-1, keepdims=True)
            vn = (vh - mu) * lax.rsqrt(var + EPS) * ln_ref[h:h + 1, :]
            z = jnp.dot(ws_ref[h], vn.astype(BF16), preferred_element_type=F32) + bs_ref[h]
            y_scr[rs, cols] = (u[rs, cols] * z).astype(BF16)
    o = of_ref[0].astype(F32) + ob_ref[0].astype(F32)
    gate = g_ref[0].astype(F32)
    gate = gate * _sigmoid(gate)
    for h in range(o.shape[1] // HEAD_DIM):
        cols = slice(h * HEAD_DIM, (h + 1) * HEAD_DIM)
        oh = o[:, cols]
        on = oh * lax.rsqrt(jnp.mean(oh * oh, axis=-1, keepdims=True) + EPS)
        y_scr[:, sgu_w + h * HEAD_DIM:sgu_w + (h + 1) * HEAD_DIM] = (
            on * hn_ref[:, cols] * gate[:, cols]).astype(BF16)
    y = jnp.dot(y_scr[...], wout_ref[...], preferred_element_type=F32)
    x1 = x_ref[0] + g1_ref[0] * y
    x1_ref[0] = x1
    h2 = _rms(x1, n2_ref[...]) * (1.0 + sc2_ref[0]) + sh2_ref[0]
    h2_ref[...] = h2.reshape(h2_ref.shape)
    n_exp = rb_ref.shape[0]
    h_hi = h2.astype(BF16)
    h_lo = (h2 - h_hi.astype(F32)).astype(BF16)
    part = _nt_dot(rw_ref[...], h_hi)
    lg = part[:n_exp] + part[n_exp:] + _nt_dot(rw_ref[:n_exp, :], h_lo) + rb_ref[...]
    for s in range(rows // TOKEN_TILE):
        lg_ref[s] = lg[:, s * TOKEN_TILE:(s + 1) * TOKEN_TILE]


def _postmix(proj, o_fwd, o_bwd, x, mod3, sgu_ln, sgu_w_bf16, sgu_b3, hgrn_norm, w_out_bf16,
             norm2, router_w, router_b):
    bsz, n, d = x.shape
    width = o_fwd.shape[2]
    rows = POSTMIX_ROWS
    n_exp = router_b.shape[0]
    nj = n // rows
    sub = rows // TOKEN_TILE

    def colgroup(col):
        return pl.BlockSpec((1, rows, width), lambda b, j: (b, j, col))

    def modrow(idx):
        return pl.BlockSpec((1, 1, d), lambda b, j: (b * 6 + idx, 0, 0))

    def whole(a):
        nd = a.ndim
        return pl.BlockSpec(a.shape, lambda b, j: (0,) * nd)

    return pl.pallas_call(
        _postmix_kernel,
        grid=(bsz, nj),
        in_specs=[colgroup(0), colgroup(1), colgroup(6), colgroup(0), colgroup(0),
                  pl.BlockSpec((1, rows, d), lambda b, j: (b, j, 0)),
                  modrow(2), modrow(3), modrow(4),
                  whole(sgu_ln), whole(sgu_w_bf16), whole(sgu_b3), whole(hgrn_norm),
                  whole(w_out_bf16), whole(norm2), whole(router_w), whole(router_b)],
        out_specs=[pl.BlockSpec((1, rows, d), lambda b, j: (b, j, 0)),
                   pl.BlockSpec((rows, 1, d), lambda b, j: (b * nj + j, 0, 0)),
                   pl.BlockSpec((sub, n_exp, TOKEN_TILE), lambda b, j: (b * nj + j, 0, 0))],
        out_shape=[jax.ShapeDtypeStruct((bsz, n, d), F32),
                   jax.ShapeDtypeStruct((bsz * n, 1, d), F32),
                   jax.ShapeDtypeStruct((bsz * n // TOKEN_TILE, n_exp, TOKEN_TILE), F32)],
        scratch_shapes=[pltpu.VMEM((rows, d), BF16)],
        compiler_params=_params(("arbitrary", "arbitrary")),
        name="postmix",
    )(proj, proj, proj, o_fwd, o_bwd, x, mod3, mod3, mod3, sgu_ln, sgu_w_bf16, sgu_b3, hgrn_norm,
      w_out_bf16, norm2, router_w, router_b)


def _route_kernel(lt_ref, upper_ref, ones_ref, ltri_ref,
                  dest_ref, gate_ref, pads_ref, pos_scr, eidx_scr, *, block_rows):
    n_tiles, n_exp, lanes = lt_ref.shape
    iota_e = lax.broadcasted_iota(jnp.int32, (n_exp, lanes), 0).astype(F32)

    def tile_body(j, carry):
        l = lt_ref[j]
        picks, vals = [], []
        for _ in range(TOP_K):
            m = jnp.max(l, axis=0, keepdims=True)
            idx = jnp.min(jnp.where(l == m, iota_e, float(n_exp)), axis=0, keepdims=True)
            pick = iota_e == idx
            picks.append(pick)
            vals.append(m)
            l = jnp.where(pick, -jnp.inf, l)
            eidx_scr[j, len(picks) - 1:len(picks), :] = idx
        exps = [jnp.exp(v - vals[0]) for v in vals]
        tot = exps[0]
        for e in exps[1:]:
            tot = tot + e
        for kk in range(TOP_K):
            gate_ref[j, kk:kk + 1, :] = exps[kk] / tot
        chosen = picks[0]
        for p in picks[1:]:
            chosen = chosen | p
        cb = jnp.where(chosen, 1.0, 0.0).astype(BF16)
        before = carry + jnp.dot(cb, upper_ref[...], preferred_element_type=F32)
        for kk in range(TOP_K):
            pos_scr[j, kk:kk + 1, :] = jnp.sum(jnp.where(picks[kk], before, 0.0),
                                               axis=0, keepdims=True)
        return carry + jnp.dot(cb, ones_ref[...], preferred_element_type=F32)

    counts = lax.fori_loop(0, n_tiles, tile_body, jnp.zeros((n_exp, lanes), F32))
    n_blk = jnp.ceil(counts * (1.0 / block_rows))
    end_blk = jnp.dot(ltri_ref[...], n_blk.astype(BF16), preferred_element_type=F32)
    start_row = (end_blk - n_blk) * block_rows

    def dest_body(j, _):
        for kk in range(TOP_K):
            idx = eidx_scr[j, kk:kk + 1, :]
            base = jnp.sum(jnp.where(iota_e == idx, start_row, 0.0), axis=0, keepdims=True)
            dest_ref[j, kk:kk + 1, :] = (base + pos_scr[j, kk:kk + 1, :]).astype(jnp.int32)
        return 0

    lax.fori_loop(0, n_tiles, dest_body, 0)

    pads_ref[0] = (start_row + counts).astype(jnp.int32)
    pads_ref[1] = (n_blk * block_rows - counts).astype(jnp.int32)
    pads_ref[2] = end_blk.astype(jnp.int32)
    pads_ref[3] = start_row.astype(jnp.int32)
    pads_ref[4] = n_blk.astype(jnp.int32)


def _route(lt3):
    n_tiles, n_exp, lanes = lt3.shape
    upper = jnp.asarray(np.triu(np.ones((lanes, lanes), np.float32), 1), BF16)
    ones = jnp.ones((lanes, lanes), BF16)
    ltri = jnp.asarray(np.tril(np.ones((n_exp, n_exp), np.float32)), BF16)
    vm = pl.BlockSpec(memory_space=pltpu.VMEM)
    return pl.pallas_call(
        functools.partial(_route_kernel, block_rows=EXPERT_ROWS),
        in_specs=[vm, vm, vm, vm],
        out_specs=[vm, vm, vm],
        out_shape=[jax.ShapeDtypeStruct((n_tiles, TOP_K, lanes), jnp.int32),
                   jax.ShapeDtypeStruct((n_tiles, TOP_K, lanes), F32),
                   jax.ShapeDtypeStruct((5, n_exp, lanes), jnp.int32)],
        scratch_shapes=[pltpu.VMEM((n_tiles, TOP_K, lanes), F32),
                        pltpu.VMEM((n_tiles, TOP_K, lanes), F32)],
        compiler_params=pltpu.CompilerParams(vmem_limit_bytes=VMEM_LIMIT),
        name="route",
    )(lt3, upper, ones, ltri)


def _pad_sizes():
    return [1 << b for b in reversed(range(int(math.log2(EXPERT_ROWS))))]


def _dispatch_kernel(pads_ref, dest_ref, h2_hbm, xs_hbm, zero_scr, ring_scr, sem, load_sem, zsem):
    j = pl.program_id(0)
    n_steps = pl.num_programs(0)
    tile = dest_ref.shape[2]
    n_exp = pads_ref.shape[1]

    def pad_copies(e, wait):
        off = pads_ref[0, e]
        cnt = pads_ref[1, e]
        for sz in _pad_sizes():
            cp = pltpu.make_async_copy(zero_scr.at[pl.ds(0, sz)], xs_hbm.at[pl.ds(off, sz)], zsem)

            @pl.when((cnt & sz) != 0)
            def _():
                if wait:
                    cp.wait()
                else:
                    cp.start()
            off = off + (cnt & sz)

    half = zero_scr.shape[0]
    n_blocks = xs_hbm.shape[0] // (2 * half)
    n_used = pads_ref[2, n_exp - 1]

    def tail_copies(blk, wait):
        for part in range(2):
            cp = pltpu.make_async_copy(
                zero_scr, xs_hbm.at[pl.ds((blk * 2 + part) * half, half)], zsem)
            if wait:
                cp.wait()
            else:
                cp.start()

    @pl.when(j == 0)
    def _():
        zero_scr[...] = jnp.zeros_like(zero_scr)

        def start(e, _):
            pad_copies(e, False)
            return 0
        lax.fori_loop(0, n_exp, start, 0)

        def start_tail(blk, _):
            tail_copies(blk, False)
            return 0
        lax.fori_loop(n_used, n_blocks, start_tail, 0)

    ring = ring_scr.shape[0] // tile

    def load(step):
        slot = step % ring
        return pltpu.make_async_copy(h2_hbm.at[pl.ds(step * tile, tile)],
                                     ring_scr.at[pl.ds(slot * tile, tile)], load_sem.at[slot])

    def finish_rows(step):
        for kk in range(TOP_K):
            pltpu.make_async_copy(ring_scr.at[pl.ds(0, tile)], xs_hbm.at[pl.ds(0, tile)],
                                  sem.at[step % 2]).wait()

    @pl.when(j == 0)
    def _():
        load(0).start()
        load(1).start()

    load(j).wait()
    base = (j % ring) * tile
    for i in range(tile):
        for kk in range(TOP_K):
            pltpu.make_async_copy(ring_scr.at[base + i], xs_hbm.at[dest_ref[0, kk, i]],
                                  sem.at[j % 2]).start(priority=kk % 2)

    @pl.when(j > 0)
    def _():
        finish_rows(j - 1)

    @pl.when(j + 2 < n_steps)
    def _():
        load(j + 2).start()

    @pl.when(j == n_steps - 1)
    def _():
        finish_rows(j)

        def finish(e, _):
            pad_copies(e, True)
            return 0
        lax.fori_loop(0, n_exp, finish, 0)

        def finish_tail(blk, _):
            tail_copies(blk, True)
            return 0
        lax.fori_loop(n_used, n_blocks, finish_tail, 0)


def _dispatch(pads, dest, h2_rows, n_rows):
    n_tok, _, d = h2_rows.shape
    n_tiles = dest.shape[0]
    assert n_tiles >= 2
    grid_spec = pltpu.PrefetchScalarGridSpec(
        num_scalar_prefetch=1,
        grid=(n_tiles,),
        in_specs=[pl.BlockSpec((1, TOP_K, TOKEN_TILE), lambda j, p: (j, 0, 0),
                               memory_space=pltpu.SMEM),
                  pl.BlockSpec(memory_space=pl.ANY)],
        out_specs=pl.BlockSpec(memory_space=pl.ANY),
        scratch_shapes=[pltpu.VMEM((EXPERT_ROWS // 2, 1, d), F32),
                        pltpu.VMEM((DISPATCH_RING * TOKEN_TILE, 1, d), F32),
                        pltpu.SemaphoreType.DMA((2,)),
                        pltpu.SemaphoreType.DMA((DISPATCH_RING,)),
                        pltpu.SemaphoreType.DMA(())],
    )
    return pl.pallas_call(
        _dispatch_kernel,
        grid_spec=grid_spec,
        out_shape=jax.ShapeDtypeStruct((n_rows, 1, d), F32),
        compiler_params=_params(("arbitrary",)),
        name="dispatch",
    )(pads, dest, h2_rows)


def _experts_kernel(start_ref, nblk_ref, xs_hbm, w1_ref, b1_ref, w2_ref, b2_ref, ys_hbm,
                    w1_scr, w2_scr, x_ring, y_ring, x_scr, act_scr, load_sem, store_sem):
    e = pl.program_id(0)
    n_exp = pl.num_programs(0)
    rows = x_scr.shape[0]
    shift = int(math.log2(rows))
    n_blk = nblk_ref[e]
    first = lax.shift_right_logical(start_ref[e], shift)
    n_used = lax.shift_right_logical(start_ref[n_exp - 1], shift) + nblk_ref[n_exp - 1]
    d, two_f = w1_scr.shape
    f = two_f // 2

    def slot_rows(g):
        return pl.ds(pl.multiple_of(jnp.bitwise_and(g, 1) * rows, rows), rows)

    def load(g):
        return pltpu.make_async_copy(xs_hbm.at[pl.ds(g * rows, rows)], x_ring.at[slot_rows(g)],
                                     load_sem.at[jnp.bitwise_and(g, 1)])

    def store(g):
        return pltpu.make_async_copy(y_ring.at[slot_rows(g)], ys_hbm.at[pl.ds(g * rows, rows)],
                                     store_sem.at[jnp.bitwise_and(g, 1)])

    @pl.when((e == 0) & (n_used > 0))
    def _():
        load(0).start()

    @pl.when(n_blk > 0)
    def _():
        step = 64

        def cast1(r, _):
            rs = pl.ds(pl.multiple_of(r * step, step), step)
            w1_scr[rs, :] = w1_ref[0, rs, :].astype(BF16)
            return 0
        lax.fori_loop(0, d // step, cast1, 0)

        def cast2(r, _):
            rs = pl.ds(pl.multiple_of(r * step, step), step)
            w2_scr[rs, :] = w2_ref[0, rs, :].astype(BF16)
            return 0
        lax.fori_loop(0, f // step, cast2, 0)

    def block(i, _):
        g = first + i
        load(g).wait()

        @pl.when(g + 1 < n_used)
        def _():
            load(g + 1).start()

        x_scr[...] = x_ring[slot_rows(g)].reshape(x_scr.shape)
        xb = x_scr[...].astype(BF16)
        width = 512
        for n in range(f // width):
            cg = slice(n * width, (n + 1) * width)
            cu = slice(f + n * width, f + (n + 1) * width)
            gate = jnp.dot(xb, w1_scr[:, cg], preferred_element_type=F32) + b1_ref[0, :, cg]
            up = jnp.dot(xb, w1_scr[:, cu], preferred_element_type=F32) + b1_ref[0, :, cu]
            gate = jnp.minimum(gate, SWIGLU_LIMIT)
            up = jnp.clip(up, -SWIGLU_LIMIT, SWIGLU_LIMIT)
            act_scr[:, cg] = (gate * _sigmoid(SWIGLU_ALPHA * gate) * (up + 1.0)).astype(BF16)
        out = jnp.dot(act_scr[...], w2_scr[...], preferred_element_type=F32) + b2_ref[0]

        @pl.when(g >= 2)
        def _():
            store(g - 2).wait()

        y_ring[slot_rows(g)] = out.reshape((rows, 1, d))
        store(g).start()
        return 0
    lax.fori_loop(0, n_blk, block, 0)

    @pl.when(e == n_exp - 1)
    def _():
        for back in (1, 2):
            @pl.when(n_used >= back)
            def _(back=back):
                store(n_used - back).wait()


def _experts(start_row, n_blk, xs, w1, b1, w2, b2):
    n_rows, _, d = xs.shape
    n_exp, _, two_f = w1.shape
    f = two_f // 2
    rows = EXPERT_ROWS
    grid_spec = pltpu.PrefetchScalarGridSpec(
        num_scalar_prefetch=2,
        grid=(n_exp,),
        in_specs=[pl.BlockSpec(memory_space=pl.ANY),
                  pl.BlockSpec((1, d, two_f), lambda e, st, nb: (e, 0, 0)),
                  pl.BlockSpec((1, 1, two_f), lambda e, st, nb: (e, 0, 0)),
                  pl.BlockSpec((1, f, d), lambda e, st, nb: (e, 0, 0)),
                  pl.BlockSpec((1, 1, d), lambda e, st, nb: (e, 0, 0))],
        out_specs=pl.BlockSpec(memory_space=pl.ANY),
        scratch_shapes=[pltpu.VMEM((d, two_f), BF16),
                        pltpu.VMEM((f, d), BF16)]
        + [pltpu.VMEM((2 * rows, 1, d), F32) for _ in range(2)]
        + [pltpu.VMEM((rows, d), F32),
           pltpu.VMEM((rows, f), BF16),
           pltpu.SemaphoreType.DMA((2,)),
           pltpu.SemaphoreType.DMA((2,))],
    )
    return pl.pallas_call(
        _experts_kernel,
        grid_spec=grid_spec,
        out_shape=jax.ShapeDtypeStruct((n_rows, 1, d), F32),
        input_output_aliases={2: 0},
        compiler_params=_params(("arbitrary",)),
        name="experts",
    )(start_row, n_blk, xs, w1, b1.reshape(n_exp, 1, two_f), w2, b2.reshape(n_exp, 1, d))


def _combine_kernel(dest_ref, ahead1_ref, ahead2_ref, ys_hbm, gate_ref, x1_ref, g2_ref, fn_ref,
                    o_ref, ring_scr, flat_scr, sem):
    j = pl.program_id(0)
    n_steps = pl.num_programs(0)
    tile = dest_ref.shape[2]
    ring = ring_scr.shape[0] // (TOP_K * tile)

    def slot_of(step):
        return step % ring

    def issue(idx_ref, slot, i):
        for kk in range(TOP_K):
            pltpu.make_async_copy(ys_hbm.at[idx_ref[0, kk, i]],
                                  ring_scr.at[(slot * TOP_K + kk) * tile + i],
                                  sem.at[slot]).start(priority=kk % 2)

    def drain(slot):
        for kk in range(TOP_K):
            pltpu.make_async_copy(ys_hbm.at[pl.ds(0, tile)], ring_scr.at[pl.ds(0, tile)],
                                  sem.at[slot]).wait()

    @pl.when(j == 0)
    def _():
        for idx_ref, slot in ((dest_ref, 0), (ahead1_ref, 1)):
            def body(i, _, idx_ref=idx_ref, slot=slot):
                issue(idx_ref, slot, i)
                return 0
            lax.fori_loop(0, tile, body, 0)

    slot = slot_of(j)
    drain(slot)
    ahead_slot = slot_of(j + 2)
    for i in range(tile):
        issue(ahead2_ref, ahead_slot, i)
    y = None
    for kk in range(TOP_K):
        rows = pl.ds(pl.multiple_of((slot * TOP_K + kk) * tile, tile), tile)
        flat_scr[...] = ring_scr[rows].reshape(flat_scr.shape)
        term = gate_ref[:, kk:kk + 1] * flat_scr[...]
        y = term if y is None else y + term
    x2 = x1_ref[...] + g2_ref[0] * y
    o_ref[...] = _rms(x2, fn_ref[...])

    @pl.when(j == n_steps - 1)
    def _():
        drain(slot_of(j + 1))
        drain(slot_of(j + 2))


def _combine(dest, ys, gates_tok, x1, mod3, final_norm, seq):
    n_tok, d = x1.shape
    n_tiles = dest.shape[0]
    tiles_per_sample = seq // TOKEN_TILE
    return pl.pallas_call(
        _combine_kernel,
        grid=(n_tiles,),
        in_specs=[pl.BlockSpec((1, TOP_K, TOKEN_TILE), lambda j: (j, 0, 0),
                               memory_space=pltpu.SMEM),
                  pl.BlockSpec((1, TOP_K, TOKEN_TILE),
                               lambda j: (jnp.minimum(j + 1, n_tiles - 1), 0, 0),
                               memory_space=pltpu.SMEM),
                  pl.BlockSpec((1, TOP_K, TOKEN_TILE),
                               lambda j: (jnp.minimum(j + 2, n_tiles - 1), 0, 0),
                               memory_space=pltpu.SMEM),
                  pl.BlockSpec(memory_space=pl.ANY),
                  pl.BlockSpec((TOKEN_TILE, TOP_K), lambda j: (j, 0)),
                  pl.BlockSpec((TOKEN_TILE, d), lambda j: (j, 0)),
                  pl.BlockSpec((1, 1, d), lambda j: ((j // tiles_per_sample) * 6 + 5, 0, 0)),
                  pl.BlockSpec((1, d), lambda j: (0, 0))],
        out_specs=pl.BlockSpec((TOKEN_TILE, d), lambda j: (j, 0)),
        out_shape=jax.ShapeDtypeStruct((n_tok, d), F32),
        scratch_shapes=[pltpu.VMEM((COMBINE_RING * TOP_K * TOKEN_TILE, 1, d), F32),
                        pltpu.VMEM((TOKEN_TILE, d), F32),
                        pltpu.SemaphoreType.DMA((COMBINE_RING,))],
        compiler_params=_params(("arbitrary",)),
        name="combine",
    )(dest, dest, dest, ys, gates_tok, x1, mod3, final_norm)


def kernel(x, c, ctx, c_ctx, w_mod, b_mod, norm1, w_in, sgu_ln, sgu_w, sgu_b, lb_fwd, lb_bwd,
           hgrn_norm, w_out, norm2, router_w, router_b, w1, b1, w2, b2, final_norm):
    depth = w_mod.shape[0]
    assert depth == 1, "single-layer block only"
    bsz, seq, d = x.shape
    assert bsz + 1 <= MOD_ROWS
    width = w_in.shape[2] // N_PROJ
    n_exp = router_w.shape[2]
    n_tok = bsz * seq

    cc = jnp.zeros((MOD_ROWS, d), F32).at[:bsz].set(c).at[bsz].set(c_ctx)
    mod = _modulation(cc, w_mod[0], b_mod[0])
    mod3 = mod.reshape(MOD_ROWS * 6, 1, d)

    w_in_b = w_in[0].astype(BF16)
    norm1_l = norm1[0].reshape(1, d)
    proj_ctx = _inproj(ctx, mod3, lambda b: bsz, norm1_l, w_in_b[:, 2 * width:6 * width],
                       ctx.shape[1], "inproj_ctx")
    states = _scan(proj_ctx, 0, 1, 2, 3, lb_fwd, lb_bwd, None, False, "scan_ctx")
    proj = _inproj(x, mod3, lambda b: b, norm1_l, w_in_b, INPROJ_ROWS, "inproj")
    o_fwd, o_bwd = _scan(proj, 2, 3, 4, 5, lb_fwd, lb_bwd, states, True, "scan")

    rw_t = router_w[0].T
    rw_hi = rw_t.astype(BF16)
    rw_parts = jnp.concatenate([rw_hi, (rw_t - rw_hi.astype(F32)).astype(BF16)], axis=0)
    x1, h2_rows, lt3 = _postmix(
        proj, o_fwd, o_bwd, x, mod3, sgu_ln[0], sgu_w[0].astype(BF16),
        sgu_b[0].reshape(sgu_b.shape[1], sgu_b.shape[2], 1), hgrn_norm[0].reshape(1, -1),
        w_out[0].astype(BF16), norm2[0].reshape(1, d), rw_parts, router_b[0].reshape(n_exp, 1))

    n_blocks = -(-(n_tok * TOP_K + n_exp * (EXPERT_ROWS - 1)) // EXPERT_ROWS)
    dest, gates, pads = _route(lt3)
    tables = pads[:, :, 0]

    xs = _dispatch(tables[:3], dest, h2_rows, n_blocks * EXPERT_ROWS)
    ys = _experts(tables[3], tables[4], xs, w1[0], b1[0], w2[0], b2[0])
    gates_tok = gates.transpose(0, 2, 1).reshape(n_tok, TOP_K)
    out = _combine(dest, ys, gates_tok, x1.reshape(n_tok, d), mod3, final_norm.reshape(1, d), seq)
    return out.reshape(bsz, seq, d)
```

```python
import functools
import math

import jax
import jax.numpy as jnp
import numpy as np
from jax import lax
from jax.experimental import pallas as pl
from jax.experimental.pallas import tpu as pltpu

F32 = jnp.float32
BF16 = jnp.bfloat16
EPS = 1e-6
LOG2_E = 1.4426950408889634
SUBLANES = 8

HEAD_DIM = 128
SGU_CHUNK = 128
N_PROJ = 7
TOP_K = 4
SWIGLU_LIMIT = 7.0
SWIGLU_ALPHA = 1.702

SCAN_CHUNK = 64
SCAN_SAMPLES = 4
INPROJ_ROWS = 512
POSTMIX_ROWS = 512
EXPERT_ROWS = 512
TOKEN_TILE = 128
DISPATCH_RING = 3
COMBINE_RING = 3
ROUTE_GROUP = 4
MOD_ROWS = 16
VMEM_LIMIT = 56 * 1024 * 1024


def _params(sem, vmem=None):
    return pltpu.CompilerParams(dimension_semantics=sem, vmem_limit_bytes=vmem or VMEM_LIMIT)


def _sigmoid(x):
    return 1.0 / (1.0 + jnp.exp(-x))


def _mod_kernel(c_ref, w_ref, b_ref, o_ref):
    c = c_ref[...]
    s = c * _sigmoid(c)
    o_ref[...] = jnp.dot(s, w_ref[...], precision=lax.Precision.HIGHEST,
                         preferred_element_type=F32) + b_ref[...]


def _modulation(cc, w_mod, b_mod):
    d, n6 = w_mod.shape
    tn = 1024
    return pl.pallas_call(
        _mod_kernel,
        grid=(n6 // tn,),
        in_specs=[pl.BlockSpec((MOD_ROWS, d), lambda j: (0, 0)),
                  pl.BlockSpec((d, tn), lambda j: (0, j)),
                  pl.BlockSpec((1, tn), lambda j: (0, j))],
        out_specs=pl.BlockSpec((MOD_ROWS, tn), lambda j: (0, j)),
        out_shape=jax.ShapeDtypeStruct((MOD_ROWS, n6), F32),
        compiler_params=_params(("arbitrary",)),
        name="mod",
    )(cc, w_mod, b_mod.reshape(1, n6))


def _rms(x, g):
    return x * lax.rsqrt(jnp.mean(x * x, axis=-1, keepdims=True) + EPS) * g


def _rms_modulated(x, g, scale, shift):
    return _rms(x, g * (1.0 + scale)) + shift


def _gelu_tanh(x):
    a = -2.0 * LOG2_E * math.sqrt(2.0 / math.pi)
    return x / (1.0 + jnp.exp2(x * (a + (a * 0.044715) * (x * x))))


def _inproj_kernel(x_ref, sh_ref, sc_ref, g_ref, w_ref, o_ref, *, width):
    hb = _rms_modulated(x_ref[0], g_ref[...], sc_ref[0], sh_ref[0]).astype(BF16)
    for j in range(w_ref.shape[1] // width):
        cols = slice(j * width, (j + 1) * width)
        o_ref[0, :, cols] = jnp.dot(hb, w_ref[:, cols],
                                    preferred_element_type=F32).astype(o_ref.dtype)


def _inproj(x, mod3, mod_row, norm_g, w_bf16, rows, name):
    bsz, n, d = x.shape
    wcols = w_bf16.shape[1]
    return pl.pallas_call(
        functools.partial(_inproj_kernel, width=512),
        grid=(bsz, n // rows),
        in_specs=[pl.BlockSpec((1, rows, d), lambda b, j: (b, j, 0)),
                  pl.BlockSpec((1, 1, d), lambda b, j: (mod_row(b) * 6 + 0, 0, 0)),
                  pl.BlockSpec((1, 1, d), lambda b, j: (mod_row(b) * 6 + 1, 0, 0)),
                  pl.BlockSpec((1, d), lambda b, j: (0, 0)),
                  pl.BlockSpec((d, wcols), lambda b, j: (0, 0))],
        out_specs=pl.BlockSpec((1, rows, wcols), lambda b, j: (b, j, 0)),
        out_shape=jax.ShapeDtypeStruct((bsz, n, wcols), BF16),
        compiler_params=_params(("arbitrary", "arbitrary")),
        name=name,
    )(x, mod3, mod3, norm_g, w_bf16)


def _scan_constants(length):
    levels = int(math.log2(length))
    assert 1 << levels == length
    t = np.arange(length)
    sums = np.zeros((2, levels + 1, length, length), np.float32)
    masks = np.zeros((2, levels + 1, length, length), np.float32)
    for lv in range(levels):
        half = 1 << lv
        blk = t // (2 * half)
        mid = blk * 2 * half + half
        later = (t % (2 * half)) >= half
        for r in range(length):
            if later[r]:
                sums[0, lv, r, mid[r] + 1:r + 1] = 1.0
            else:
                sums[0, lv, r, r + 1:mid[r] + 1] = 1.0
        masks[0, lv] = (blk[:, None] == blk[None, :]) & later[:, None] & (~later)[None, :]
    sums[0, levels] = np.tril(np.ones((length, length), np.float32))
    masks[0, levels] = np.eye(length, dtype=np.float32)
    sums[1] = sums[0][:, ::-1, ::-1]
    masks[1] = masks[0][:, ::-1, ::-1]
    n_lv = levels + 1
    kept = [lv for lv in range(levels) if (1 << lv) < SUBLANES] + [levels]
    big = np.zeros((len(kept), 2, length, 2, 2, length), np.float32)
    pair = np.zeros((n_lv, 2, length, 2, length), np.float32)
    for d in range(2):
        big[:, d, :, 0, d, :] = sums[d][kept]
        big[:, d, :, 1, d, :] = sums[d][kept]
        pair[:, d, :, d, :] = masks[d]
    return (big.reshape(len(kept) * 2 * length, 4 * length),
            pair.reshape(n_lv, 2 * length, 2 * length), len(kept) - 1)


def _coarse_exponent(bp, half, length):
    parts = []
    for d in range(2):
        for start in range(0, length, 2 * half):
            lo = d * length + start
            mid = lo + half
            ref_row = mid if d == 0 else mid - 1
            ref = jnp.broadcast_to(bp[ref_row:ref_row + 1, :], (half, bp.shape[1]))
            first, second = bp[lo:mid, :], bp[mid:mid + half, :]
            parts += [ref - first, second - ref] if d == 0 else [first - ref, ref - second]
    return jnp.concatenate(parts, axis=0)


def _nt_dot(a, b):
    return lax.dot_general(a, b, (((1,), (1,)), ((), ())), preferred_element_type=F32)


def _lower_bound(tab):
    ex = jnp.exp(tab - jnp.max(tab, axis=0, keepdims=True))
    return ex[0:1] / jnp.sum(ex, axis=0, keepdims=True)


def _scan_kernel(*refs, heads, n_fine, emit_o, has_init):
    qf_ref, zf_ref, if_ref, qb_ref, zb_ref, ib_ref, tabf_ref, tabb_ref, sums_ref, masks_ref = refs[:10]
    pos = 10
    init_ref = None
    if has_init:
        init_ref = refs[pos]
        pos += 1
    n_out = 2 if emit_o else 1
    out_refs = refs[pos:pos + n_out]
    state_ref, d_scr = refs[pos + n_out:]

    c = pl.program_id(1)
    n_chunks = pl.num_programs(1)
    n_samples, length, width = qf_ref.shape
    levels = masks_ref.shape[0] - 1
    rows2 = 2 * length
    groups = n_samples * heads

    @pl.when(c == 0)
    def _():
        if has_init:
            state_ref[...] = init_ref[...].reshape(state_ref.shape)
        else:
            state_ref[...] = jnp.zeros_like(state_ref)

    def both(f_ref, b_ref):
        return jnp.concatenate(
            [jnp.concatenate([r[s] for s in range(n_samples)], axis=1) for r in (f_ref, b_ref)],
            axis=0)

    def per_dir(a, b):
        return jnp.concatenate([jnp.broadcast_to(a, (length, a.shape[1])),
                                jnp.broadcast_to(b, (length, b.shape[1]))], axis=0)

    def per_sample(a):
        return jnp.concatenate([a] * n_samples, axis=1)

    is_fwd = lax.broadcasted_iota(jnp.int32, (rows2, HEAD_DIM), 0) < length

    def split_dirs(a):
        return jnp.concatenate([jnp.where(is_fwd, a, 0.0), jnp.where(is_fwd, 0.0, a)], axis=1)

    lb = per_dir(per_sample(_lower_bound(tabf_ref[...])), per_sample(_lower_bound(tabb_ref[...])))
    z = both(zf_ref, zb_ref).astype(F32)
    q = both(qf_ref, qb_ref).astype(F32)
    v = both(if_ref, ib_ref)
    f = lb + (1.0 - lb) * _sigmoid(z)
    k = 1.0 - f
    g = jnp.log(f) * LOG2_E
    g_hi = g.astype(BF16)
    g_lo = (g - g_hi.astype(F32)).astype(BF16)
    g_parts = jnp.concatenate([g_hi, g_lo], axis=0)
    if emit_o:
        d_scr[...] = jnp.dot(sums_ref[...], g_parts, preferred_element_type=F32)
    else:
        d_scr[...] = jnp.dot(sums_ref[n_fine * rows2:(n_fine + 1) * rows2, :], g_parts,
                             preferred_element_type=F32)
    prefix_rows = slice(n_fine * rows2, (n_fine + 1) * rows2) if emit_o else slice(0, rows2)
    bp = d_scr[prefix_rows, :]
    btot_f = bp[length - 1:length]
    btot_b = bp[length:length + 1]

    if emit_o:
        qd = q * jnp.exp2(bp)
        o_carry = []
        for gi in range(groups):
            cols = slice(gi * HEAD_DIM, (gi + 1) * HEAD_DIM)
            o_carry.append(_nt_dot(split_dirs(qd[:, cols]).astype(BF16),
                                   state_ref[gi].astype(BF16)))
    kd = k * jnp.exp2(per_dir(btot_f, btot_b) - bp)
    dec_f = jnp.exp2(btot_f)
    dec_b = jnp.exp2(btot_b)
    for gi in range(groups):
        cols = slice(gi * HEAD_DIM, (gi + 1) * HEAD_DIM)
        dec = jnp.concatenate([dec_f[:, cols], dec_b[:, cols]], axis=1)
        vt = v[:, cols].astype(F32).T.astype(BF16)
        state_ref[gi] = state_ref[gi] * dec + jnp.dot(
            vt, split_dirs(kd[:, cols]).astype(BF16), preferred_element_type=F32)

    if emit_o:
        of_ref, ob_ref = out_refs
        scores = [None] * groups
        q_bf, k_bf = q.astype(BF16), k.astype(BF16)
        for lv in range(levels + 1):
            if lv == levels:
                xq, yk = q_bf, k_bf
            else:
                if lv < n_fine:
                    dl = d_scr[lv * rows2:(lv + 1) * rows2, :]
                else:
                    dl = _coarse_exponent(bp, 1 << lv, length)
                e = jnp.exp2(dl).astype(BF16)
                xq, yk = q_bf * e, k_bf * e
            keep = masks_ref[lv] > 0.5
            for gi in range(groups):
                cols = slice(gi * HEAD_DIM, (gi + 1) * HEAD_DIM)
                p = _nt_dot(xq[:, cols], yk[:, cols])
                scores[gi] = jnp.where(keep, p, 0.0 if scores[gi] is None else scores[gi])
        for gi in range(groups):
            cols = slice(gi * HEAD_DIM, (gi + 1) * HEAD_DIM)
            out_cols = slice((gi % heads) * HEAD_DIM, (gi % heads + 1) * HEAD_DIM)
            o = o_carry[gi] + jnp.dot(scores[gi].astype(BF16), v[:, cols],
                                      preferred_element_type=F32)
            of_ref[gi // heads, :, out_cols] = o[:length].astype(of_ref.dtype)
            ob_ref[gi // heads, :, out_cols] = o[length:].astype(ob_ref.dtype)

    if not emit_o:
        @pl.when(c == n_chunks - 1)
        def _():
            out_refs[0][...] = state_ref[...].reshape(out_refs[0].shape)


def _scan(proj, col_q, col_zf, col_zb, col_i, tab_f, tab_b, init, emit_o, name):
    bsz, n, _ = proj.shape
    width = tab_f.shape[1]
    heads = width // HEAD_DIM
    length = SCAN_CHUNK
    n_chunks = n // length
    sums_np, masks_np, n_fine = _scan_constants(length)
    sums = jnp.asarray(sums_np, BF16)
    masks = jnp.asarray(masks_np, F32)
    d_rows = sums.shape[0] if emit_o else 2 * length
    ns = SCAN_SAMPLES

    def fwd(col):
        return pl.BlockSpec((ns, length, width), lambda b, c: (b, c, col))

    def bwd(col):
        return pl.BlockSpec((ns, length, width), lambda b, c: (b, n_chunks - 1 - c, col))

    def whole(a):
        nd = a.ndim
        return pl.BlockSpec(a.shape, lambda b, c: (0,) * nd)

    in_specs = [fwd(col_q), fwd(col_zf), fwd(col_i), bwd(col_q), bwd(col_zb), bwd(col_i),
                whole(tab_f), whole(tab_b), whole(sums), whole(masks)]
    args = [proj, proj, proj, proj, proj, proj, tab_f, tab_b, sums, masks]
    state_shape = (heads, HEAD_DIM, 2 * HEAD_DIM)
    if init is not None:
        in_specs.append(pl.BlockSpec((ns,) + state_shape, lambda b, c: (b, 0, 0, 0)))
        args.append(init)
    if emit_o:
        out_spec = [pl.BlockSpec((ns, length, width), lambda b, c: (b, c, 0)),
                    pl.BlockSpec((ns, length, width), lambda b, c: (b, n_chunks - 1 - c, 0))]
        out_shape = [jax.ShapeDtypeStruct((bsz, n, width), BF16)] * 2
    else:
        out_spec = pl.BlockSpec((ns,) + state_shape, lambda b, c: (b, 0, 0, 0))
        out_shape = jax.ShapeDtypeStruct((bsz,) + state_shape, F32)
    return pl.pallas_call(
        functools.partial(_scan_kernel, heads=heads, n_fine=n_fine, emit_o=emit_o,
                          has_init=init is not None),
        grid=(bsz // ns, n_chunks),
        in_specs=in_specs,
        out_specs=out_spec,
        out_shape=out_shape,
        scratch_shapes=[pltpu.VMEM((ns * heads,) + state_shape[1:], F32),
                        pltpu.VMEM((d_rows, ns * width), F32)],
        compiler_params=_params(("arbitrary", "arbitrary")),
        name=name,
    )(*args)


def _postmix_kernel(u_ref, v_ref, g_ref, of_ref, ob_ref, x_ref, g1_ref, sh2_ref, sc2_ref,
                    ln_ref, ws_ref, bs_ref, hn_ref, wout_ref, n2_ref, rw_ref, rb_ref,
                    x1_ref, h2_ref, lg_ref, y_scr):
    rows = x_ref.shape[1]
    heads = ln_ref.shape[0]
    sgu_w = heads * HEAD_DIM
    u = _gelu_tanh(u_ref[0].astype(F32))
    v = _gelu_tanh(v_ref[0].astype(F32))
    for ch in range(rows // SGU_CHUNK):
        rs = slice(ch * SGU_CHUNK, (ch + 1) * SGU_CHUNK)
        for h in range(heads):
            cols = slice(h * HEAD_DIM, (h + 1) * HEAD_DIM)
            vh = v[rs, cols]
            mu = jnp.mean(vh, axis=-1, keepdims=True)
            var = jnp.mean(jnp.square(vh - mu), axis=-1, keepdims=True)
            vn = (vh - mu) * lax.rsqrt(var + EPS) * ln_ref[h:h + 1, :]
            z = jnp.dot(ws_ref[h], vn.astype(BF16), preferred_element_type=F32) + bs_ref[h]
            y_scr[rs, cols] = (u[rs, cols] * z).astype(BF16)
    o = of_ref[0].astype(F32) + ob_ref[0].astype(F32)
    gate = g_ref[0].astype(F32)
    gate = gate * _sigmoid(gate)
    for h in range(o.shape[1] // HEAD_DIM):
        cols = slice(h * HEAD_DIM, (h + 1) * HEAD_DIM)
        oh = o[:, cols]
        on = oh * lax.rsqrt(jnp.mean(oh * oh, axis=-1, keepdims=True) + EPS)
        y_scr[:, sgu_w + h * HEAD_DIM:sgu_w + (h + 1) * HEAD_DIM] = (
            on * hn_ref[:, cols] * gate[:, cols]).astype(BF16)
    y = jnp.dot(y_scr[...], wout_ref[...], preferred_element_type=F32)
    x1 = x_ref[0] + g1_ref[0] * y
    x1_ref[0] = x1
    h2 = _rms_modulated(x1, n2_ref[...], sc2_ref[0], sh2_ref[0])
    h2_ref[...] = h2.reshape(h2_ref.shape)
    n_exp = rb_ref.shape[0]
    h_hi = h2.astype(BF16)
    h_lo = (h2 - h_hi.astype(F32)).astype(BF16)
    part = _nt_dot(rw_ref[...], h_hi)
    lg = part[:n_exp] + part[n_exp:] + _nt_dot(rw_ref[:n_exp, :], h_lo) + rb_ref[...]
    for s in range(rows // TOKEN_TILE):
        lg_ref[s] = lg[:, s * TOKEN_TILE:(s + 1) * TOKEN_TILE]


def _postmix(proj, o_fwd, o_bwd, x, mod3, sgu_ln, sgu_w_bf16, sgu_b3, hgrn_norm, w_out_bf16,
             norm2, router_w, router_b):
    bsz, n, d = x.shape
    width = o_fwd.shape[2]
    rows = POSTMIX_ROWS
    n_exp = router_b.shape[0]
    nj = n // rows
    sub = rows // TOKEN_TILE

    def colgroup(col):
        return pl.BlockSpec((1, rows, width), lambda b, j: (b, j, col))

    def modrow(idx):
        return pl.BlockSpec((1, 1, d), lambda b, j: (b * 6 + idx, 0, 0))

    def whole(a):
        nd = a.ndim
        return pl.BlockSpec(a.shape, lambda b, j: (0,) * nd)

    return pl.pallas_call(
        _postmix_kernel,
        grid=(bsz, nj),
        in_specs=[colgroup(0), colgroup(1), colgroup(6), colgroup(0), colgroup(0),
                  pl.BlockSpec((1, rows, d), lambda b, j: (b, j, 0)),
                  modrow(2), modrow(3), modrow(4),
                  whole(sgu_ln), whole(sgu_w_bf16), whole(sgu_b3), whole(hgrn_norm),
                  whole(w_out_bf16), whole(norm2), whole(router_w), whole(router_b)],
        out_specs=[pl.BlockSpec((1, rows, d), lambda b, j: (b, j, 0)),
                   pl.BlockSpec((rows, 1, d), lambda b, j: (b * nj + j, 0, 0)),
                   pl.BlockSpec((sub, n_exp, TOKEN_TILE), lambda b, j: (b * nj + j, 0, 0))],
        out_shape=[jax.ShapeDtypeStruct((bsz, n, d), F32),
                   jax.ShapeDtypeStruct((bsz * n, 1, d), F32),
                   jax.ShapeDtypeStruct((bsz * n // TOKEN_TILE, n_exp, TOKEN_TILE), F32)],
        scratch_shapes=[pltpu.VMEM((rows, d), BF16)],
        compiler_params=_params(("arbitrary", "arbitrary")),
        name="postmix",
    )(proj, proj, proj, o_fwd, o_bwd, x, mod3, mod3, mod3, sgu_ln, sgu_w_bf16, sgu_b3, hgrn_norm,
      w_out_bf16, norm2, router_w, router_b)


def _route_kernel(lt_ref, upper_ref, ones_ref, ltri_ref,
                  dest_ref, gate_ref, pads_ref, pos_scr, eidx_scr, *, block_rows):
    n_tiles, n_exp, lanes = lt_ref.shape
    iota_e = lax.broadcasted_iota(jnp.int32, (n_exp, lanes), 0).astype(F32)
    group = ROUTE_GROUP
    wide_iota = jnp.concatenate([iota_e] * group, axis=1)

    def part(a, t):
        return a[:, t * lanes:(t + 1) * lanes]

    def tile_body(jg, carry):
        j0 = jg * group
        l = jnp.concatenate([lt_ref[j0 + t] for t in range(group)], axis=1)
        picks, vals = [], []
        for kk in range(TOP_K):
            m = jnp.max(l, axis=0, keepdims=True)
            idx = jnp.min(jnp.where(l == m, wide_iota, float(n_exp)), axis=0, keepdims=True)
            pick = wide_iota == idx
            picks.append(pick)
            vals.append(m)
            l = jnp.where(pick, -jnp.inf, l)
            for t in range(group):
                eidx_scr[j0 + t, kk:kk + 1, :] = part(idx, t)
        exps = [jnp.exp(v - vals[0]) for v in vals]
        tot = exps[0]
        for e in exps[1:]:
            tot = tot + e
        for kk in range(TOP_K):
            gate = exps[kk] / tot
            for t in range(group):
                gate_ref[j0 + t, kk:kk + 1, :] = part(gate, t)
        chosen = picks[0]
        for p in picks[1:]:
            chosen = chosen | p
        cb = jnp.where(chosen, 1.0, 0.0).astype(BF16)
        for t in range(group):
            cbt = part(cb, t)
            before = carry + jnp.dot(cbt, upper_ref[...], preferred_element_type=F32)
            for kk in range(TOP_K):
                pos_scr[j0 + t, kk:kk + 1, :] = jnp.sum(
                    jnp.where(part(picks[kk], t), before, 0.0), axis=0, keepdims=True)
            carry = carry + jnp.dot(cbt, ones_ref[...], preferred_element_type=F32)
        return carry

    counts = lax.fori_loop(0, n_tiles // group, tile_body, jnp.zeros((n_exp, lanes), F32))
    n_blk = jnp.ceil(counts * (1.0 / block_rows))
    end_blk = jnp.dot(ltri_ref[...], n_blk.astype(BF16), preferred_element_type=F32)
    start_row = (end_blk - n_blk) * block_rows

    def dest_body(j, _):
        for kk in range(TOP_K):
            idx = eidx_scr[j, kk:kk + 1, :]
            base = jnp.sum(jnp.where(iota_e == idx, start_row, 0.0), axis=0, keepdims=True)
            dest_ref[j, kk:kk + 1, :] = (base + pos_scr[j, kk:kk + 1, :]).astype(jnp.int32)
        return 0

    lax.fori_loop(0, n_tiles, dest_body, 0)

    pads_ref[0] = (start_row + counts).astype(jnp.int32)
    pads_ref[1] = (n_blk * block_rows - counts).astype(jnp.int32)
    pads_ref[2] = end_blk.astype(jnp.int32)
    pads_ref[3] = start_row.astype(jnp.int32)
    pads_ref[4] = n_blk.astype(jnp.int32)


def _route(lt3):
    n_tiles, n_exp, lanes = lt3.shape
    upper = jnp.asarray(np.triu(np.ones((lanes, lanes), np.float32), 1), BF16)
    ones = jnp.ones((lanes, lanes), BF16)
    ltri = jnp.asarray(np.tril(np.ones((n_exp, n_exp), np.float32)), BF16)
    vm = pl.BlockSpec(memory_space=pltpu.VMEM)
    return pl.pallas_call(
        functools.partial(_route_kernel, block_rows=EXPERT_ROWS),
        in_specs=[vm, vm, vm, vm],
        out_specs=[vm, vm, vm],
        out_shape=[jax.ShapeDtypeStruct((n_tiles, TOP_K, lanes), jnp.int32),
                   jax.ShapeDtypeStruct((n_tiles, TOP_K, lanes), F32),
                   jax.ShapeDtypeStruct((5, n_exp, lanes), jnp.int32)],
        scratch_shapes=[pltpu.VMEM((n_tiles, TOP_K, lanes), F32),
                        pltpu.VMEM((n_tiles, TOP_K, lanes), F32)],
        compiler_params=pltpu.CompilerParams(vmem_limit_bytes=VMEM_LIMIT),
        name="route",
    )(lt3, upper, ones, ltri)


def _pad_sizes():
    return [1 << b for b in reversed(range(int(math.log2(EXPERT_ROWS))))]


def _dispatch_kernel(pads_ref, dest_ref, h2_hbm, xs_hbm, zero_scr, ring_scr, sem, load_sem, zsem):
    j = pl.program_id(0)
    n_steps = pl.num_programs(0)
    tile = dest_ref.shape[2]
    n_exp = pads_ref.shape[1]

    def pad_copies(e, wait):
        off = pads_ref[0, e]
        cnt = pads_ref[1, e]
        for sz in _pad_sizes():
            cp = pltpu.make_async_copy(zero_scr.at[pl.ds(0, sz)], xs_hbm.at[pl.ds(off, sz)], zsem)

            @pl.when((cnt & sz) != 0)
            def _():
                if wait:
                    cp.wait()
                else:
                    cp.start()
            off = off + (cnt & sz)

    half = zero_scr.shape[0]
    n_blocks = xs_hbm.shape[0] // (2 * half)
    n_used = pads_ref[2, n_exp - 1]

    def tail_copies(blk, wait):
        for part in range(2):
            cp = pltpu.make_async_copy(
                zero_scr, xs_hbm.at[pl.ds((blk * 2 + part) * half, half)], zsem)
            if wait:
                cp.wait()
            else:
                cp.start()

    @pl.when(j == 0)
    def _():
        zero_scr[...] = jnp.zeros_like(zero_scr)

        def start(e, _):
            pad_copies(e, False)
            return 0
        lax.fori_loop(0, n_exp, start, 0)

        def start_tail(blk, _):
            tail_copies(blk, False)
            return 0
        lax.fori_loop(n_used, n_blocks, start_tail, 0)

    ring = ring_scr.shape[0] // tile

    def load(step):
        slot = step % ring
        return pltpu.make_async_copy(h2_hbm.at[pl.ds(step * tile, tile)],
                                     ring_scr.at[pl.ds(slot * tile, tile)], load_sem.at[slot])

    def finish_rows(step):
        for kk in range(TOP_K):
            pltpu.make_async_copy(ring_scr.at[pl.ds(0, tile)], xs_hbm.at[pl.ds(0, tile)],
                                  sem.at[step % 2]).wait()

    @pl.when(j == 0)
    def _():
        load(0).start()
        load(1).start()

    load(j).wait()
    base = (j % ring) * tile
    for i in range(tile):
        for kk in range(TOP_K):
            pltpu.make_async_copy(ring_scr.at[base + i], xs_hbm.at[dest_ref[0, kk, i]],
                                  sem.at[j % 2]).start(priority=kk % 2)

    @pl.when(j > 0)
    def _():
        finish_rows(j - 1)

    @pl.when(j + 2 < n_steps)
    def _():
        load(j + 2).start()

    @pl.when(j == n_steps - 1)
    def _():
        finish_rows(j)

        def finish(e, _):
            pad_copies(e, True)
            return 0
        lax.fori_loop(0, n_exp, finish, 0)

        def finish_tail(blk, _):
            tail_copies(blk, True)
            return 0
        lax.fori_loop(n_used, n_blocks, finish_tail, 0)


def _dispatch(pads, dest, h2_rows, n_rows):
    n_tok, _, d = h2_rows.shape
    n_tiles = dest.shape[0]
    assert n_tiles >= 2
    grid_spec = pltpu.PrefetchScalarGridSpec(
        num_scalar_prefetch=1,
        grid=(n_tiles,),
        in_specs=[pl.BlockSpec((1, TOP_K, TOKEN_TILE), lambda j, p: (j, 0, 0),
                               memory_space=pltpu.SMEM),
                  pl.BlockSpec(memory_space=pl.ANY)],
        out_specs=pl.BlockSpec(memory_space=pl.ANY),
        scratch_shapes=[pltpu.VMEM((EXPERT_ROWS // 2, 1, d), F32),
                        pltpu.VMEM((DISPATCH_RING * TOKEN_TILE, 1, d), F32),
                        pltpu.SemaphoreType.DMA((2,)),
                        pltpu.SemaphoreType.DMA((DISPATCH_RING,)),
                        pltpu.SemaphoreType.DMA(())],
    )
    return pl.pallas_call(
        _dispatch_kernel,
        grid_spec=grid_spec,
        out_shape=jax.ShapeDtypeStruct((n_rows, 1, d), F32),
        compiler_params=_params(("arbitrary",)),
        name="dispatch",
    )(pads, dest, h2_rows)


def _experts_kernel(start_ref, nblk_ref, xs_hbm, w1_ref, b1_ref, w2_ref, b2_ref, ys_hbm,
                    w1_scr, w2_scr, x_ring, y_ring, x_scr, out_scr, act_scr, load_sem, store_sem):
    e = pl.program_id(0)
    n_exp = pl.num_programs(0)
    rows = x_scr.shape[0]
    shift = int(math.log2(rows))
    n_blk = nblk_ref[e]
    first = lax.shift_right_logical(start_ref[e], shift)
    n_used = lax.shift_right_logical(start_ref[n_exp - 1], shift) + nblk_ref[n_exp - 1]
    d, two_f = w1_scr.shape
    f = two_f // 2

    def slot_rows(g):
        return pl.ds(pl.multiple_of(jnp.bitwise_and(g, 1) * rows, rows), rows)

    def load(g):
        return pltpu.make_async_copy(xs_hbm.at[pl.ds(g * rows, rows)], x_ring.at[slot_rows(g)],
                                     load_sem.at[jnp.bitwise_and(g, 1)])

    def store(g):
        return pltpu.make_async_copy(y_ring.at[slot_rows(g)], ys_hbm.at[pl.ds(g * rows, rows)],
                                     store_sem.at[jnp.bitwise_and(g, 1)])

    def finish(p):
        y_ring[slot_rows(p)] = out_scr[...].reshape((rows, 1, d))
        store(p).start()

    @pl.when((e == 0) & (n_used > 0))
    def _():
        load(0).start()
        out_scr[...] = jnp.zeros_like(out_scr)

    @pl.when(n_blk > 0)
    def _():
        step = 64

        def cast1(r, _):
            rs = pl.ds(pl.multiple_of(r * step, step), step)
            w1_scr[rs, :] = w1_ref[0, rs, :].astype(BF16)
            return 0
        lax.fori_loop(0, d // step, cast1, 0)

        def cast2(r, _):
            rs = pl.ds(pl.multiple_of(r * step, step), step)
            w2_scr[rs, :] = w2_ref[0, rs, :].astype(BF16)
            return 0
        lax.fori_loop(0, f // step, cast2, 0)

    def block(i, _):
        g = first + i
        load(g).wait()

        @pl.when(g + 1 < n_used)
        def _():
            load(g + 1).start()

        prev = jnp.maximum(g - 1, 0)

        @pl.when((g == 1) | (g >= 3))
        def _():
            store(prev).wait()

        x_scr[...] = x_ring[slot_rows(g)].reshape(x_scr.shape)
        xb = x_scr[...].astype(BF16)
        width = 512
        for n in range(f // width):
            if n == 1:
                finish(prev)
            cg = slice(n * width, (n + 1) * width)
            cu = slice(f + n * width, f + (n + 1) * width)
            gate = jnp.dot(xb, w1_scr[:, cg], preferred_element_type=F32) + b1_ref[0, :, cg]
            up = jnp.dot(xb, w1_scr[:, cu], preferred_element_type=F32) + b1_ref[0, :, cu]
            gate = jnp.minimum(gate, SWIGLU_LIMIT)
            up = jnp.clip(up, -SWIGLU_LIMIT, SWIGLU_LIMIT)
            act_scr[:, cg] = (gate * _sigmoid(SWIGLU_ALPHA * gate) * (up + 1.0)).astype(BF16)
        out_scr[...] = jnp.dot(act_scr[...], w2_scr[...], preferred_element_type=F32) + b2_ref[0]
        return 0
    lax.fori_loop(0, n_blk, block, 0)

    @pl.when((e == n_exp - 1) & (n_used > 0))
    def _():
        last = n_used - 1

        @pl.when(last != 1)
        def _():
            store(last).wait()

        finish(last)
        store(last).wait()

        @pl.when(last >= 1)
        def _():
            store(last - 1).wait()


def _experts(start_row, n_blk, xs, w1, b1, w2, b2):
    n_rows, _, d = xs.shape
    n_exp, _, two_f = w1.shape
    f = two_f // 2
    rows = EXPERT_ROWS
    grid_spec = pltpu.PrefetchScalarGridSpec(
        num_scalar_prefetch=2,
        grid=(n_exp,),
        in_specs=[pl.BlockSpec(memory_space=pl.ANY),
                  pl.BlockSpec((1, d, two_f), lambda e, st, nb: (e, 0, 0)),
                  pl.BlockSpec((1, 1, two_f), lambda e, st, nb: (e, 0, 0)),
                  pl.BlockSpec((1, f, d), lambda e, st, nb: (e, 0, 0)),
                  pl.BlockSpec((1, 1, d), lambda e, st, nb: (e, 0, 0))],
        out_specs=pl.BlockSpec(memory_space=pl.ANY),
        scratch_shapes=[pltpu.VMEM((d, two_f), BF16),
                        pltpu.VMEM((f, d), BF16)]
        + [pltpu.VMEM((2 * rows, 1, d), F32) for _ in range(2)]
        + [pltpu.VMEM((rows, d), F32),
           pltpu.VMEM((rows, d), F32),
           pltpu.VMEM((rows, f), BF16),
           pltpu.SemaphoreType.DMA((2,)),
           pltpu.SemaphoreType.DMA((2,))],
    )
    return pl.pallas_call(
        _experts_kernel,
        grid_spec=grid_spec,
        out_shape=jax.ShapeDtypeStruct((n_rows, 1, d), F32),
        input_output_aliases={2: 0},
        compiler_params=_params(("arbitrary",)),
        name="experts",
    )(start_row, n_blk, xs, w1, b1.reshape(n_exp, 1, two_f), w2, b2.reshape(n_exp, 1, d))


def _combine_kernel(dest_ref, ahead1_ref, ahead2_ref, ys_hbm, gate_ref, x1_ref, g2_ref, fn_ref,
                    o_ref, ring_scr, flat_scr, sem):
    j = pl.program_id(0)
    n_steps = pl.num_programs(0)
    tile = dest_ref.shape[2]
    ring = ring_scr.shape[0] // (TOP_K * tile)

    def slot_of(step):
        return step % ring

    def issue(idx_ref, slot, i):
        for kk in range(TOP_K):
            pltpu.make_async_copy(ys_hbm.at[idx_ref[0, kk, i]],
                                  ring_scr.at[(slot * TOP_K + kk) * tile + i],
                                  sem.at[slot]).start(priority=kk % 2)

    def drain(slot):
        for kk in range(TOP_K):
            pltpu.make_async_copy(ys_hbm.at[pl.ds(0, tile)], ring_scr.at[pl.ds(0, tile)],
                                  sem.at[slot]).wait()

    @pl.when(j == 0)
    def _():
        for idx_ref, slot in ((dest_ref, 0), (ahead1_ref, 1)):
            def body(i, _, idx_ref=idx_ref, slot=slot):
                issue(idx_ref, slot, i)
                return 0
            lax.fori_loop(0, tile, body, 0)

    slot = slot_of(j)
    drain(slot)
    ahead_slot = slot_of(j + 2)
    for i in range(tile):
        issue(ahead2_ref, ahead_slot, i)
    y = None
    for kk in range(TOP_K):
        rows = pl.ds(pl.multiple_of((slot * TOP_K + kk) * tile, tile), tile)
        flat_scr[...] = ring_scr[rows].reshape(flat_scr.shape)
        term = gate_ref[:, kk:kk + 1] * flat_scr[...]
        y = term if y is None else y + term
    x2 = x1_ref[...] + g2_ref[0] * y
    o_ref[...] = _rms(x2, fn_ref[...])

    @pl.when(j == n_steps - 1)
    def _():
        drain(slot_of(j + 1))
        drain(slot_of(j + 2))


def _combine(dest, ys, gates_tok, x1, mod3, final_norm, seq):
    n_tok, d = x1.shape
    n_tiles = dest.shape[0]
    tiles_per_sample = seq // TOKEN_TILE
    return pl.pallas_call(
        _combine_kernel,
        grid=(n_tiles,),
        in_specs=[pl.BlockSpec((1, TOP_K, TOKEN_TILE), lambda j: (j, 0, 0),
                               memory_space=pltpu.SMEM),
                  pl.BlockSpec((1, TOP_K, TOKEN_TILE),
                               lambda j: (jnp.minimum(j + 1, n_tiles - 1), 0, 0),
                               memory_space=pltpu.SMEM),
                  pl.BlockSpec((1, TOP_K, TOKEN_TILE),
                               lambda j: (jnp.minimum(j + 2, n_tiles - 1), 0, 0),
                               memory_space=pltpu.SMEM),
                  pl.BlockSpec(memory_space=pl.ANY),
                  pl.BlockSpec((TOKEN_TILE, TOP_K), lambda j: (j, 0)),
                  pl.BlockSpec((TOKEN_TILE, d), lambda j: (j, 0)),
                  pl.BlockSpec((1, 1, d), lambda j: ((j // tiles_per_sample) * 6 + 5, 0, 0)),
                  pl.BlockSpec((1, d), lambda j: (0, 0))],
        out_specs=pl.BlockSpec((TOKEN_TILE, d), lambda j: (j, 0)),
        out_shape=jax.ShapeDtypeStruct((n_tok, d), F32),
        scratch_shapes=[pltpu.VMEM((COMBINE_RING * TOP_K * TOKEN_TILE, 1, d), F32),
                        pltpu.VMEM((TOKEN_TILE, d), F32),
                        pltpu.SemaphoreType.DMA((COMBINE_RING,))],
        compiler_params=_params(("arbitrary",)),
        name="combine",
    )(dest, dest, dest, ys, gates_tok, x1, mod3, final_norm)


def kernel(x, c, ctx, c_ctx, w_mod, b_mod, norm1, w_in, sgu_ln, sgu_w, sgu_b, lb_fwd, lb_bwd,
           hgrn_norm, w_out, norm2, router_w, router_b, w1, b1, w2, b2, final_norm):
    depth = w_mod.shape[0]
    assert depth == 1, "single-layer block only"
    bsz, seq, d = x.shape
    assert bsz + 1 <= MOD_ROWS
    width = w_in.shape[2] // N_PROJ
    n_exp = router_w.shape[2]
    n_tok = bsz * seq

    cc = jnp.zeros((MOD_ROWS, d), F32).at[:bsz].set(c).at[bsz].set(c_ctx)
    mod = _modulation(cc, w_mod[0], b_mod[0])
    mod3 = mod.reshape(MOD_ROWS * 6, 1, d)

    w_in_b = w_in[0].astype(BF16)
    norm1_l = norm1[0].reshape(1, d)
    proj_ctx = _inproj(ctx, mod3, lambda b: bsz, norm1_l, w_in_b[:, 2 * width:6 * width],
                       ctx.shape[1], "inproj_ctx")
    states = _scan(proj_ctx, 0, 1, 2, 3, lb_fwd, lb_bwd, None, False, "scan_ctx")
    proj = _inproj(x, mod3, lambda b: b, norm1_l, w_in_b, INPROJ_ROWS, "inproj")
    o_fwd, o_bwd = _scan(proj, 2, 3, 4, 5, lb_fwd, lb_bwd, states, True, "scan")

    rw_t = router_w[0].T
    rw_hi = rw_t.astype(BF16)
    rw_parts = jnp.concatenate([rw_hi, (rw_t - rw_hi.astype(F32)).astype(BF16)], axis=0)
    x1, h2_rows, lt3 = _postmix(
        proj, o_fwd, o_bwd, x, mod3, sgu_ln[0], sgu_w[0].astype(BF16),
        sgu_b[0].reshape(sgu_b.shape[1], sgu_b.shape[2], 1), hgrn_norm[0].reshape(1, -1),
        w_out[0].astype(BF16), norm2[0].reshape(1, d), rw_parts, router_b[0].reshape(n_exp, 1))

    n_blocks = -(-(n_tok * TOP_K + n_exp * (EXPERT_ROWS - 1)) // EXPERT_ROWS)
    dest, gates, pads = _route(lt3)
    tables = pads[:, :, 0]

    xs = _dispatch(tables[:3], dest, h2_rows, n_blocks * EXPERT_ROWS)
    ys = _experts(tables[3], tables[4], xs, w1[0], b1[0], w2[0], b2[0])
    gates_tok = gates.transpose(0, 2, 1).reshape(n_tok, TOP_K)
    out = _combine(dest, ys, gates_tok, x1.reshape(n_tok, d), mod3, final_norm.reshape(1, d), seq)
    return out.reshape(bsz, seq, d)
```

```python
import functools
import math

import jax
import jax.numpy as jnp
import numpy as np
from jax import lax
from jax.experimental import pallas as pl
from jax.experimental.pallas import tpu as pltpu

F32 = jnp.float32
BF16 = jnp.bfloat16
EPS = 1e-6
LOG2_E = 1.4426950408889634
SUBLANES = 8

HEAD_DIM = 128
SGU_CHUNK = 128
N_PROJ = 7
TOP_K = 4
SWIGLU_LIMIT = 7.0
SWIGLU_ALPHA = 1.702

SCAN_CHUNK = 64
SCAN_SAMPLES = 4
INPROJ_ROWS = 512
POSTMIX_ROWS = 512
EXPERT_ROWS = 512
TOKEN_TILE = 128
DISPATCH_RING = 3
COMBINE_RING = 3
ROUTE_GROUP = 4
MOD_ROWS = 16
VMEM_LIMIT = 56 * 1024 * 1024


def _params(sem, vmem=None):
    return pltpu.CompilerParams(dimension_semantics=sem, vmem_limit_bytes=vmem or VMEM_LIMIT)


def _sigmoid(x):
    return 1.0 / (1.0 + jnp.exp(-x))


def _mod_kernel(c_ref, w_ref, b_ref, o_ref):
    c = c_ref[...]
    s = c * _sigmoid(c)
    o_ref[...] = jnp.dot(s, w_ref[...], precision=lax.Precision.HIGHEST,
                         preferred_element_type=F32) + b_ref[...]


def _modulation(cc, w_mod, b_mod):
    d, n6 = w_mod.shape
    tn = 1024
    return pl.pallas_call(
        _mod_kernel,
        grid=(n6 // tn,),
        in_specs=[pl.BlockSpec((MOD_ROWS, d), lambda j: (0, 0)),
                  pl.BlockSpec((d, tn), lambda j: (0, j)),
                  pl.BlockSpec((1, tn), lambda j: (0, j))],
        out_specs=pl.BlockSpec((MOD_ROWS, tn), lambda j: (0, j)),
        out_shape=jax.ShapeDtypeStruct((MOD_ROWS, n6), F32),
        compiler_params=_params(("arbitrary",)),
        name="mod",
    )(cc, w_mod, b_mod.reshape(1, n6))


def _rms(x, g):
    return x * lax.rsqrt(jnp.mean(x * x, axis=-1, keepdims=True) + EPS) * g


def _rms_modulated(x, g, scale, shift):
    return _rms(x, g * (1.0 + scale)) + shift


def _gelu_tanh(x):
    a = -2.0 * LOG2_E * math.sqrt(2.0 / math.pi)
    return x / (1.0 + jnp.exp2(x * (a + (a * 0.044715) * (x * x))))


def _inproj_kernel(x_ref, sh_ref, sc_ref, g_ref, w_ref, o_ref, *, width):
    hb = _rms_modulated(x_ref[0], g_ref[...], sc_ref[0], sh_ref[0]).astype(BF16)
    for j in range(w_ref.shape[1] // width):
        cols = slice(j * width, (j + 1) * width)
        o_ref[0, :, cols] = jnp.dot(hb, w_ref[:, cols],
                                    preferred_element_type=F32).astype(o_ref.dtype)


def _inproj(x, mod3, mod_row, norm_g, w_bf16, rows, name):
    bsz, n, d = x.shape
    wcols = w_bf16.shape[1]
    return pl.pallas_call(
        functools.partial(_inproj_kernel, width=512),
        grid=(bsz, n // rows),
        in_specs=[pl.BlockSpec((1, rows, d), lambda b, j: (b, j, 0)),
                  pl.BlockSpec((1, 1, d), lambda b, j: (mod_row(b) * 6 + 0, 0, 0)),
                  pl.BlockSpec((1, 1, d), lambda b, j: (mod_row(b) * 6 + 1, 0, 0)),
                  pl.BlockSpec((1, d), lambda b, j: (0, 0)),
                  pl.BlockSpec((d, wcols), lambda b, j: (0, 0))],
        out_specs=pl.BlockSpec((1, rows, wcols), lambda b, j: (b, j, 0)),
        out_shape=jax.ShapeDtypeStruct((bsz, n, wcols), BF16),
        compiler_params=_params(("arbitrary", "arbitrary")),
        name=name,
    )(x, mod3, mod3, norm_g, w_bf16)


def _scan_constants(length):
    levels = int(math.log2(length))
    assert 1 << levels == length
    t = np.arange(length)
    sums = np.zeros((2, levels + 1, length, length), np.float32)
    masks = np.zeros((2, levels + 1, length, length), np.float32)
    for lv in range(levels):
        half = 1 << lv
        blk = t // (2 * half)
        mid = blk * 2 * half + half
        later = (t % (2 * half)) >= half
        for r in range(length):
            if later[r]:
                sums[0, lv, r, mid[r] + 1:r + 1] = 1.0
            else:
                sums[0, lv, r, r + 1:mid[r] + 1] = 1.0
        masks[0, lv] = (blk[:, None] == blk[None, :]) & later[:, None] & (~later)[None, :]
    sums[0, levels] = np.tril(np.ones((length, length), np.float32))
    masks[0, levels] = np.eye(length, dtype=np.float32)
    sums[1] = sums[0][:, ::-1, ::-1]
    masks[1] = masks[0][:, ::-1, ::-1]
    n_lv = levels + 1
    kept = [lv for lv in range(levels) if (1 << lv) < SUBLANES] + [levels]
    big = np.zeros((len(kept), 2, length, 2, 2, length), np.float32)
    pair = np.zeros((n_lv, 2, length, 2, length), np.float32)
    for d in range(2):
        big[:, d, :, 0, d, :] = sums[d][kept]
        big[:, d, :, 1, d, :] = sums[d][kept]
        pair[:, d, :, d, :] = masks[d]
    return (big.reshape(len(kept) * 2 * length, 4 * length),
            pair.reshape(n_lv, 2 * length, 2 * length), len(kept) - 1)


def _coarse_exponent(bp, half, length):
    parts = []
    for d in range(2):
        for start in range(0, length, 2 * half):
            lo = d * length + start
            mid = lo + half
            ref_row = mid if d == 0 else mid - 1
            ref = jnp.broadcast_to(bp[ref_row:ref_row + 1, :], (half, bp.shape[1]))
            first, second = bp[lo:mid, :], bp[mid:mid + half, :]
            parts += [ref - first, second - ref] if d == 0 else [first - ref, ref - second]
    return jnp.concatenate(parts, axis=0)


def _nt_dot(a, b):
    return lax.dot_general(a, b, (((1,), (1,)), ((), ())), preferred_element_type=F32)


def _lower_bound(tab):
    ex = jnp.exp(tab - jnp.max(tab, axis=0, keepdims=True))
    return ex[0:1] / jnp.sum(ex, axis=0, keepdims=True)


def _scan_kernel(*refs, heads, n_fine, emit_o, has_init):
    qf_ref, zf_ref, if_ref, qb_ref, zb_ref, ib_ref, tabf_ref, tabb_ref, sums_ref, masks_ref = refs[:10]
    pos = 10
    init_ref = None
    if has_init:
        init_ref = refs[pos]
        pos += 1
    n_out = 2 if emit_o else 1
    out_refs = refs[pos:pos + n_out]
    state_ref, d_scr = refs[pos + n_out:]

    c = pl.program_id(1)
    n_chunks = pl.num_programs(1)
    n_samples, length, width = qf_ref.shape
    levels = masks_ref.shape[0] - 1
    rows2 = 2 * length
    groups = n_samples * heads

    @pl.when(c == 0)
    def _():
        if has_init:
            state_ref[...] = init_ref[...].reshape(state_ref.shape)
        else:
            state_ref[...] = jnp.zeros_like(state_ref)

    def both(f_ref, b_ref):
        return jnp.concatenate(
            [jnp.concatenate([r[s] for s in range(n_samples)], axis=1) for r in (f_ref, b_ref)],
            axis=0)

    def per_dir(a, b):
        return jnp.concatenate([jnp.broadcast_to(a, (length, a.shape[1])),
                                jnp.broadcast_to(b, (length, b.shape[1]))], axis=0)

    def per_sample(a):
        return jnp.concatenate([a] * n_samples, axis=1)

    is_fwd = lax.broadcasted_iota(jnp.int32, (rows2, HEAD_DIM), 0) < length

    def split_dirs(a):
        return jnp.concatenate([jnp.where(is_fwd, a, 0.0), jnp.where(is_fwd, 0.0, a)], axis=1)

    lb = per_dir(per_sample(_lower_bound(tabf_ref[...])), per_sample(_lower_bound(tabb_ref[...])))
    z = both(zf_ref, zb_ref).astype(F32)
    q = both(qf_ref, qb_ref).astype(F32)
    v = both(if_ref, ib_ref)
    f = lb + (1.0 - lb) * _sigmoid(z)
    k = 1.0 - f
    g = jnp.log(f) * LOG2_E
    g_hi = g.astype(BF16)
    g_lo = (g - g_hi.astype(F32)).astype(BF16)
    g_parts = jnp.concatenate([g_hi, g_lo], axis=0)
    if emit_o:
        d_scr[...] = jnp.dot(sums_ref[...], g_parts, preferred_element_type=F32)
    else:
        d_scr[...] = jnp.dot(sums_ref[n_fine * rows2:(n_fine + 1) * rows2, :], g_parts,
                             preferred_element_type=F32)
    prefix_rows = slice(n_fine * rows2, (n_fine + 1) * rows2) if emit_o else slice(0, rows2)
    bp = d_scr[prefix_rows, :]
    btot_f = bp[length - 1:length]
    btot_b = bp[length:length + 1]

    if emit_o:
        qd = q * jnp.exp2(bp)
        o_carry = []
        for gi in range(groups):
            cols = slice(gi * HEAD_DIM, (gi + 1) * HEAD_DIM)
            o_carry.append(_nt_dot(split_dirs(qd[:, cols]).astype(BF16),
                                   state_ref[gi].astype(BF16)))
    kd = k * jnp.exp2(per_dir(btot_f, btot_b) - bp)
    dec_f = jnp.exp2(btot_f)
    dec_b = jnp.exp2(btot_b)
    for gi in range(groups):
        cols = slice(gi * HEAD_DIM, (gi + 1) * HEAD_DIM)
        dec = jnp.concatenate([dec_f[:, cols], dec_b[:, cols]], axis=1)
        vt = v[:, cols].astype(F32).T.astype(BF16)
        state_ref[gi] = state_ref[gi] * dec + jnp.dot(
            vt, split_dirs(kd[:, cols]).astype(BF16), preferred_element_type=F32)

    if emit_o:
        of_ref, ob_ref = out_refs
        scores = [None] * groups
        q_bf, k_bf = q.astype(BF16), k.astype(BF16)
        for lv in range(levels + 1):
            if lv == levels:
                xq, yk = q_bf, k_bf
            else:
                if lv < n_fine:
                    dl = d_scr[lv * rows2:(lv + 1) * rows2, :]
                else:
                    dl = _coarse_exponent(bp, 1 << lv, length)
                e = jnp.exp2(dl).astype(BF16)
                xq, yk = q_bf * e, k_bf * e
            keep = masks_ref[lv] > 0.5
            for gi in range(groups):
                cols = slice(gi * HEAD_DIM, (gi + 1) * HEAD_DIM)
                p = _nt_dot(xq[:, cols], yk[:, cols])
                scores[gi] = jnp.where(keep, p, 0.0 if scores[gi] is None else scores[gi])
        for gi in range(groups):
            cols = slice(gi * HEAD_DIM, (gi + 1) * HEAD_DIM)
            out_cols = slice((gi % heads) * HEAD_DIM, (gi % heads + 1) * HEAD_DIM)
            o = o_carry[gi] + jnp.dot(scores[gi].astype(BF16), v[:, cols],
                                      preferred_element_type=F32)
            of_ref[gi // heads, :, out_cols] = o[:length].astype(of_ref.dtype)
            ob_ref[gi // heads, :, out_cols] = o[length:].astype(ob_ref.dtype)

    if not emit_o:
        @pl.when(c == n_chunks - 1)
        def _():
            out_refs[0][...] = state_ref[...].reshape(out_refs[0].shape)


def _scan(proj, col_q, col_zf, col_zb, col_i, tab_f, tab_b, init, emit_o, name):
    bsz, n, _ = proj.shape
    width = tab_f.shape[1]
    heads = width // HEAD_DIM
    length = SCAN_CHUNK
    n_chunks = n // length
    sums_np, masks_np, n_fine = _scan_constants(length)
    sums = jnp.asarray(sums_np, BF16)
    masks = jnp.asarray(masks_np, F32)
    d_rows = sums.shape[0] if emit_o else 2 * length
    ns = SCAN_SAMPLES

    def fwd(col):
        return pl.BlockSpec((ns, length, width), lambda b, c: (b, c, col))

    def bwd(col):
        return pl.BlockSpec((ns, length, width), lambda b, c: (b, n_chunks - 1 - c, col))

    def whole(a):
        nd = a.ndim
        return pl.BlockSpec(a.shape, lambda b, c: (0,) * nd)

    in_specs = [fwd(col_q), fwd(col_zf), fwd(col_i), bwd(col_q), bwd(col_zb), bwd(col_i),
                whole(tab_f), whole(tab_b), whole(sums), whole(masks)]
    args = [proj, proj, proj, proj, proj, proj, tab_f, tab_b, sums, masks]
    state_shape = (heads, HEAD_DIM, 2 * HEAD_DIM)
    if init is not None:
        in_specs.append(pl.BlockSpec((ns,) + state_shape, lambda b, c: (b, 0, 0, 0)))
        args.append(init)
    if emit_o:
        out_spec = [pl.BlockSpec((ns, length, width), lambda b, c: (b, c, 0)),
                    pl.BlockSpec((ns, length, width), lambda b, c: (b, n_chunks - 1 - c, 0))]
        out_shape = [jax.ShapeDtypeStruct((bsz, n, width), BF16)] * 2
    else:
        out_spec = pl.BlockSpec((ns,) + state_shape, lambda b, c: (b, 0, 0, 0))
        out_shape = jax.ShapeDtypeStruct((bsz,) + state_shape, F32)
    return pl.pallas_call(
        functools.partial(_scan_kernel, heads=heads, n_fine=n_fine, emit_o=emit_o,
                          has_init=init is not None),
        grid=(bsz // ns, n_chunks),
        in_specs=in_specs,
        out_specs=out_spec,
        out_shape=out_shape,
        scratch_shapes=[pltpu.VMEM((ns * heads,) + state_shape[1:], F32),
                        pltpu.VMEM((d_rows, ns * width), F32)],
        compiler_params=_params(("arbitrary", "arbitrary")),
        name=name,
    )(*args)


def _postmix_kernel(u_ref, v_ref, g_ref, of_ref, ob_ref, x_ref, g1_ref, sh2_ref, sc2_ref,
                    ln_ref, ws_ref, bs_ref, hn_ref, wout_ref, n2_ref, rw_ref, rb_ref,
                    x1_ref, h2_ref, lg_ref, y_scr):
    rows = x_ref.shape[1]
    heads = ln_ref.shape[0]
    sgu_w = heads * HEAD_DIM
    u = _gelu_tanh(u_ref[0].astype(F32))
    v = _gelu_tanh(v_ref[0].astype(F32))
    for ch in range(rows // SGU_CHUNK):
        rs = slice(ch * SGU_CHUNK, (ch + 1) * SGU_CHUNK)
        for h in range(heads):
            cols = slice(h * HEAD_DIM, (h + 1) * HEAD_DIM)
            vh = v[rs, cols]
            mu = jnp.mean(vh, axis=-1, keepdims=True)
            var = jnp.mean(jnp.square(vh - mu), axis=-1, keepdims=True)
            vn = (vh - mu) * lax.rsqrt(var + EPS) * ln_ref[h:h + 1, :]
            z = jnp.dot(ws_ref[h], vn.astype(BF16), preferred_element_type=F32) + bs_ref[h]
            y_scr[rs, cols] = (u[rs, cols] * z).astype(BF16)
    o = of_ref[0].astype(F32) + ob_ref[0].astype(F32)
    gate = g_ref[0].astype(F32)
    gate = gate * _sigmoid(gate)
    for h in range(o.shape[1] // HEAD_DIM):
        cols = slice(h * HEAD_DIM, (h + 1) * HEAD_DIM)
        oh = o[:, cols]
        on = oh * lax.rsqrt(jnp.mean(oh * oh, axis=-1, keepdims=True) + EPS)
        y_scr[:, sgu_w + h * HEAD_DIM:sgu_w + (h + 1) * HEAD_DIM] = (
            on * hn_ref[:, cols] * gate[:, cols]).astype(BF16)
    y = jnp.dot(y_scr[...], wout_ref[...], preferred_element_type=F32)
    x1 = x_ref[0] + g1_ref[0] * y
    x1_ref[0] = x1
    h2 = _rms_modulated(x1, n2_ref[...], sc2_ref[0], sh2_ref[0])
    h2_ref[...] = h2.reshape(h2_ref.shape)
    n_exp = rb_ref.shape[0]
    h_hi = h2.astype(BF16)
    h_lo = (h2 - h_hi.astype(F32)).astype(BF16)
    part = _nt_dot(rw_ref[...], h_hi)
    lg = part[:n_exp] + part[n_exp:] + _nt_dot(rw_ref[:n_exp, :], h_lo) + rb_ref[...]
    for s in range(rows // TOKEN_TILE):
        lg_ref[s] = lg[:, s * TOKEN_TILE:(s + 1) * TOKEN_TILE]


def _postmix(proj, o_fwd, o_bwd, x, mod3, sgu_ln, sgu_w_bf16, sgu_b3, hgrn_norm, w_out_bf16,
             norm2, router_w, router_b):
    bsz, n, d = x.shape
    width = o_fwd.shape[2]
    rows = POSTMIX_ROWS
    n_exp = router_b.shape[0]
    nj = n // rows
    sub = rows // TOKEN_TILE

    def colgroup(col):
        return pl.BlockSpec((1, rows, width), lambda b, j: (b, j, col))

    def modrow(idx):
        return pl.BlockSpec((1, 1, d), lambda b, j: (b * 6 + idx, 0, 0))

    def whole(a):
        nd = a.ndim
        return pl.BlockSpec(a.shape, lambda b, j: (0,) * nd)

    return pl.pallas_call(
        _postmix_kernel,
        grid=(bsz, nj),
        in_specs=[colgroup(0), colgroup(1), colgroup(6), colgroup(0), colgroup(0),
                  pl.BlockSpec((1, rows, d), lambda b, j: (b, j, 0)),
                  modrow(2), modrow(3), modrow(4),
                  whole(sgu_ln), whole(sgu_w_bf16), whole(sgu_b3), whole(hgrn_norm),
                  whole(w_out_bf16), whole(norm2), whole(router_w), whole(router_b)],
        out_specs=[pl.BlockSpec((1, rows, d), lambda b, j: (b, j, 0)),
                   pl.BlockSpec((rows, 1, d), lambda b, j: (b * nj + j, 0, 0)),
                   pl.BlockSpec((sub, n_exp, TOKEN_TILE), lambda b, j: (b * nj + j, 0, 0))],
        out_shape=[jax.ShapeDtypeStruct((bsz, n, d), F32),
                   jax.ShapeDtypeStruct((bsz * n, 1, d), F32),
                   jax.ShapeDtypeStruct((bsz * n // TOKEN_TILE, n_exp, TOKEN_TILE), F32)],
        scratch_shapes=[pltpu.VMEM((rows, d), BF16)],
        compiler_params=_params(("arbitrary", "arbitrary")),
        name="postmix",
    )(proj, proj, proj, o_fwd, o_bwd, x, mod3, mod3, mod3, sgu_ln, sgu_w_bf16, sgu_b3, hgrn_norm,
      w_out_bf16, norm2, router_w, router_b)


def _route_kernel(lt_ref, upper_ref, ones_ref, ltri_ref,
                  dest_ref, gate_ref, pads_ref, pos_scr, eidx_scr, *, block_rows):
    n_tiles, n_exp, lanes = lt_ref.shape
    iota_e = lax.broadcasted_iota(jnp.int32, (n_exp, lanes), 0).astype(F32)
    group = ROUTE_GROUP
    wide_iota = jnp.concatenate([iota_e] * group, axis=1)

    def part(a, t):
        return a[:, t * lanes:(t + 1) * lanes]

    def tile_body(jg, carry):
        j0 = jg * group
        l = jnp.concatenate([lt_ref[j0 + t] for t in range(group)], axis=1)
        picks, vals = [], []
        for kk in range(TOP_K):
            m = jnp.max(l, axis=0, keepdims=True)
            idx = jnp.min(jnp.where(l == m, wide_iota, float(n_exp)), axis=0, keepdims=True)
            pick = wide_iota == idx
            picks.append(pick)
            vals.append(m)
            l = jnp.where(pick, -jnp.inf, l)
            for t in range(group):
                eidx_scr[j0 + t, kk:kk + 1, :] = part(idx, t)
        exps = [jnp.exp(v - vals[0]) for v in vals]
        tot = exps[0]
        for e in exps[1:]:
            tot = tot + e
        for kk in range(TOP_K):
            gate = exps[kk] / tot
            for t in range(group):
                gate_ref[j0 + t, kk:kk + 1, :] = part(gate, t)
        chosen = picks[0]
        for p in picks[1:]:
            chosen = chosen | p
        cb = jnp.where(chosen, 1.0, 0.0).astype(BF16)
        for t in range(group):
            cbt = part(cb, t)
            before = carry + jnp.dot(cbt, upper_ref[...], preferred_element_type=F32)
            for kk in range(TOP_K):
                pos_scr[j0 + t, kk:kk + 1, :] = jnp.sum(
                    jnp.where(part(picks[kk], t), before, 0.0), axis=0, keepdims=True)
            carry = carry + jnp.dot(cbt, ones_ref[...], preferred_element_type=F32)
        return carry

    counts = lax.fori_loop(0, n_tiles // group, tile_body, jnp.zeros((n_exp, lanes), F32))
    n_blk = jnp.ceil(counts * (1.0 / block_rows))
    end_blk = jnp.dot(ltri_ref[...], n_blk.astype(BF16), preferred_element_type=F32)
    start_row = (end_blk - n_blk) * block_rows

    def dest_body(j, _):
        for kk in range(TOP_K):
            idx = eidx_scr[j, kk:kk + 1, :]
            base = jnp.sum(jnp.where(iota_e == idx, start_row, 0.0), axis=0, keepdims=True)
            dest_ref[j, kk:kk + 1, :] = (base + pos_scr[j, kk:kk + 1, :]).astype(jnp.int32)
        return 0

    lax.fori_loop(0, n_tiles, dest_body, 0)

    pads_ref[0] = (start_row + counts).astype(jnp.int32)
    pads_ref[1] = (n_blk * block_rows - counts).astype(jnp.int32)
    pads_ref[2] = end_blk.astype(jnp.int32)
    pads_ref[3] = start_row.astype(jnp.int32)
    pads_ref[4] = n_blk.astype(jnp.int32)


def _route(lt3):
    n_tiles, n_exp, lanes = lt3.shape
    upper = jnp.asarray(np.triu(np.ones((lanes, lanes), np.float32), 1), BF16)
    ones = jnp.ones((lanes, lanes), BF16)
    ltri = jnp.asarray(np.tril(np.ones((n_exp, n_exp), np.float32)), BF16)
    vm = pl.BlockSpec(memory_space=pltpu.VMEM)
    return pl.pallas_call(
        functools.partial(_route_kernel, block_rows=EXPERT_ROWS),
        in_specs=[vm, vm, vm, vm],
        out_specs=[vm, vm, vm],
        out_shape=[jax.ShapeDtypeStruct((n_tiles, TOP_K, lanes), jnp.int32),
                   jax.ShapeDtypeStruct((n_tiles, TOP_K, lanes), F32),
                   jax.ShapeDtypeStruct((5, n_exp, lanes), jnp.int32)],
        scratch_shapes=[pltpu.VMEM((n_tiles, TOP_K, lanes), F32),
                        pltpu.VMEM((n_tiles, TOP_K, lanes), F32)],
        compiler_params=pltpu.CompilerParams(vmem_limit_bytes=VMEM_LIMIT),
        name="route",
    )(lt3, upper, ones, ltri)


def _pad_sizes():
    return [1 << b for b in reversed(range(int(math.log2(EXPERT_ROWS))))]


def _dispatch_kernel(pads_ref, dest_ref, h2_hbm, xs_hbm, zero_scr, ring_scr, sem, load_sem, zsem):
    j = pl.program_id(0)
    n_steps = pl.num_programs(0)
    tile = dest_ref.shape[2]
    n_exp = pads_ref.shape[1]

    def pad_copies(e, wait):
        off = pads_ref[0, e]
        cnt = pads_ref[1, e]
        for sz in _pad_sizes():
            cp = pltpu.make_async_copy(zero_scr.at[pl.ds(0, sz)], xs_hbm.at[pl.ds(off, sz)], zsem)

            @pl.when((cnt & sz) != 0)
            def _():
                if wait:
                    cp.wait()
                else:
                    cp.start()
            off = off + (cnt & sz)

    half = zero_scr.shape[0]
    n_blocks = xs_hbm.shape[0] // (2 * half)
    n_used = pads_ref[2, n_exp - 1]

    def tail_copies(blk, wait):
        for part in range(2):
            cp = pltpu.make_async_copy(
                zero_scr, xs_hbm.at[pl.ds((blk * 2 + part) * half, half)], zsem)
            if wait:
                cp.wait()
            else:
                cp.start()

    @pl.when(j == 0)
    def _():
        zero_scr[...] = jnp.zeros_like(zero_scr)

        def start(e, _):
            pad_copies(e, False)
            return 0
        lax.fori_loop(0, n_exp, start, 0)

        def start_tail(blk, _):
            tail_copies(blk, False)
            return 0
        lax.fori_loop(n_used, n_blocks, start_tail, 0)

    ring = ring_scr.shape[0] // tile

    def load(step):
        slot = step % ring
        return pltpu.make_async_copy(h2_hbm.at[pl.ds(step * tile, tile)],
                                     ring_scr.at[pl.ds(slot * tile, tile)], load_sem.at[slot])

    def finish_rows(step):
        for kk in range(TOP_K):
            pltpu.make_async_copy(ring_scr.at[pl.ds(0, tile)], xs_hbm.at[pl.ds(0, tile)],
                                  sem.at[step % 2]).wait()

    @pl.when(j == 0)
    def _():
        load(0).start()
        load(1).start()

    load(j).wait()
    base = (j % ring) * tile
    for i in range(tile):
        for kk in range(TOP_K):
            pltpu.make_async_copy(ring_scr.at[base + i], xs_hbm.at[dest_ref[0, kk, i]],
                                  sem.at[j % 2]).start(priority=kk % 2)

    @pl.when(j > 0)
    def _():
        finish_rows(j - 1)

    @pl.when(j + 2 < n_steps)
    def _():
        load(j + 2).start()

    @pl.when(j == n_steps - 1)
    def _():
        finish_rows(j)

        def finish(e, _):
            pad_copies(e, True)
            return 0
        lax.fori_loop(0, n_exp, finish, 0)

        def finish_tail(blk, _):
            tail_copies(blk, True)
            return 0
        lax.fori_loop(n_used, n_blocks, finish_tail, 0)


def _dispatch(pads, dest, h2_rows, n_rows):
    n_tok, _, d = h2_rows.shape
    n_tiles = dest.shape[0]
    assert n_tiles >= 2
    grid_spec = pltpu.PrefetchScalarGridSpec(
        num_scalar_prefetch=1,
        grid=(n_tiles,),
        in_specs=[pl.BlockSpec((1, TOP_K, TOKEN_TILE), lambda j, p: (j, 0, 0),
                               memory_space=pltpu.SMEM),
                  pl.BlockSpec(memory_space=pl.ANY)],
        out_specs=pl.BlockSpec(memory_space=pl.ANY),
        scratch_shapes=[pltpu.VMEM((EXPERT_ROWS // 2, 1, d), F32),
                        pltpu.VMEM((DISPATCH_RING * TOKEN_TILE, 1, d), F32),
                        pltpu.SemaphoreType.DMA((2,)),
                        pltpu.SemaphoreType.DMA((DISPATCH_RING,)),
                        pltpu.SemaphoreType.DMA(())],
    )
    return pl.pallas_call(
        _dispatch_kernel,
        grid_spec=grid_spec,
        out_shape=jax.ShapeDtypeStruct((n_rows, 1, d), F32),
        compiler_params=_params(("arbitrary",)),
        name="dispatch",
    )(pads, dest, h2_rows)


def _experts_kernel(start_ref, nblk_ref, pad_ref, xs_hbm, w1_ref, b1_ref, w2_ref, b2_ref, ys_hbm,
                    w1_scr, w2_scr, x_ring, y_ring, x_scr, out_scr, act_scr, load_sem, store_sem):
    e = pl.program_id(0)
    n_exp = pl.num_programs(0)
    rows = x_scr.shape[0]
    shift = int(math.log2(rows))
    n_blk = nblk_ref[e]
    first = lax.shift_right_logical(start_ref[e], shift)
    n_used = lax.shift_right_logical(start_ref[n_exp - 1], shift) + nblk_ref[n_exp - 1]
    d, two_f = w1_scr.shape
    f = two_f // 2

    def slot_rows(g):
        return pl.ds(pl.multiple_of(jnp.bitwise_and(g, 1) * rows, rows), rows)

    def load(g):
        return pltpu.make_async_copy(xs_hbm.at[pl.ds(g * rows, rows)], x_ring.at[slot_rows(g)],
                                     load_sem.at[jnp.bitwise_and(g, 1)])

    def store(g):
        return pltpu.make_async_copy(y_ring.at[slot_rows(g)], ys_hbm.at[pl.ds(g * rows, rows)],
                                     store_sem.at[jnp.bitwise_and(g, 1)])

    def finish(p):
        y_ring[slot_rows(p)] = out_scr[...].reshape((rows, 1, d))
        store(p).start()

    @pl.when((e == 0) & (n_used > 0))
    def _():
        load(0).start()
        out_scr[...] = jnp.zeros_like(out_scr)

    @pl.when(n_blk > 0)
    def _():
        step = 64

        def cast1(r, _):
            rs = pl.ds(pl.multiple_of(r * step, step), step)
            w1_scr[rs, :] = w1_ref[0, rs, :].astype(BF16)
            return 0
        lax.fori_loop(0, d // step, cast1, 0)

        def cast2(r, _):
            rs = pl.ds(pl.multiple_of(r * step, step), step)
            w2_scr[rs, :] = w2_ref[0, rs, :].astype(BF16)
            return 0
        lax.fori_loop(0, f // step, cast2, 0)

    def block(i, _):
        g = first + i
        load(g).wait()

        @pl.when(g + 1 < n_used)
        def _():
            load(g + 1).start()

        prev = jnp.maximum(g - 1, 0)

        @pl.when((g == 1) | (g >= 3))
        def _():
            store(prev).wait()

        def mlp(m):
            base = pl.multiple_of(jnp.bitwise_and(g, 1) * rows, rows)
            x_scr[0:m, :] = x_ring[pl.ds(base, m)].reshape((m, d))
            xb = x_scr[0:m, :].astype(BF16)
            finish(prev)
            width = 512
            for n in range(f // width):
                cg = slice(n * width, (n + 1) * width)
                cu = slice(f + n * width, f + (n + 1) * width)
                gate = jnp.dot(xb, w1_scr[:, cg], preferred_element_type=F32) + b1_ref[0, :, cg]
                up = jnp.dot(xb, w1_scr[:, cu], preferred_element_type=F32) + b1_ref[0, :, cu]
                gate = jnp.minimum(gate, SWIGLU_LIMIT)
                up = jnp.clip(up, -SWIGLU_LIMIT, SWIGLU_LIMIT)
                act_scr[0:m, cg] = (gate * _sigmoid(SWIGLU_ALPHA * gate)
                                    * (up + 1.0)).astype(BF16)
            out_scr[0:m, :] = jnp.dot(act_scr[0:m, :], w2_scr[...],
                                      preferred_element_type=F32) + b2_ref[0]

        short = (i == n_blk - 1) & (pad_ref[e] >= rows // 2)

        @pl.when(short)
        def _():
            mlp(rows // 2)

        @pl.when(jnp.logical_not(short))
        def _():
            mlp(rows)
        return 0
    lax.fori_loop(0, n_blk, block, 0)

    @pl.when((e == n_exp - 1) & (n_used > 0))
    def _():
        last = n_used - 1

        @pl.when(last != 1)
        def _():
            store(last).wait()

        finish(last)
        store(last).wait()

        @pl.when(last >= 1)
        def _():
            store(last - 1).wait()


def _experts(start_row, n_blk, n_pad, xs, w1, b1, w2, b2):
    n_rows, _, d = xs.shape
    n_exp, _, two_f = w1.shape
    f = two_f // 2
    rows = EXPERT_ROWS
    grid_spec = pltpu.PrefetchScalarGridSpec(
        num_scalar_prefetch=3,
        grid=(n_exp,),
        in_specs=[pl.BlockSpec(memory_space=pl.ANY),
                  pl.BlockSpec((1, d, two_f), lambda e, st, nb, pd: (e, 0, 0)),
                  pl.BlockSpec((1, 1, two_f), lambda e, st, nb, pd: (e, 0, 0)),
                  pl.BlockSpec((1, f, d), lambda e, st, nb, pd: (e, 0, 0)),
                  pl.BlockSpec((1, 1, d), lambda e, st, nb, pd: (e, 0, 0))],
        out_specs=pl.BlockSpec(memory_space=pl.ANY),
        scratch_shapes=[pltpu.VMEM((d, two_f), BF16),
                        pltpu.VMEM((f, d), BF16)]
        + [pltpu.VMEM((2 * rows, 1, d), F32) for _ in range(2)]
        + [pltpu.VMEM((rows, d), F32),
           pltpu.VMEM((rows, d), F32),
           pltpu.VMEM((rows, f), BF16),
           pltpu.SemaphoreType.DMA((2,)),
           pltpu.SemaphoreType.DMA((2,))],
    )
    return pl.pallas_call(
        _experts_kernel,
        grid_spec=grid_spec,
        out_shape=jax.ShapeDtypeStruct((n_rows, 1, d), F32),
        input_output_aliases={3: 0},
        compiler_params=_params(("arbitrary",)),
        name="experts",
    )(start_row, n_blk, n_pad, xs, w1, b1.reshape(n_exp, 1, two_f), w2, b2.reshape(n_exp, 1, d))


def _combine_kernel(dest_ref, ahead1_ref, ahead2_ref, ys_hbm, gate_ref, x1_ref, g2_ref, fn_ref,
                    o_ref, ring_scr, flat_scr, sem):
    j = pl.program_id(0)
    n_steps = pl.num_programs(0)
    tile = dest_ref.shape[2]
    ring = ring_scr.shape[0] // (TOP_K * tile)

    def slot_of(step):
        return step % ring

    def issue(idx_ref, slot, i):
        for kk in range(TOP_K):
            pltpu.make_async_copy(ys_hbm.at[idx_ref[0, kk, i]],
                                  ring_scr.at[(slot * TOP_K + kk) * tile + i],
                                  sem.at[slot]).start(priority=kk % 2)

    def drain(slot):
        for kk in range(TOP_K):
            pltpu.make_async_copy(ys_hbm.at[pl.ds(0, tile)], ring_scr.at[pl.ds(0, tile)],
                                  sem.at[slot]).wait()

    @pl.when(j == 0)
    def _():
        for idx_ref, slot in ((dest_ref, 0), (ahead1_ref, 1)):
            def body(i, _, idx_ref=idx_ref, slot=slot):
                issue(idx_ref, slot, i)
                return 0
            lax.fori_loop(0, tile, body, 0)

    slot = slot_of(j)
    drain(slot)
    ahead_slot = slot_of(j + 2)
    for i in range(tile):
        issue(ahead2_ref, ahead_slot, i)
    y = None
    for kk in range(TOP_K):
        rows = pl.ds(pl.multiple_of((slot * TOP_K + kk) * tile, tile), tile)
        flat_scr[...] = ring_scr[rows].reshape(flat_scr.shape)
        term = gate_ref[:, kk:kk + 1] * flat_scr[...]
        y = term if y is None else y + term
    x2 = x1_ref[...] + g2_ref[0] * y
    o_ref[...] = _rms(x2, fn_ref[...])

    @pl.when(j == n_steps - 1)
    def _():
        drain(slot_of(j + 1))
        drain(slot_of(j + 2))


def _combine(dest, ys, gates_tok, x1, mod3, final_norm, seq):
    n_tok, d = x1.shape
    n_tiles = dest.shape[0]
    tiles_per_sample = seq // TOKEN_TILE
    return pl.pallas_call(
        _combine_kernel,
        grid=(n_tiles,),
        in_specs=[pl.BlockSpec((1, TOP_K, TOKEN_TILE), lambda j: (j, 0, 0),
                               memory_space=pltpu.SMEM),
                  pl.BlockSpec((1, TOP_K, TOKEN_TILE),
                               lambda j: (jnp.minimum(j + 1, n_tiles - 1), 0, 0),
                               memory_space=pltpu.SMEM),
                  pl.BlockSpec((1, TOP_K, TOKEN_TILE),
                               lambda j: (jnp.minimum(j + 2, n_tiles - 1), 0, 0),
                               memory_space=pltpu.SMEM),
                  pl.BlockSpec(memory_space=pl.ANY),
                  pl.BlockSpec((TOKEN_TILE, TOP_K), lambda j: (j, 0)),
                  pl.BlockSpec((TOKEN_TILE, d), lambda j: (j, 0)),
                  pl.BlockSpec((1, 1, d), lambda j: ((j // tiles_per_sample) * 6 + 5, 0, 0)),
                  pl.BlockSpec((1, d), lambda j: (0, 0))],
        out_specs=pl.BlockSpec((TOKEN_TILE, d), lambda j: (j, 0)),
        out_shape=jax.ShapeDtypeStruct((n_tok, d), F32),
        scratch_shapes=[pltpu.VMEM((COMBINE_RING * TOP_K * TOKEN_TILE, 1, d), F32),
                        pltpu.VMEM((TOKEN_TILE, d), F32),
                        pltpu.SemaphoreType.DMA((COMBINE_RING,))],
        compiler_params=_params(("arbitrary",)),
        name="combine",
    )(dest, dest, dest, ys, gates_tok, x1, mod3, final_norm)


def kernel(x, c, ctx, c_ctx, w_mod, b_mod, norm1, w_in, sgu_ln, sgu_w, sgu_b, lb_fwd, lb_bwd,
           hgrn_norm, w_out, norm2, router_w, router_b, w1, b1, w2, b2, final_norm):
    depth = w_mod.shape[0]
    assert depth == 1, "single-layer block only"
    bsz, seq, d = x.shape
    assert bsz + 1 <= MOD_ROWS
    width = w_in.shape[2] // N_PROJ
    n_exp = router_w.shape[2]
    n_tok = bsz * seq

    cc = jnp.zeros((MOD_ROWS, d), F32).at[:bsz].set(c).at[bsz].set(c_ctx)
    mod = _modulation(cc, w_mod[0], b_mod[0])
    mod3 = mod.reshape(MOD_ROWS * 6, 1, d)

    w_in_b = w_in[0].astype(BF16)
    norm1_l = norm1[0].reshape(1, d)
    proj_ctx = _inproj(ctx, mod3, lambda b: bsz, norm1_l, w_in_b[:, 2 * width:6 * width],
                       ctx.shape[1], "inproj_ctx")
    states = _scan(proj_ctx, 0, 1, 2, 3, lb_fwd, lb_bwd, None, False, "scan_ctx")
    proj = _inproj(x, mod3, lambda b: b, norm1_l, w_in_b, INPROJ_ROWS, "inproj")
    o_fwd, o_bwd = _scan(proj, 2, 3, 4, 5, lb_fwd, lb_bwd, states, True, "scan")

    rw_t = router_w[0].T
    rw_hi = rw_t.astype(BF16)
    rw_parts = jnp.concatenate([rw_hi, (rw_t - rw_hi.astype(F32)).astype(BF16)], axis=0)
    x1, h2_rows, lt3 = _postmix(
        proj, o_fwd, o_bwd, x, mod3, sgu_ln[0], sgu_w[0].astype(BF16),
        sgu_b[0].reshape(sgu_b.shape[1], sgu_b.shape[2], 1), hgrn_norm[0].reshape(1, -1),
        w_out[0].astype(BF16), norm2[0].reshape(1, d), rw_parts, router_b[0].reshape(n_exp, 1))

    n_blocks = -(-(n_tok * TOP_K + n_exp * (EXPERT_ROWS - 1)) // EXPERT_ROWS)
    dest, gates, pads = _route(lt3)
    tables = pads[:, :, 0]

    xs = _dispatch(tables[:3], dest, h2_rows, n_blocks * EXPERT_ROWS)
    ys = _experts(tables[3], tables[4], tables[1], xs, w1[0], b1[0], w2[0], b2[0])
    gates_tok = gates.transpose(0, 2, 1).reshape(n_tok, TOP_K)
    out = _combine(dest, ys, gates_tok, x1.reshape(n_tok, d), mod3, final_norm.reshape(1, d), seq)
    return out.reshape(bsz, seq, d)
```

```python
import functools
import math

import jax
import jax.numpy as jnp
import numpy as np
from jax import lax
from jax.experimental import pallas as pl
from jax.experimental.pallas import tpu as pltpu

F32 = jnp.float32
BF16 = jnp.bfloat16
EPS = 1e-6
LOG2_E = 1.4426950408889634
SUBLANES = 8

HEAD_DIM = 128
SGU_CHUNK = 128
N_PROJ = 7
TOP_K = 4
SWIGLU_LIMIT = 7.0
SWIGLU_ALPHA = 1.702

SCAN_CHUNK = 64
SCAN_SAMPLES = 4
INPROJ_ROWS = 1024
POSTMIX_ROWS = 512
EXPERT_ROWS = 512
EXPERT_PARTS = 4
TOKEN_TILE = 128
DISPATCH_RING = 3
COMBINE_RING = 3
ROUTE_GROUP = 4
MOD_ROWS = 16
VMEM_LIMIT = 56 * 1024 * 1024


def _params(sem, vmem=None):
    return pltpu.CompilerParams(dimension_semantics=sem, vmem_limit_bytes=vmem or VMEM_LIMIT)


def _sigmoid(x):
    return 1.0 / (1.0 + jnp.exp(-x))


def _mod_kernel(c_ref, w_ref, b_ref, o_ref):
    c = c_ref[...]
    s = c * _sigmoid(c)
    o_ref[...] = jnp.dot(s, w_ref[...], precision=lax.Precision.HIGHEST,
                         preferred_element_type=F32) + b_ref[...]


def _modulation(cc, w_mod, b_mod):
    d, n6 = w_mod.shape
    tn = 1024
    return pl.pallas_call(
        _mod_kernel,
        grid=(n6 // tn,),
        in_specs=[pl.BlockSpec((MOD_ROWS, d), lambda j: (0, 0)),
                  pl.BlockSpec((d, tn), lambda j: (0, j)),
                  pl.BlockSpec((1, tn), lambda j: (0, j))],
        out_specs=pl.BlockSpec((MOD_ROWS, tn), lambda j: (0, j)),
        out_shape=jax.ShapeDtypeStruct((MOD_ROWS, n6), F32),
        compiler_params=_params(("arbitrary",)),
        name="mod",
    )(cc, w_mod, b_mod.reshape(1, n6))


def _rms(x, g):
    return x * lax.rsqrt(jnp.mean(x * x, axis=-1, keepdims=True) + EPS) * g


def _rms_modulated(x, g, scale, shift):
    return _rms(x, g * (1.0 + scale)) + shift


def _gelu_tanh(x):
    a = -2.0 * LOG2_E * math.sqrt(2.0 / math.pi)
    return x / (1.0 + jnp.exp2(x * (a + (a * 0.044715) * (x * x))))


def _inproj_kernel(x_ref, sh_ref, sc_ref, g_ref, w_ref, o_ref, *, width):
    hb = _rms_modulated(x_ref[0], g_ref[...], sc_ref[0], sh_ref[0]).astype(BF16)
    for j in range(w_ref.shape[1] // width):
        cols = slice(j * width, (j + 1) * width)
        o_ref[0, :, cols] = jnp.dot(hb, w_ref[:, cols],
                                    preferred_element_type=F32).astype(o_ref.dtype)


def _inproj(x, mod3, mod_row, norm_g, w_bf16, rows, name):
    bsz, n, d = x.shape
    wcols = w_bf16.shape[1]
    return pl.pallas_call(
        functools.partial(_inproj_kernel, width=512),
        grid=(bsz, n // rows),
        in_specs=[pl.BlockSpec((1, rows, d), lambda b, j: (b, j, 0)),
                  pl.BlockSpec((1, 1, d), lambda b, j: (mod_row(b) * 6 + 0, 0, 0)),
                  pl.BlockSpec((1, 1, d), lambda b, j: (mod_row(b) * 6 + 1, 0, 0)),
                  pl.BlockSpec((1, d), lambda b, j: (0, 0)),
                  pl.BlockSpec((d, wcols), lambda b, j: (0, 0))],
        out_specs=pl.BlockSpec((1, rows, wcols), lambda b, j: (b, j, 0)),
        out_shape=jax.ShapeDtypeStruct((bsz, n, wcols), BF16),
        compiler_params=_params(("arbitrary", "arbitrary")),
        name=name,
    )(x, mod3, mod3, norm_g, w_bf16)


def _scan_constants(length):
    levels = int(math.log2(length))
    assert 1 << levels == length
    t = np.arange(length)
    sums = np.zeros((2, levels + 1, length, length), np.float32)
    masks = np.zeros((2, levels + 1, length, length), np.float32)
    for lv in range(levels):
        half = 1 << lv
        blk = t // (2 * half)
        mid = blk * 2 * half + half
        later = (t % (2 * half)) >= half
        for r in range(length):
            if later[r]:
                sums[0, lv, r, mid[r] + 1:r + 1] = 1.0
            else:
                sums[0, lv, r, r + 1:mid[r] + 1] = 1.0
        masks[0, lv] = (blk[:, None] == blk[None, :]) & later[:, None] & (~later)[None, :]
    sums[0, levels] = np.tril(np.ones((length, length), np.float32))
    masks[0, levels] = np.eye(length, dtype=np.float32)
    sums[1] = sums[0][:, ::-1, ::-1]
    masks[1] = masks[0][:, ::-1, ::-1]
    n_lv = levels + 1
    kept = [lv for lv in range(levels) if (1 << lv) < SUBLANES] + [levels]
    big = np.zeros((len(kept), 2, length, 2, 2, length), np.float32)
    pair = np.zeros((n_lv, 2, length, 2, length), np.float32)
    for d in range(2):
        big[:, d, :, 0, d, :] = sums[d][kept]
        big[:, d, :, 1, d, :] = sums[d][kept]
        pair[:, d, :, d, :] = masks[d]
    return (big.reshape(len(kept) * 2 * length, 4 * length),
            pair.reshape(n_lv, 2 * length, 2 * length), len(kept) - 1)


def _coarse_exponent(bp, half, length):
    parts = []
    for d in range(2):
        for start in range(0, length, 2 * half):
            lo = d * length + start
            mid = lo + half
            ref_row = mid if d == 0 else mid - 1
            ref = jnp.broadcast_to(bp[ref_row:ref_row + 1, :], (half, bp.shape[1]))
            first, second = bp[lo:mid, :], bp[mid:mid + half, :]
            parts += [ref - first, second - ref] if d == 0 else [first - ref, ref - second]
    return jnp.concatenate(parts, axis=0)


def _nt_dot(a, b):
    return lax.dot_general(a, b, (((1,), (1,)), ((), ())), preferred_element_type=F32)


def _lower_bound(tab):
    ex = jnp.exp(tab - jnp.max(tab, axis=0, keepdims=True))
    return ex[0:1] / jnp.sum(ex, axis=0, keepdims=True)


def _scan_kernel(*refs, heads, n_fine, emit_o, has_init):
    qf_ref, zf_ref, if_ref, qb_ref, zb_ref, ib_ref, tabf_ref, tabb_ref, sums_ref, masks_ref = refs[:10]
    pos = 10
    init_ref = None
    if has_init:
        init_ref = refs[pos]
        pos += 1
    n_out = 2 if emit_o else 1
    out_refs = refs[pos:pos + n_out]
    state_ref, d_scr = refs[pos + n_out:]

    c = pl.program_id(1)
    n_chunks = pl.num_programs(1)
    n_samples, length, width = qf_ref.shape
    levels = masks_ref.shape[0] - 1
    rows2 = 2 * length
    groups = n_samples * heads

    @pl.when(c == 0)
    def _():
        if has_init:
            state_ref[...] = init_ref[...].reshape(state_ref.shape)
        else:
            state_ref[...] = jnp.zeros_like(state_ref)

    def both(f_ref, b_ref):
        return jnp.concatenate(
            [jnp.concatenate([r[s] for s in range(n_samples)], axis=1) for r in (f_ref, b_ref)],
            axis=0)

    def per_dir(a, b):
        return jnp.concatenate([jnp.broadcast_to(a, (length, a.shape[1])),
                                jnp.broadcast_to(b, (length, b.shape[1]))], axis=0)

    def per_sample(a):
        return jnp.concatenate([a] * n_samples, axis=1)

    is_fwd = lax.broadcasted_iota(jnp.int32, (rows2, HEAD_DIM), 0) < length

    def split_dirs(a):
        return jnp.concatenate([jnp.where(is_fwd, a, 0.0), jnp.where(is_fwd, 0.0, a)], axis=1)

    lb = per_dir(per_sample(_lower_bound(tabf_ref[...])), per_sample(_lower_bound(tabb_ref[...])))
    z = both(zf_ref, zb_ref).astype(F32)
    q = both(qf_ref, qb_ref).astype(F32)
    v = both(if_ref, ib_ref)
    f = lb + (1.0 - lb) * _sigmoid(z)
    k = 1.0 - f
    g = jnp.log(f) * LOG2_E
    g_hi = g.astype(BF16)
    g_lo = (g - g_hi.astype(F32)).astype(BF16)
    g_parts = jnp.concatenate([g_hi, g_lo], axis=0)
    if emit_o:
        d_scr[...] = jnp.dot(sums_ref[...], g_parts, preferred_element_type=F32)
    else:
        d_scr[...] = jnp.dot(sums_ref[n_fine * rows2:(n_fine + 1) * rows2, :], g_parts,
                             preferred_element_type=F32)
    prefix_rows = slice(n_fine * rows2, (n_fine + 1) * rows2) if emit_o else slice(0, rows2)
    bp = d_scr[prefix_rows, :]
    btot_f = bp[length - 1:length]
    btot_b = bp[length:length + 1]

    if emit_o:
        qd = q * jnp.exp2(bp)
        o_carry = []
        for gi in range(groups):
            cols = slice(gi * HEAD_DIM, (gi + 1) * HEAD_DIM)
            o_carry.append(_nt_dot(split_dirs(qd[:, cols]).astype(BF16),
                                   state_ref[gi].astype(BF16)))
    kd = k * jnp.exp2(per_dir(btot_f, btot_b) - bp)
    dec_f = jnp.exp2(btot_f)
    dec_b = jnp.exp2(btot_b)
    for gi in range(groups):
        cols = slice(gi * HEAD_DIM, (gi + 1) * HEAD_DIM)
        dec = jnp.concatenate([dec_f[:, cols], dec_b[:, cols]], axis=1)
        vt = v[:, cols].astype(F32).T.astype(BF16)
        state_ref[gi] = state_ref[gi] * dec + jnp.dot(
            vt, split_dirs(kd[:, cols]).astype(BF16), preferred_element_type=F32)

    if emit_o:
        of_ref, ob_ref = out_refs
        scores = [None] * groups
        q_bf, k_bf = q.astype(BF16), k.astype(BF16)
        for lv in range(levels + 1):
            if lv == levels:
                xq, yk = q_bf, k_bf
            else:
                if lv < n_fine:
                    dl = d_scr[lv * rows2:(lv + 1) * rows2, :]
                else:
                    dl = _coarse_exponent(bp, 1 << lv, length)
                e = jnp.exp2(dl).astype(BF16)
                xq, yk = q_bf * e, k_bf * e
            keep = masks_ref[lv] > 0.5
            for gi in range(groups):
                cols = slice(gi * HEAD_DIM, (gi + 1) * HEAD_DIM)
                p = _nt_dot(xq[:, cols], yk[:, cols])
                scores[gi] = jnp.where(keep, p, 0.0 if scores[gi] is None else scores[gi])
        for gi in range(groups):
            cols = slice(gi * HEAD_DIM, (gi + 1) * HEAD_DIM)
            out_cols = slice((gi % heads) * HEAD_DIM, (gi % heads + 1) * HEAD_DIM)
            o = o_carry[gi] + jnp.dot(scores[gi].astype(BF16), v[:, cols],
                                      preferred_element_type=F32)
            of_ref[gi // heads, :, out_cols] = o[:length].astype(of_ref.dtype)
            ob_ref[gi // heads, :, out_cols] = o[length:].astype(ob_ref.dtype)

    if not emit_o:
        @pl.when(c == n_chunks - 1)
        def _():
            out_refs[0][...] = state_ref[...].reshape(out_refs[0].shape)


def _scan(proj, col_q, col_zf, col_zb, col_i, tab_f, tab_b, init, emit_o, name):
    bsz, n, _ = proj.shape
    width = tab_f.shape[1]
    heads = width // HEAD_DIM
    length = SCAN_CHUNK
    n_chunks = n // length
    sums_np, masks_np, n_fine = _scan_constants(length)
    sums = jnp.asarray(sums_np, BF16)
    masks = jnp.asarray(masks_np, F32)
    d_rows = sums.shape[0] if emit_o else 2 * length
    ns = SCAN_SAMPLES

    def fwd(col):
        return pl.BlockSpec((ns, length, width), lambda b, c: (b, c, col))

    def bwd(col):
        return pl.BlockSpec((ns, length, width), lambda b, c: (b, n_chunks - 1 - c, col))

    def whole(a):
        nd = a.ndim
        return pl.BlockSpec(a.shape, lambda b, c: (0,) * nd)

    in_specs = [fwd(col_q), fwd(col_zf), fwd(col_i), bwd(col_q), bwd(col_zb), bwd(col_i),
                whole(tab_f), whole(tab_b), whole(sums), whole(masks)]
    args = [proj, proj, proj, proj, proj, proj, tab_f, tab_b, sums, masks]
    state_shape = (heads, HEAD_DIM, 2 * HEAD_DIM)
    if init is not None:
        in_specs.append(pl.BlockSpec((ns,) + state_shape, lambda b, c: (b, 0, 0, 0)))
        args.append(init)
    if emit_o:
        out_spec = [pl.BlockSpec((ns, length, width), lambda b, c: (b, c, 0)),
                    pl.BlockSpec((ns, length, width), lambda b, c: (b, n_chunks - 1 - c, 0))]
        out_shape = [jax.ShapeDtypeStruct((bsz, n, width), BF16)] * 2
    else:
        out_spec = pl.BlockSpec((ns,) + state_shape, lambda b, c: (b, 0, 0, 0))
        out_shape = jax.ShapeDtypeStruct((bsz,) + state_shape, F32)
    return pl.pallas_call(
        functools.partial(_scan_kernel, heads=heads, n_fine=n_fine, emit_o=emit_o,
                          has_init=init is not None),
        grid=(bsz // ns, n_chunks),
        in_specs=in_specs,
        out_specs=out_spec,
        out_shape=out_shape,
        scratch_shapes=[pltpu.VMEM((ns * heads,) + state_shape[1:], F32),
                        pltpu.VMEM((d_rows, ns * width), F32)],
        compiler_params=_params(("arbitrary", "arbitrary")),
        name=name,
    )(*args)


def _postmix_kernel(u_ref, v_ref, g_ref, of_ref, ob_ref, x_ref, g1_ref, sh2_ref, sc2_ref,
                    ln_ref, ws_ref, bs_ref, hn_ref, wout_ref, n2_ref, rw_ref, rb_ref,
                    x1_ref, h2_ref, lg_ref, y_scr):
    rows = x_ref.shape[1]
    heads = ln_ref.shape[0]
    sgu_w = heads * HEAD_DIM
    u = _gelu_tanh(u_ref[0].astype(F32))
    v = _gelu_tanh(v_ref[0].astype(F32))
    for ch in range(rows // SGU_CHUNK):
        rs = slice(ch * SGU_CHUNK, (ch + 1) * SGU_CHUNK)
        for h in range(heads):
            cols = slice(h * HEAD_DIM, (h + 1) * HEAD_DIM)
            vh = v[rs, cols]
            mu = jnp.mean(vh, axis=-1, keepdims=True)
            var = jnp.mean(jnp.square(vh - mu), axis=-1, keepdims=True)
            vn = (vh - mu) * lax.rsqrt(var + EPS) * ln_ref[h:h + 1, :]
            z = jnp.dot(ws_ref[h], vn.astype(BF16), preferred_element_type=F32) + bs_ref[h]
            y_scr[rs, cols] = (u[rs, cols] * z).astype(BF16)
    o = of_ref[0].astype(F32) + ob_ref[0].astype(F32)
    gate = g_ref[0].astype(F32)
    gate = gate * _sigmoid(gate)
    for h in range(o.shape[1] // HEAD_DIM):
        cols = slice(h * HEAD_DIM, (h + 1) * HEAD_DIM)
        oh = o[:, cols]
        on = oh * lax.rsqrt(jnp.mean(oh * oh, axis=-1, keepdims=True) + EPS)
        y_scr[:, sgu_w + h * HEAD_DIM:sgu_w + (h + 1) * HEAD_DIM] = (
            on * hn_ref[:, cols] * gate[:, cols]).astype(BF16)
    y = jnp.dot(y_scr[...], wout_ref[...], preferred_element_type=F32)
    x1 = x_ref[0] + g1_ref[0] * y
    x1_ref[0] = x1
    h2 = _rms_modulated(x1, n2_ref[...], sc2_ref[0], sh2_ref[0])
    h2_ref[...] = h2.reshape(h2_ref.shape)
    n_exp = rb_ref.shape[0]
    h_hi = h2.astype(BF16)
    h_lo = (h2 - h_hi.astype(F32)).astype(BF16)
    part = _nt_dot(rw_ref[...], h_hi)
    lg = part[:n_exp] + part[n_exp:] + _nt_dot(rw_ref[:n_exp, :], h_lo) + rb_ref[...]
    for s in range(rows // TOKEN_TILE):
        lg_ref[s] = lg[:, s * TOKEN_TILE:(s + 1) * TOKEN_TILE]


def _postmix(proj, o_fwd, o_bwd, x, mod3, sgu_ln, sgu_w_bf16, sgu_b3, hgrn_norm, w_out_bf16,
             norm2, router_w, router_b):
    bsz, n, d = x.shape
    width = o_fwd.shape[2]
    rows = POSTMIX_ROWS
    n_exp = router_b.shape[0]
    nj = n // rows
    sub = rows // TOKEN_TILE

    def colgroup(col):
        return pl.BlockSpec((1, rows, width), lambda b, j: (b, j, col))

    def modrow(idx):
        return pl.BlockSpec((1, 1, d), lambda b, j: (b * 6 + idx, 0, 0))

    def whole(a):
        nd = a.ndim
        return pl.BlockSpec(a.shape, lambda b, j: (0,) * nd)

    return pl.pallas_call(
        _postmix_kernel,
        grid=(bsz, nj),
        in_specs=[colgroup(0), colgroup(1), colgroup(6), colgroup(0), colgroup(0),
                  pl.BlockSpec((1, rows, d), lambda b, j: (b, j, 0)),
                  modrow(2), modrow(3), modrow(4),
                  whole(sgu_ln), whole(sgu_w_bf16), whole(sgu_b3), whole(hgrn_norm),
                  whole(w_out_bf16), whole(norm2), whole(router_w), whole(router_b)],
        out_specs=[pl.BlockSpec((1, rows, d), lambda b, j: (b, j, 0)),
                   pl.BlockSpec((rows, 1, d), lambda b, j: (b * nj + j, 0, 0)),
                   pl.BlockSpec((sub, n_exp, TOKEN_TILE), lambda b, j: (b * nj + j, 0, 0))],
        out_shape=[jax.ShapeDtypeStruct((bsz, n, d), F32),
                   jax.ShapeDtypeStruct((bsz * n, 1, d), F32),
                   jax.ShapeDtypeStruct((bsz * n // TOKEN_TILE, n_exp, TOKEN_TILE), F32)],
        scratch_shapes=[pltpu.VMEM((rows, d), BF16)],
        compiler_params=_params(("arbitrary", "arbitrary")),
        name="postmix",
    )(proj, proj, proj, o_fwd, o_bwd, x, mod3, mod3, mod3, sgu_ln, sgu_w_bf16, sgu_b3, hgrn_norm,
      w_out_bf16, norm2, router_w, router_b)


def _route_kernel(lt_ref, upper_ref, ones_ref, ltri_ref,
                  dest_ref, gate_ref, pads_ref, pos_scr, eidx_scr, *, block_rows):
    n_tiles, n_exp, lanes = lt_ref.shape
    iota_e = lax.broadcasted_iota(jnp.int32, (n_exp, lanes), 0).astype(F32)
    group = ROUTE_GROUP
    wide_iota = jnp.concatenate([iota_e] * group, axis=1)

    def part(a, t):
        return a[:, t * lanes:(t + 1) * lanes]

    def tile_body(jg, carry):
        j0 = jg * group
        l = jnp.concatenate([lt_ref[j0 + t] for t in range(group)], axis=1)
        picks, vals = [], []
        for kk in range(TOP_K):
            m = jnp.max(l, axis=0, keepdims=True)
            idx = jnp.min(jnp.where(l == m, wide_iota, float(n_exp)), axis=0, keepdims=True)
            pick = wide_iota == idx
            picks.append(pick)
            vals.append(m)
            l = jnp.where(pick, -jnp.inf, l)
            for t in range(group):
                eidx_scr[j0 + t, kk:kk + 1, :] = part(idx, t)
        exps = [jnp.exp(v - vals[0]) for v in vals]
        tot = exps[0]
        for e in exps[1:]:
            tot = tot + e
        for kk in range(TOP_K):
            gate = exps[kk] / tot
            for t in range(group):
                gate_ref[j0 + t, kk:kk + 1, :] = part(gate, t)
        chosen = picks[0]
        for p in picks[1:]:
            chosen = chosen | p
        cb = jnp.where(chosen, 1.0, 0.0).astype(BF16)
        for t in range(group):
            cbt = part(cb, t)
            before = carry + jnp.dot(cbt, upper_ref[...], preferred_element_type=F32)
            for kk in range(TOP_K):
                pos_scr[j0 + t, kk:kk + 1, :] = jnp.sum(
                    jnp.where(part(picks[kk], t), before, 0.0), axis=0, keepdims=True)
            carry = carry + jnp.dot(cbt, ones_ref[...], preferred_element_type=F32)
        return carry

    counts = lax.fori_loop(0, n_tiles // group, tile_body, jnp.zeros((n_exp, lanes), F32))
    n_blk = jnp.ceil(counts * (1.0 / block_rows))
    end_blk = jnp.dot(ltri_ref[...], n_blk.astype(BF16), preferred_element_type=F32)
    start_row = (end_blk - n_blk) * block_rows

    def dest_body(j, _):
        for kk in range(TOP_K):
            idx = eidx_scr[j, kk:kk + 1, :]
            base = jnp.sum(jnp.where(iota_e == idx, start_row, 0.0), axis=0, keepdims=True)
            dest_ref[j, kk:kk + 1, :] = (base + pos_scr[j, kk:kk + 1, :]).astype(jnp.int32)
        return 0

    lax.fori_loop(0, n_tiles, dest_body, 0)

    pads_ref[0] = (start_row + counts).astype(jnp.int32)
    pads_ref[1] = (n_blk * block_rows - counts).astype(jnp.int32)
    pads_ref[2] = end_blk.astype(jnp.int32)
    pads_ref[3] = start_row.astype(jnp.int32)
    pads_ref[4] = n_blk.astype(jnp.int32)


def _route(lt3):
    n_tiles, n_exp, lanes = lt3.shape
    upper = jnp.asarray(np.triu(np.ones((lanes, lanes), np.float32), 1), BF16)
    ones = jnp.ones((lanes, lanes), BF16)
    ltri = jnp.asarray(np.tril(np.ones((n_exp, n_exp), np.float32)), BF16)
    vm = pl.BlockSpec(memory_space=pltpu.VMEM)
    return pl.pallas_call(
        functools.partial(_route_kernel, block_rows=EXPERT_ROWS),
        in_specs=[vm, vm, vm, vm],
        out_specs=[vm, vm, vm],
        out_shape=[jax.ShapeDtypeStruct((n_tiles, TOP_K, lanes), jnp.int32),
                   jax.ShapeDtypeStruct((n_tiles, TOP_K, lanes), F32),
                   jax.ShapeDtypeStruct((5, n_exp, lanes), jnp.int32)],
        scratch_shapes=[pltpu.VMEM((n_tiles, TOP_K, lanes), F32),
                        pltpu.VMEM((n_tiles, TOP_K, lanes), F32)],
        compiler_params=pltpu.CompilerParams(vmem_limit_bytes=VMEM_LIMIT),
        name="route",
    )(lt3, upper, ones, ltri)


def _pad_sizes():
    return [1 << b for b in reversed(range(int(math.log2(EXPERT_ROWS))))]


def _dispatch_kernel(pads_ref, dest_ref, h2_hbm, xs_hbm, zero_scr, ring_scr, sem, load_sem, zsem):
    j = pl.program_id(0)
    n_steps = pl.num_programs(0)
    tile = dest_ref.shape[2]
    n_exp = pads_ref.shape[1]

    def pad_copies(e, wait):
        off = pads_ref[0, e]
        cnt = pads_ref[1, e]
        for sz in _pad_sizes():
            cp = pltpu.make_async_copy(zero_scr.at[pl.ds(0, sz)], xs_hbm.at[pl.ds(off, sz)], zsem)

            @pl.when((cnt & sz) != 0)
            def _():
                if wait:
                    cp.wait()
                else:
                    cp.start()
            off = off + (cnt & sz)

    half = zero_scr.shape[0]
    n_blocks = xs_hbm.shape[0] // (2 * half)
    n_used = pads_ref[2, n_exp - 1]

    def tail_copies(blk, wait):
        for part in range(2):
            cp = pltpu.make_async_copy(
                zero_scr, xs_hbm.at[pl.ds((blk * 2 + part) * half, half)], zsem)
            if wait:
                cp.wait()
            else:
                cp.start()

    @pl.when(j == 0)
    def _():
        zero_scr[...] = jnp.zeros_like(zero_scr)

        def start(e, _):
            pad_copies(e, False)
            return 0
        lax.fori_loop(0, n_exp, start, 0)

        def start_tail(blk, _):
            tail_copies(blk, False)
            return 0
        lax.fori_loop(n_used, n_blocks, start_tail, 0)

    ring = ring_scr.shape[0] // tile

    def load(step):
        slot = step % ring
        return pltpu.make_async_copy(h2_hbm.at[pl.ds(step * tile, tile)],
                                     ring_scr.at[pl.ds(slot * tile, tile)], load_sem.at[slot])

    def finish_rows(step):
        for kk in range(TOP_K):
            pltpu.make_async_copy(ring_scr.at[pl.ds(0, tile)], xs_hbm.at[pl.ds(0, tile)],
                                  sem.at[step % 2]).wait()

    @pl.when(j == 0)
    def _():
        load(0).start()
        load(1).start()

    load(j).wait()
    base = (j % ring) * tile
    for i in range(tile):
        for kk in range(TOP_K):
            pltpu.make_async_copy(ring_scr.at[base + i], xs_hbm.at[dest_ref[0, kk, i]],
                                  sem.at[j % 2]).start(priority=kk % 2)

    @pl.when(j > 0)
    def _():
        finish_rows(j - 1)

    @pl.when(j + 2 < n_steps)
    def _():
        load(j + 2).start()

    @pl.when(j == n_steps - 1)
    def _():
        finish_rows(j)

        def finish(e, _):
            pad_copies(e, True)
            return 0
        lax.fori_loop(0, n_exp, finish, 0)

        def finish_tail(blk, _):
            tail_copies(blk, True)
            return 0
        lax.fori_loop(n_used, n_blocks, finish_tail, 0)


def _dispatch(pads, dest, h2_rows, n_rows):
    n_tok, _, d = h2_rows.shape
    n_tiles = dest.shape[0]
    assert n_tiles >= 2
    grid_spec = pltpu.PrefetchScalarGridSpec(
        num_scalar_prefetch=1,
        grid=(n_tiles,),
        in_specs=[pl.BlockSpec((1, TOP_K, TOKEN_TILE), lambda j, p: (j, 0, 0),
                               memory_space=pltpu.SMEM),
                  pl.BlockSpec(memory_space=pl.ANY)],
        out_specs=pl.BlockSpec(memory_space=pl.ANY),
        scratch_shapes=[pltpu.VMEM((EXPERT_ROWS // 2, 1, d), F32),
                        pltpu.VMEM((DISPATCH_RING * TOKEN_TILE, 1, d), F32),
                        pltpu.SemaphoreType.DMA((2,)),
                        pltpu.SemaphoreType.DMA((DISPATCH_RING,)),
                        pltpu.SemaphoreType.DMA(())],
    )
    return pl.pallas_call(
        _dispatch_kernel,
        grid_spec=grid_spec,
        out_shape=jax.ShapeDtypeStruct((n_rows, 1, d), F32),
        compiler_params=_params(("arbitrary",)),
        name="dispatch",
    )(pads, dest, h2_rows)


def _experts_kernel(start_ref, nblk_ref, pad_ref, xs_hbm, w1_ref, b1_ref, w2_ref, b2_ref, ys_hbm,
                    w1_scr, w2_scr, x_ring, y_ring, x_scr, out_scr, act_scr, load_sem, store_sem):
    e = pl.program_id(0)
    n_exp = pl.num_programs(0)
    rows = x_scr.shape[0]
    shift = int(math.log2(rows))
    n_blk = nblk_ref[e]
    first = lax.shift_right_logical(start_ref[e], shift)
    n_used = lax.shift_right_logical(start_ref[n_exp - 1], shift) + nblk_ref[n_exp - 1]
    d, two_f = w1_scr.shape
    f = two_f // 2

    def slot_rows(g):
        return pl.ds(pl.multiple_of(jnp.bitwise_and(g, 1) * rows, rows), rows)

    def load(g):
        return pltpu.make_async_copy(xs_hbm.at[pl.ds(g * rows, rows)], x_ring.at[slot_rows(g)],
                                     load_sem.at[jnp.bitwise_and(g, 1)])

    def store(g):
        return pltpu.make_async_copy(y_ring.at[slot_rows(g)], ys_hbm.at[pl.ds(g * rows, rows)],
                                     store_sem.at[jnp.bitwise_and(g, 1)])

    def finish(p):
        y_ring[slot_rows(p)] = out_scr[...].reshape((rows, 1, d))
        store(p).start()

    @pl.when((e == 0) & (n_used > 0))
    def _():
        load(0).start()
        out_scr[...] = jnp.zeros_like(out_scr)

    @pl.when(n_blk > 0)
    def _():
        step = 64

        def cast1(r, _):
            rs = pl.ds(pl.multiple_of(r * step, step), step)
            w1_scr[rs, :] = w1_ref[0, rs, :].astype(BF16)
            return 0
        lax.fori_loop(0, d // step, cast1, 0)

        def cast2(r, _):
            rs = pl.ds(pl.multiple_of(r * step, step), step)
            w2_scr[rs, :] = w2_ref[0, rs, :].astype(BF16)
            return 0
        lax.fori_loop(0, f // step, cast2, 0)

    def block(i, _):
        g = first + i
        load(g).wait()

        @pl.when(g + 1 < n_used)
        def _():
            load(g + 1).start()

        prev = jnp.maximum(g - 1, 0)

        @pl.when((g == 1) | (g >= 3))
        def _():
            store(prev).wait()

        def mlp(m):
            base = pl.multiple_of(jnp.bitwise_and(g, 1) * rows, rows)
            x_scr[0:m, :] = x_ring[pl.ds(base, m)].reshape((m, d))
            xb = x_scr[0:m, :].astype(BF16)
            finish(prev)
            width = 512
            for n in range(f // width):
                cg = slice(n * width, (n + 1) * width)
                cu = slice(f + n * width, f + (n + 1) * width)
                gate = jnp.dot(xb, w1_scr[:, cg], preferred_element_type=F32) + b1_ref[0, :, cg]
                up = jnp.dot(xb, w1_scr[:, cu], preferred_element_type=F32) + b1_ref[0, :, cu]
                gate = jnp.minimum(gate, SWIGLU_LIMIT)
                up = jnp.clip(up, -SWIGLU_LIMIT, SWIGLU_LIMIT)
                act_scr[0:m, cg] = (gate * _sigmoid(SWIGLU_ALPHA * gate)
                                    * (up + 1.0)).astype(BF16)
            out_scr[0:m, :] = jnp.dot(act_scr[0:m, :], w2_scr[...],
                                      preferred_element_type=F32) + b2_ref[0]

        quarter = rows // EXPERT_PARTS
        parts = jnp.where(i == n_blk - 1,
                          EXPERT_PARTS - lax.div(pad_ref[e], quarter), EXPERT_PARTS)
        for n_parts in range(1, EXPERT_PARTS + 1):
            @pl.when(parts == n_parts)
            def _(n_parts=n_parts):
                mlp(n_parts * quarter)
        return 0
    lax.fori_loop(0, n_blk, block, 0)

    @pl.when((e == n_exp - 1) & (n_used > 0))
    def _():
        last = n_used - 1

        @pl.when(last != 1)
        def _():
            store(last).wait()

        finish(last)
        store(last).wait()

        @pl.when(last >= 1)
        def _():
            store(last - 1).wait()


def _experts(start_row, n_blk, n_pad, xs, w1, b1, w2, b2):
    n_rows, _, d = xs.shape
    n_exp, _, two_f = w1.shape
    f = two_f // 2
    rows = EXPERT_ROWS
    grid_spec = pltpu.PrefetchScalarGridSpec(
        num_scalar_prefetch=3,
        grid=(n_exp,),
        in_specs=[pl.BlockSpec(memory_space=pl.ANY),
                  pl.BlockSpec((1, d, two_f), lambda e, st, nb, pd: (e, 0, 0)),
                  pl.BlockSpec((1, 1, two_f), lambda e, st, nb, pd: (e, 0, 0)),
                  pl.BlockSpec((1, f, d), lambda e, st, nb, pd: (e, 0, 0)),
                  pl.BlockSpec((1, 1, d), lambda e, st, nb, pd: (e, 0, 0))],
        out_specs=pl.BlockSpec(memory_space=pl.ANY),
        scratch_shapes=[pltpu.VMEM((d, two_f), BF16),
                        pltpu.VMEM((f, d), BF16)]
        + [pltpu.VMEM((2 * rows, 1, d), F32) for _ in range(2)]
        + [pltpu.VMEM((rows, d), F32),
           pltpu.VMEM((rows, d), F32),
           pltpu.VMEM((rows, f), BF16),
           pltpu.SemaphoreType.DMA((2,)),
           pltpu.SemaphoreType.DMA((2,))],
    )
    return pl.pallas_call(
        _experts_kernel,
        grid_spec=grid_spec,
        out_shape=jax.ShapeDtypeStruct((n_rows, 1, d), F32),
        input_output_aliases={3: 0},
        compiler_params=_params(("arbitrary",)),
        name="experts",
    )(start_row, n_blk, n_pad, xs, w1, b1.reshape(n_exp, 1, two_f), w2, b2.reshape(n_exp, 1, d))


def _combine_kernel(dest_ref, ahead1_ref, ahead2_ref, ys_hbm, gate_ref, x1_ref, g2_ref, fn_ref,
                    o_ref, ring_scr, flat_scr, sem):
    j = pl.program_id(0)
    n_steps = pl.num_programs(0)
    tile = dest_ref.shape[2]
    ring = ring_scr.shape[0] // (TOP_K * tile)

    def slot_of(step):
        return step % ring

    def issue(idx_ref, slot, i):
        for kk in range(TOP_K):
            pltpu.make_async_copy(ys_hbm.at[idx_ref[0, kk, i]],
                                  ring_scr.at[(slot * TOP_K + kk) * tile + i],
                                  sem.at[slot]).start(priority=kk % 2)

    def drain(slot):
        for kk in range(TOP_K):
            pltpu.make_async_copy(ys_hbm.at[pl.ds(0, tile)], ring_scr.at[pl.ds(0, tile)],
                                  sem.at[slot]).wait()

    @pl.when(j == 0)
    def _():
        for idx_ref, slot in ((dest_ref, 0), (ahead1_ref, 1)):
            def body(i, _, idx_ref=idx_ref, slot=slot):
                issue(idx_ref, slot, i)
                return 0
            lax.fori_loop(0, tile, body, 0)

    slot = slot_of(j)
    drain(slot)
    ahead_slot = slot_of(j + 2)
    for i in range(tile):
        issue(ahead2_ref, ahead_slot, i)
    y = None
    for kk in range(TOP_K):
        rows = pl.ds(pl.multiple_of((slot * TOP_K + kk) * tile, tile), tile)
        flat_scr[...] = ring_scr[rows].reshape(flat_scr.shape)
        term = gate_ref[:, kk:kk + 1] * flat_scr[...]
        y = term if y is None else y + term
    x2 = x1_ref[...] + g2_ref[0] * y
    o_ref[...] = _rms(x2, fn_ref[...])

    @pl.when(j == n_steps - 1)
    def _():
        drain(slot_of(j + 1))
        drain(slot_of(j + 2))


def _combine(dest, ys, gates_tok, x1, mod3, final_norm, seq):
    n_tok, d = x1.shape
    n_tiles = dest.shape[0]
    tiles_per_sample = seq // TOKEN_TILE
    return pl.pallas_call(
        _combine_kernel,
        grid=(n_tiles,),
        in_specs=[pl.BlockSpec((1, TOP_K, TOKEN_TILE), lambda j: (j, 0, 0),
                               memory_space=pltpu.SMEM),
                  pl.BlockSpec((1, TOP_K, TOKEN_TILE),
                               lambda j: (jnp.minimum(j + 1, n_tiles - 1), 0, 0),
                               memory_space=pltpu.SMEM),
                  pl.BlockSpec((1, TOP_K, TOKEN_TILE),
                               lambda j: (jnp.minimum(j + 2, n_tiles - 1), 0, 0),
                               memory_space=pltpu.SMEM),
                  pl.BlockSpec(memory_space=pl.ANY),
                  pl.BlockSpec((TOKEN_TILE, TOP_K), lambda j: (j, 0)),
                  pl.BlockSpec((TOKEN_TILE, d), lambda j: (j, 0)),
                  pl.BlockSpec((1, 1, d), lambda j: ((j // tiles_per_sample) * 6 + 5, 0, 0)),
                  pl.BlockSpec((1, d), lambda j: (0, 0))],
        out_specs=pl.BlockSpec((TOKEN_TILE, d), lambda j: (j, 0)),
        out_shape=jax.ShapeDtypeStruct((n_tok, d), F32),
        scratch_shapes=[pltpu.VMEM((COMBINE_RING * TOP_K * TOKEN_TILE, 1, d), F32),
                        pltpu.VMEM((TOKEN_TILE, d), F32),
                        pltpu.SemaphoreType.DMA((COMBINE_RING,))],
        compiler_params=_params(("arbitrary",)),
        name="combine",
    )(dest, dest, dest, ys, gates_tok, x1, mod3, final_norm)


def kernel(x, c, ctx, c_ctx, w_mod, b_mod, norm1, w_in, sgu_ln, sgu_w, sgu_b, lb_fwd, lb_bwd,
           hgrn_norm, w_out, norm2, router_w, router_b, w1, b1, w2, b2, final_norm):
    depth = w_mod.shape[0]
    assert depth == 1, "single-layer block only"
    bsz, seq, d = x.shape
    assert bsz + 1 <= MOD_ROWS
    width = w_in.shape[2] // N_PROJ
    n_exp = router_w.shape[2]
    n_tok = bsz * seq

    cc = jnp.zeros((MOD_ROWS, d), F32).at[:bsz].set(c).at[bsz].set(c_ctx)
    mod = _modulation(cc, w_mod[0], b_mod[0])
    mod3 = mod.reshape(MOD_ROWS * 6, 1, d)

    w_in_b = w_in[0].astype(BF16)
    norm1_l = norm1[0].reshape(1, d)
    n_ctx = ctx.shape[1]
    proj_ctx = _inproj(ctx.reshape(1, bsz * n_ctx, d), mod3, lambda b: bsz, norm1_l,
                       w_in_b[:, 2 * width:6 * width], INPROJ_ROWS,
                       "inproj_ctx").reshape(bsz, n_ctx, 4 * width)
    states = _scan(proj_ctx, 0, 1, 2, 3, lb_fwd, lb_bwd, None, False, "scan_ctx")
    proj = _inproj(x, mod3, lambda b: b, norm1_l, w_in_b, INPROJ_ROWS, "inproj")
    o_fwd, o_bwd = _scan(proj, 2, 3, 4, 5, lb_fwd, lb_bwd, states, True, "scan")

    rw_t = router_w[0].T
    rw_hi = rw_t.astype(BF16)
    rw_parts = jnp.concatenate([rw_hi, (rw_t - rw_hi.astype(F32)).astype(BF16)], axis=0)
    x1, h2_rows, lt3 = _postmix(
        proj, o_fwd, o_bwd, x, mod3, sgu_ln[0], sgu_w[0].astype(BF16),
        sgu_b[0].reshape(sgu_b.shape[1], sgu_b.shape[2], 1), hgrn_norm[0].reshape(1, -1),
        w_out[0].astype(BF16), norm2[0].reshape(1, d), rw_parts, router_b[0].reshape(n_exp, 1))

    n_blocks = -(-(n_tok * TOP_K + n_exp * (EXPERT_ROWS - 1)) // EXPERT_ROWS)
    dest, gates, pads = _route(lt3)
    tables = pads[:, :, 0]

    xs = _dispatch(tables[:3], dest, h2_rows, n_blocks * EXPERT_ROWS)
    ys = _experts(tables[3], tables[4], tables[1], xs, w1[0], b1[0], w2[0], b2[0])
    gates_tok = gates.transpose(0, 2, 1).reshape(n_tok, TOP_K)
    out = _combine(dest, ys, gates_tok, x1.reshape(n_tok, d), mod3, final_norm.reshape(1, d), seq)
    return out.reshape(bsz, seq, d)
```

```python
import functools
import math

import jax
import jax.numpy as jnp
import numpy as np
from jax import lax
from jax.experimental import pallas as pl
from jax.experimental.pallas import tpu as pltpu

F32 = jnp.float32
BF16 = jnp.bfloat16
EPS = 1e-6
LOG2_E = 1.4426950408889634
SUBLANES = 8

HEAD_DIM = 128
SGU_CHUNK = 128
N_PROJ = 7
TOP_K = 4
SWIGLU_LIMIT = 7.0
SWIGLU_ALPHA = 1.702

SCAN_CHUNK = 64
SCAN_SAMPLES = 4
INPROJ_ROWS = 1024
POSTMIX_ROWS = 512
EXPERT_ROWS = 512
EXPERT_PARTS = 4
TOKEN_TILE = 128
DISPATCH_RING = 3
COMBINE_RING = 3
ROUTE_GROUP = 4
MOD_ROWS = 16
MOD_COLS = 1024
MATMUL_COLS = 512
VMEM_LIMIT = 56 * 1024 * 1024


def _params(sem):
    return pltpu.CompilerParams(dimension_semantics=sem, vmem_limit_bytes=VMEM_LIMIT)


def _sigmoid(x):
    return 1.0 / (1.0 + jnp.exp(-x))


def _mod_kernel(c_ref, w_ref, b_ref, o_ref):
    c = c_ref[...]
    s = c * _sigmoid(c)
    o_ref[...] = jnp.dot(s, w_ref[...], precision=lax.Precision.HIGHEST,
                         preferred_element_type=F32) + b_ref[...]


def _modulation(cc, w_mod, b_mod):
    d, n6 = w_mod.shape
    tn = MOD_COLS
    return pl.pallas_call(
        _mod_kernel,
        grid=(n6 // tn,),
        in_specs=[pl.BlockSpec((MOD_ROWS, d), lambda j: (0, 0)),
                  pl.BlockSpec((d, tn), lambda j: (0, j)),
                  pl.BlockSpec((1, tn), lambda j: (0, j))],
        out_specs=pl.BlockSpec((MOD_ROWS, tn), lambda j: (0, j)),
        out_shape=jax.ShapeDtypeStruct((MOD_ROWS, n6), F32),
        compiler_params=_params(("arbitrary",)),
        name="mod",
    )(cc, w_mod, b_mod.reshape(1, n6))


def _rms(x, g):
    return x * lax.rsqrt(jnp.mean(x * x, axis=-1, keepdims=True) + EPS) * g


def _rms_modulated(x, g, scale, shift):
    return _rms(x, g * (1.0 + scale)) + shift


def _gelu_tanh(x):
    a = -2.0 * LOG2_E * math.sqrt(2.0 / math.pi)
    return x / (1.0 + jnp.exp2(x * (a + (a * 0.044715) * (x * x))))


def _inproj_kernel(x_ref, sh_ref, sc_ref, g_ref, w_ref, o_ref, *, width):
    hb = _rms_modulated(x_ref[0], g_ref[...], sc_ref[0], sh_ref[0]).astype(BF16)
    for j in range(w_ref.shape[1] // width):
        cols = slice(j * width, (j + 1) * width)
        o_ref[0, :, cols] = jnp.dot(hb, w_ref[:, cols],
                                    preferred_element_type=F32).astype(o_ref.dtype)


def _inproj(x, mod3, mod_row, norm_g, w_bf16, rows, name):
    bsz, n, d = x.shape
    wcols = w_bf16.shape[1]
    return pl.pallas_call(
        functools.partial(_inproj_kernel, width=MATMUL_COLS),
        grid=(bsz, n // rows),
        in_specs=[pl.BlockSpec((1, rows, d), lambda b, j: (b, j, 0)),
                  pl.BlockSpec((1, 1, d), lambda b, j: (mod_row(b) * 6 + 0, 0, 0)),
                  pl.BlockSpec((1, 1, d), lambda b, j: (mod_row(b) * 6 + 1, 0, 0)),
                  pl.BlockSpec((1, d), lambda b, j: (0, 0)),
                  pl.BlockSpec((d, wcols), lambda b, j: (0, 0))],
        out_specs=pl.BlockSpec((1, rows, wcols), lambda b, j: (b, j, 0)),
        out_shape=jax.ShapeDtypeStruct((bsz, n, wcols), BF16),
        compiler_params=_params(("arbitrary", "arbitrary")),
        name=name,
    )(x, mod3, mod3, norm_g, w_bf16)


def _scan_constants(length):
    levels = int(math.log2(length))
    assert 1 << levels == length
    t = np.arange(length)
    sums = np.zeros((2, levels + 1, length, length), np.float32)
    masks = np.zeros((2, levels + 1, length, length), np.float32)
    for lv in range(levels):
        half = 1 << lv
        blk = t // (2 * half)
        mid = blk * 2 * half + half
        later = (t % (2 * half)) >= half
        for r in range(length):
            if later[r]:
                sums[0, lv, r, mid[r] + 1:r + 1] = 1.0
            else:
                sums[0, lv, r, r + 1:mid[r] + 1] = 1.0
        masks[0, lv] = (blk[:, None] == blk[None, :]) & later[:, None] & (~later)[None, :]
    sums[0, levels] = np.tril(np.ones((length, length), np.float32))
    masks[0, levels] = np.eye(length, dtype=np.float32)
    sums[1] = sums[0][:, ::-1, ::-1]
    masks[1] = masks[0][:, ::-1, ::-1]
    n_lv = levels + 1
    kept = [lv for lv in range(levels) if (1 << lv) < SUBLANES] + [levels]
    big = np.zeros((len(kept), 2, length, 2, 2, length), np.float32)
    pair = np.zeros((n_lv, 2, length, 2, length), np.float32)
    for d in range(2):
        big[:, d, :, 0, d, :] = sums[d][kept]
        big[:, d, :, 1, d, :] = sums[d][kept]
        pair[:, d, :, d, :] = masks[d]
    return (big.reshape(len(kept) * 2 * length, 4 * length),
            pair.reshape(n_lv, 2 * length, 2 * length), len(kept) - 1)


def _coarse_exponent(bp, half, length):
    parts = []
    for d in range(2):
        for start in range(0, length, 2 * half):
            lo = d * length + start
            mid = lo + half
            ref_row = mid if d == 0 else mid - 1
            ref = jnp.broadcast_to(bp[ref_row:ref_row + 1, :], (half, bp.shape[1]))
            first, second = bp[lo:mid, :], bp[mid:mid + half, :]
            parts += [ref - first, second - ref] if d == 0 else [first - ref, ref - second]
    return jnp.concatenate(parts, axis=0)


def _nt_dot(a, b):
    return lax.dot_general(a, b, (((1,), (1,)), ((), ())), preferred_element_type=F32)


def _lower_bound(tab):
    ex = jnp.exp(tab - jnp.max(tab, axis=0, keepdims=True))
    return ex[0:1] / jnp.sum(ex, axis=0, keepdims=True)


def _scan_kernel(*refs, heads, n_fine, emit_o, has_init):
    qf_ref, zf_ref, if_ref, qb_ref, zb_ref, ib_ref, tabf_ref, tabb_ref, sums_ref, masks_ref = refs[:10]
    pos = 10
    init_ref = None
    if has_init:
        init_ref = refs[pos]
        pos += 1
    n_out = 2 if emit_o else 1
    out_refs = refs[pos:pos + n_out]
    state_ref, d_scr = refs[pos + n_out:]

    c = pl.program_id(1)
    n_chunks = pl.num_programs(1)
    n_samples, length, width = qf_ref.shape
    levels = masks_ref.shape[0] - 1
    rows2 = 2 * length
    groups = n_samples * heads

    @pl.when(c == 0)
    def _():
        if has_init:
            state_ref[...] = init_ref[...].reshape(state_ref.shape)
        else:
            state_ref[...] = jnp.zeros_like(state_ref)

    def both(f_ref, b_ref):
        return jnp.concatenate(
            [jnp.concatenate([r[s] for s in range(n_samples)], axis=1) for r in (f_ref, b_ref)],
            axis=0)

    def per_dir(a, b):
        return jnp.concatenate([jnp.broadcast_to(a, (length, a.shape[1])),
                                jnp.broadcast_to(b, (length, b.shape[1]))], axis=0)

    def per_sample(a):
        return jnp.concatenate([a] * n_samples, axis=1)

    is_fwd = lax.broadcasted_iota(jnp.int32, (rows2, HEAD_DIM), 0) < length

    def split_dirs(a):
        return jnp.concatenate([jnp.where(is_fwd, a, 0.0), jnp.where(is_fwd, 0.0, a)], axis=1)

    lb = per_dir(per_sample(_lower_bound(tabf_ref[...])), per_sample(_lower_bound(tabb_ref[...])))
    z = both(zf_ref, zb_ref).astype(F32)
    q = both(qf_ref, qb_ref).astype(F32)
    v = both(if_ref, ib_ref)
    f = lb + (1.0 - lb) * _sigmoid(z)
    k = 1.0 - f
    g = jnp.log(f) * LOG2_E
    g_hi = g.astype(BF16)
    g_lo = (g - g_hi.astype(F32)).astype(BF16)
    g_parts = jnp.concatenate([g_hi, g_lo], axis=0)
    if emit_o:
        d_scr[...] = jnp.dot(sums_ref[...], g_parts, preferred_element_type=F32)
    else:
        d_scr[...] = jnp.dot(sums_ref[n_fine * rows2:(n_fine + 1) * rows2, :], g_parts,
                             preferred_element_type=F32)
    prefix_rows = slice(n_fine * rows2, (n_fine + 1) * rows2) if emit_o else slice(0, rows2)
    bp = d_scr[prefix_rows, :]
    btot_f = bp[length - 1:length]
    btot_b = bp[length:length + 1]

    if emit_o:
        qd = q * jnp.exp2(bp)
        o_carry = []
        for gi in range(groups):
            cols = slice(gi * HEAD_DIM, (gi + 1) * HEAD_DIM)
            o_carry.append(_nt_dot(split_dirs(qd[:, cols]).astype(BF16),
                                   state_ref[gi].astype(BF16)))
    kd = k * jnp.exp2(per_dir(btot_f, btot_b) - bp)
    dec_f = jnp.exp2(btot_f)
    dec_b = jnp.exp2(btot_b)
    for gi in range(groups):
        cols = slice(gi * HEAD_DIM, (gi + 1) * HEAD_DIM)
        dec = jnp.concatenate([dec_f[:, cols], dec_b[:, cols]], axis=1)
        vt = v[:, cols].astype(F32).T.astype(BF16)
        state_ref[gi] = state_ref[gi] * dec + jnp.dot(
            vt, split_dirs(kd[:, cols]).astype(BF16), preferred_element_type=F32)

    if emit_o:
        of_ref, ob_ref = out_refs
        scores = [None] * groups
        q_bf, k_bf = q.astype(BF16), k.astype(BF16)
        for lv in range(levels + 1):
            if lv == levels:
                xq, yk = q_bf, k_bf
            else:
                if lv < n_fine:
                    dl = d_scr[lv * rows2:(lv + 1) * rows2, :]
                else:
                    dl = _coarse_exponent(bp, 1 << lv, length)
                e = jnp.exp2(dl).astype(BF16)
                xq, yk = q_bf * e, k_bf * e
            keep = masks_ref[lv] > 0.5
            for gi in range(groups):
                cols = slice(gi * HEAD_DIM, (gi + 1) * HEAD_DIM)
                p = _nt_dot(xq[:, cols], yk[:, cols])
                scores[gi] = jnp.where(keep, p, 0.0 if scores[gi] is None else scores[gi])
        for gi in range(groups):
            cols = slice(gi * HEAD_DIM, (gi + 1) * HEAD_DIM)
            out_cols = slice((gi % heads) * HEAD_DIM, (gi % heads + 1) * HEAD_DIM)
            o = o_carry[gi] + jnp.dot(scores[gi].astype(BF16), v[:, cols],
                                      preferred_element_type=F32)
            of_ref[gi // heads, :, out_cols] = o[:length].astype(of_ref.dtype)
            ob_ref[gi // heads, :, out_cols] = o[length:].astype(ob_ref.dtype)

    if not emit_o:
        @pl.when(c == n_chunks - 1)
        def _():
            out_refs[0][...] = state_ref[...].reshape(out_refs[0].shape)


def _scan(proj, col_q, col_zf, col_zb, col_i, tab_f, tab_b, init, emit_o, name):
    bsz, n, _ = proj.shape
    width = tab_f.shape[1]
    heads = width // HEAD_DIM
    length = SCAN_CHUNK
    n_chunks = n // length
    sums_np, masks_np, n_fine = _scan_constants(length)
    sums = jnp.asarray(sums_np, BF16)
    masks = jnp.asarray(masks_np, F32)
    d_rows = sums.shape[0] if emit_o else 2 * length
    ns = SCAN_SAMPLES

    def fwd(col):
        return pl.BlockSpec((ns, length, width), lambda b, c: (b, c, col))

    def bwd(col):
        return pl.BlockSpec((ns, length, width), lambda b, c: (b, n_chunks - 1 - c, col))

    def whole(a):
        nd = a.ndim
        return pl.BlockSpec(a.shape, lambda b, c: (0,) * nd)

    in_specs = [fwd(col_q), fwd(col_zf), fwd(col_i), bwd(col_q), bwd(col_zb), bwd(col_i),
                whole(tab_f), whole(tab_b), whole(sums), whole(masks)]
    args = [proj, proj, proj, proj, proj, proj, tab_f, tab_b, sums, masks]
    state_shape = (heads, HEAD_DIM, 2 * HEAD_DIM)
    if init is not None:
        in_specs.append(pl.BlockSpec((ns,) + state_shape, lambda b, c: (b, 0, 0, 0)))
        args.append(init)
    if emit_o:
        out_spec = [pl.BlockSpec((ns, length, width), lambda b, c: (b, c, 0)),
                    pl.BlockSpec((ns, length, width), lambda b, c: (b, n_chunks - 1 - c, 0))]
        out_shape = [jax.ShapeDtypeStruct((bsz, n, width), BF16)] * 2
    else:
        out_spec = pl.BlockSpec((ns,) + state_shape, lambda b, c: (b, 0, 0, 0))
        out_shape = jax.ShapeDtypeStruct((bsz,) + state_shape, F32)
    return pl.pallas_call(
        functools.partial(_scan_kernel, heads=heads, n_fine=n_fine, emit_o=emit_o,
                          has_init=init is not None),
        grid=(bsz // ns, n_chunks),
        in_specs=in_specs,
        out_specs=out_spec,
        out_shape=out_shape,
        scratch_shapes=[pltpu.VMEM((ns * heads,) + state_shape[1:], F32),
                        pltpu.VMEM((d_rows, ns * width), F32)],
        compiler_params=_params(("arbitrary", "arbitrary")),
        name=name,
    )(*args)


def _postmix_kernel(u_ref, v_ref, g_ref, of_ref, ob_ref, x_ref, g1_ref, sh2_ref, sc2_ref,
                    ln_ref, ws_ref, bs_ref, hn_ref, wout_ref, n2_ref, rw_ref, rb_ref,
                    x1_ref, h2_ref, lg_ref, y_scr):
    rows = x_ref.shape[1]
    heads = ln_ref.shape[0]
    sgu_w = heads * HEAD_DIM
    u = _gelu_tanh(u_ref[0].astype(F32))
    v = _gelu_tanh(v_ref[0].astype(F32))
    for ch in range(rows // SGU_CHUNK):
        rs = slice(ch * SGU_CHUNK, (ch + 1) * SGU_CHUNK)
        for h in range(heads):
            cols = slice(h * HEAD_DIM, (h + 1) * HEAD_DIM)
            vh = v[rs, cols]
            mu = jnp.mean(vh, axis=-1, keepdims=True)
            var = jnp.mean(jnp.square(vh - mu), axis=-1, keepdims=True)
            vn = (vh - mu) * lax.rsqrt(var + EPS) * ln_ref[h:h + 1, :]
            z = jnp.dot(ws_ref[h], vn.astype(BF16), preferred_element_type=F32) + bs_ref[h]
            y_scr[rs, cols] = (u[rs, cols] * z).astype(BF16)
    o = of_ref[0].astype(F32) + ob_ref[0].astype(F32)
    gate = g_ref[0].astype(F32)
    gate = gate * _sigmoid(gate)
    for h in range(o.shape[1] // HEAD_DIM):
        cols = slice(h * HEAD_DIM, (h + 1) * HEAD_DIM)
        oh = o[:, cols]
        on = oh * lax.rsqrt(jnp.mean(oh * oh, axis=-1, keepdims=True) + EPS)
        y_scr[:, sgu_w + h * HEAD_DIM:sgu_w + (h + 1) * HEAD_DIM] = (
            on * hn_ref[:, cols] * gate[:, cols]).astype(BF16)
    y = jnp.dot(y_scr[...], wout_ref[...], preferred_element_type=F32)
    x1 = x_ref[0] + g1_ref[0] * y
    x1_ref[0] = x1
    h2 = _rms_modulated(x1, n2_ref[...], sc2_ref[0], sh2_ref[0])
    h2_ref[...] = h2
    n_exp = rb_ref.shape[0]
    h_hi = h2.astype(BF16)
    h_lo = (h2 - h_hi.astype(F32)).astype(BF16)
    part = _nt_dot(rw_ref[...], h_hi)
    lg = part[:n_exp] + part[n_exp:] + _nt_dot(rw_ref[:n_exp, :], h_lo) + rb_ref[...]
    for s in range(rows // TOKEN_TILE):
        lg_ref[s] = lg[:, s * TOKEN_TILE:(s + 1) * TOKEN_TILE]


def _postmix(proj, o_fwd, o_bwd, x, mod3, sgu_ln, sgu_w_bf16, sgu_b3, hgrn_norm, w_out_bf16,
             norm2, router_w, router_b):
    bsz, n, d = x.shape
    width = o_fwd.shape[2]
    rows = POSTMIX_ROWS
    n_exp = router_b.shape[0]
    nj = n // rows
    sub = rows // TOKEN_TILE

    def colgroup(col):
        return pl.BlockSpec((1, rows, width), lambda b, j: (b, j, col))

    def modrow(idx):
        return pl.BlockSpec((1, 1, d), lambda b, j: (b * 6 + idx, 0, 0))

    def whole(a):
        nd = a.ndim
        return pl.BlockSpec(a.shape, lambda b, j: (0,) * nd)

    return pl.pallas_call(
        _postmix_kernel,
        grid=(bsz, nj),
        in_specs=[colgroup(0), colgroup(1), colgroup(6), colgroup(0), colgroup(0),
                  pl.BlockSpec((1, rows, d), lambda b, j: (b, j, 0)),
                  modrow(2), modrow(3), modrow(4),
                  whole(sgu_ln), whole(sgu_w_bf16), whole(sgu_b3), whole(hgrn_norm),
                  whole(w_out_bf16), whole(norm2), whole(router_w), whole(router_b)],
        out_specs=[pl.BlockSpec((1, rows, d), lambda b, j: (b, j, 0)),
                   pl.BlockSpec((rows, d), lambda b, j: (b * nj + j, 0)),
                   pl.BlockSpec((sub, n_exp, TOKEN_TILE), lambda b, j: (b * nj + j, 0, 0))],
        out_shape=[jax.ShapeDtypeStruct((bsz, n, d), F32),
                   jax.ShapeDtypeStruct((bsz * n, d), F32),
                   jax.ShapeDtypeStruct((bsz * n // TOKEN_TILE, n_exp, TOKEN_TILE), F32)],
        scratch_shapes=[pltpu.VMEM((rows, d), BF16)],
        compiler_params=_params(("arbitrary", "arbitrary")),
        name="postmix",
    )(proj, proj, proj, o_fwd, o_bwd, x, mod3, mod3, mod3, sgu_ln, sgu_w_bf16, sgu_b3, hgrn_norm,
      w_out_bf16, norm2, router_w, router_b)


def _route_kernel(lt_ref, upper_ref, ones_ref, ltri_ref,
                  dest_ref, gate_ref, pads_ref, pos_scr, eidx_scr, *, block_rows):
    n_tiles, n_exp, lanes = lt_ref.shape
    iota_e = lax.broadcasted_iota(jnp.int32, (n_exp, lanes), 0).astype(F32)
    group = ROUTE_GROUP
    wide_iota = jnp.concatenate([iota_e] * group, axis=1)

    def part(a, t):
        return a[:, t * lanes:(t + 1) * lanes]

    def tile_body(jg, carry):
        j0 = jg * group
        l = jnp.concatenate([lt_ref[j0 + t] for t in range(group)], axis=1)
        picks, vals = [], []
        for kk in range(TOP_K):
            m = jnp.max(l, axis=0, keepdims=True)
            idx = jnp.min(jnp.where(l == m, wide_iota, float(n_exp)), axis=0, keepdims=True)
            pick = wide_iota == idx
            picks.append(pick)
            vals.append(m)
            l = jnp.where(pick, -jnp.inf, l)
            for t in range(group):
                eidx_scr[j0 + t, kk:kk + 1, :] = part(idx, t)
        exps = [jnp.exp(v - vals[0]) for v in vals]
        tot = exps[0]
        for e in exps[1:]:
            tot = tot + e
        for kk in range(TOP_K):
            gate = exps[kk] / tot
            for t in range(group):
                gate_ref[j0 + t, kk:kk + 1, :] = part(gate, t)
        chosen = picks[0]
        for p in picks[1:]:
            chosen = chosen | p
        cb = jnp.where(chosen, 1.0, 0.0).astype(BF16)
        for t in range(group):
            cbt = part(cb, t)
            before = carry + jnp.dot(cbt, upper_ref[...], preferred_element_type=F32)
            for kk in range(TOP_K):
                pos_scr[j0 + t, kk:kk + 1, :] = jnp.sum(
                    jnp.where(part(picks[kk], t), before, 0.0), axis=0, keepdims=True)
            carry = carry + jnp.dot(cbt, ones_ref[...], preferred_element_type=F32)
        return carry

    counts = lax.fori_loop(0, n_tiles // group, tile_body, jnp.zeros((n_exp, lanes), F32))
    n_blk = jnp.ceil(counts * (1.0 / block_rows))
    end_blk = jnp.dot(ltri_ref[...], n_blk.astype(BF16), preferred_element_type=F32)
    start_row = (end_blk - n_blk) * block_rows

    def dest_body(j, _):
        for kk in range(TOP_K):
            idx = eidx_scr[j, kk:kk + 1, :]
            base = jnp.sum(jnp.where(iota_e == idx, start_row, 0.0), axis=0, keepdims=True)
            dest_ref[j, kk:kk + 1, :] = (base + pos_scr[j, kk:kk + 1, :]).astype(jnp.int32)
        return 0

    lax.fori_loop(0, n_tiles, dest_body, 0)

    pads_ref[0] = (start_row + counts).astype(jnp.int32)
    pads_ref[1] = (n_blk * block_rows - counts).astype(jnp.int32)
    pads_ref[2] = end_blk.astype(jnp.int32)
    pads_ref[3] = start_row.astype(jnp.int32)
    pads_ref[4] = n_blk.astype(jnp.int32)


def _route(lt3):
    n_tiles, n_exp, lanes = lt3.shape
    upper = jnp.asarray(np.triu(np.ones((lanes, lanes), np.float32), 1), BF16)
    ones = jnp.ones((lanes, lanes), BF16)
    ltri = jnp.asarray(np.tril(np.ones((n_exp, n_exp), np.float32)), BF16)
    vm = pl.BlockSpec(memory_space=pltpu.VMEM)
    return pl.pallas_call(
        functools.partial(_route_kernel, block_rows=EXPERT_ROWS),
        in_specs=[vm, vm, vm, vm],
        out_specs=[vm, vm, vm],
        out_shape=[jax.ShapeDtypeStruct((n_tiles, TOP_K, lanes), jnp.int32),
                   jax.ShapeDtypeStruct((n_tiles, TOP_K, lanes), F32),
                   jax.ShapeDtypeStruct((5, n_exp, lanes), jnp.int32)],
        scratch_shapes=[pltpu.VMEM((n_tiles, TOP_K, lanes), F32),
                        pltpu.VMEM((n_tiles, TOP_K, lanes), F32)],
        compiler_params=pltpu.CompilerParams(vmem_limit_bytes=VMEM_LIMIT),
        name="route",
    )(lt3, upper, ones, ltri)


def _pad_sizes():
    return [1 << b for b in reversed(range(int(math.log2(EXPERT_ROWS))))]


def _dispatch_kernel(pads_ref, dest_ref, h2_hbm, xs_hbm, zero_scr, stage_scr, ring_scr, sem,
                     load_sem, zsem):
    j = pl.program_id(0)
    n_steps = pl.num_programs(0)
    tile = dest_ref.shape[2]
    n_exp = pads_ref.shape[1]

    def pad_copies(e, wait):
        off = pads_ref[0, e]
        cnt = pads_ref[1, e]
        for sz in _pad_sizes():
            cp = pltpu.make_async_copy(zero_scr.at[pl.ds(0, sz)], xs_hbm.at[pl.ds(off, sz)], zsem)

            @pl.when((cnt & sz) != 0)
            def _():
                if wait:
                    cp.wait()
                else:
                    cp.start()
            off = off + (cnt & sz)

    half = zero_scr.shape[0]
    n_blocks = xs_hbm.shape[0] // (2 * half)
    n_used = pads_ref[2, n_exp - 1]

    def tail_copies(blk, wait):
        for part in range(2):
            cp = pltpu.make_async_copy(
                zero_scr, xs_hbm.at[pl.ds((blk * 2 + part) * half, half)], zsem)
            if wait:
                cp.wait()
            else:
                cp.start()

    @pl.when(j == 0)
    def _():
        zero_scr[...] = jnp.zeros_like(zero_scr)

        def start(e, _):
            pad_copies(e, False)
            return 0
        lax.fori_loop(0, n_exp, start, 0)

        def start_tail(blk, _):
            tail_copies(blk, False)
            return 0
        lax.fori_loop(n_used, n_blocks, start_tail, 0)

    ring = ring_scr.shape[0] // tile

    def load(step):
        slot = step % 2
        return pltpu.make_async_copy(h2_hbm.at[pl.ds(step * tile, tile)],
                                     stage_scr.at[pl.ds(slot * tile, tile)], load_sem.at[slot])

    def finish_rows(step):
        for kk in range(TOP_K):
            pltpu.make_async_copy(ring_scr.at[pl.ds(0, tile)], xs_hbm.at[pl.ds(0, tile)],
                                  sem.at[step % 2]).wait()

    @pl.when(j == 0)
    def _():
        load(0).start()
        load(1).start()

    load(j).wait()
    base = (j % ring) * tile
    staged = stage_scr[pl.ds(pl.multiple_of((j % 2) * tile, tile), tile), :]
    ring_scr[pl.ds(pl.multiple_of(base, tile), tile)] = staged.reshape(tile, 1, staged.shape[1])

    @pl.when(j + 2 < n_steps)
    def _():
        load(j + 2).start()

    for i in range(tile):
        for kk in range(TOP_K):
            pltpu.make_async_copy(ring_scr.at[base + i], xs_hbm.at[dest_ref[0, kk, i]],
                                  sem.at[j % 2]).start(priority=kk % 2)

    @pl.when(j > 0)
    def _():
        finish_rows(j - 1)

    @pl.when(j == n_steps - 1)
    def _():
        finish_rows(j)

        def finish(e, _):
            pad_copies(e, True)
            return 0
        lax.fori_loop(0, n_exp, finish, 0)

        def finish_tail(blk, _):
            tail_copies(blk, True)
            return 0
        lax.fori_loop(n_used, n_blocks, finish_tail, 0)


def _dispatch(pads, dest, h2, n_rows):
    n_tok, d = h2.shape
    n_tiles = dest.shape[0]
    assert n_tiles >= 2
    grid_spec = pltpu.PrefetchScalarGridSpec(
        num_scalar_prefetch=1,
        grid=(n_tiles,),
        in_specs=[pl.BlockSpec((1, TOP_K, TOKEN_TILE), lambda j, p: (j, 0, 0),
                               memory_space=pltpu.SMEM),
                  pl.BlockSpec(memory_space=pl.ANY)],
        out_specs=pl.BlockSpec(memory_space=pl.ANY),
        scratch_shapes=[pltpu.VMEM((EXPERT_ROWS // 2, 1, d), F32),
                        pltpu.VMEM((2 * TOKEN_TILE, d), F32),
                        pltpu.VMEM((DISPATCH_RING * TOKEN_TILE, 1, d), F32),
                        pltpu.SemaphoreType.DMA((2,)),
                        pltpu.SemaphoreType.DMA((2,)),
                        pltpu.SemaphoreType.DMA(())],
    )
    return pl.pallas_call(
        _dispatch_kernel,
        grid_spec=grid_spec,
        out_shape=jax.ShapeDtypeStruct((n_rows, 1, d), F32),
        compiler_params=_params(("arbitrary",)),
        name="dispatch",
    )(pads, dest, h2)


def _experts_kernel(start_ref, nblk_ref, pad_ref, xs_hbm, w1_ref, b1_ref, w2_ref, b2_ref, ys_hbm,
                    w1_scr, w2_scr, x_ring, y_ring, x_scr, out_scr, act_scr, load_sem, store_sem):
    e = pl.program_id(0)
    n_exp = pl.num_programs(0)
    rows = x_scr.shape[0]
    shift = int(math.log2(rows))
    n_blk = nblk_ref[e]
    first = lax.shift_right_logical(start_ref[e], shift)
    n_used = lax.shift_right_logical(start_ref[n_exp - 1], shift) + nblk_ref[n_exp - 1]
    d, two_f = w1_scr.shape
    f = two_f // 2

    def slot_rows(g):
        return pl.ds(pl.multiple_of(jnp.bitwise_and(g, 1) * rows, rows), rows)

    def load(g):
        return pltpu.make_async_copy(xs_hbm.at[pl.ds(g * rows, rows)], x_ring.at[slot_rows(g)],
                                     load_sem.at[jnp.bitwise_and(g, 1)])

    def store(g):
        return pltpu.make_async_copy(y_ring.at[slot_rows(g)], ys_hbm.at[pl.ds(g * rows, rows)],
                                     store_sem.at[jnp.bitwise_and(g, 1)])

    def finish(p):
        y_ring[slot_rows(p)] = out_scr[...].reshape((rows, 1, d))
        store(p).start()

    @pl.when((e == 0) & (n_used > 0))
    def _():
        load(0).start()
        out_scr[...] = jnp.zeros_like(out_scr)

    @pl.when(n_blk > 0)
    def _():
        step = 64

        def cast1(r, _):
            rs = pl.ds(pl.multiple_of(r * step, step), step)
            w1_scr[rs, :] = w1_ref[0, rs, :].astype(BF16)
            return 0
        lax.fori_loop(0, d // step, cast1, 0)

        def cast2(r, _):
            rs = pl.ds(pl.multiple_of(r * step, step), step)
            w2_scr[rs, :] = w2_ref[0, rs, :].astype(BF16)
            return 0
        lax.fori_loop(0, f // step, cast2, 0)

    def block(i, _):
        g = first + i
        load(g).wait()

        @pl.when(g + 1 < n_used)
        def _():
            load(g + 1).start()

        prev = jnp.maximum(g - 1, 0)

        @pl.when((g == 1) | (g >= 3))
        def _():
            store(prev).wait()

        def mlp(m):
            base = pl.multiple_of(jnp.bitwise_and(g, 1) * rows, rows)
            x_scr[0:m, :] = x_ring[pl.ds(base, m)].reshape((m, d))
            xb = x_scr[0:m, :].astype(BF16)
            finish(prev)
            width = MATMUL_COLS
            for n in range(f // width):
                cg = slice(n * width, (n + 1) * width)
                cu = slice(f + n * width, f + (n + 1) * width)
                gate = jnp.dot(xb, w1_scr[:, cg], preferred_element_type=F32) + b1_ref[0, :, cg]
                up = jnp.dot(xb, w1_scr[:, cu], preferred_element_type=F32) + b1_ref[0, :, cu]
                gate = jnp.minimum(gate, SWIGLU_LIMIT)
                up = jnp.clip(up, -SWIGLU_LIMIT, SWIGLU_LIMIT)
                act_scr[0:m, cg] = (gate * _sigmoid(SWIGLU_ALPHA * gate)
                                    * (up + 1.0)).astype(BF16)
            out_scr[0:m, :] = jnp.dot(act_scr[0:m, :], w2_scr[...],
                                      preferred_element_type=F32) + b2_ref[0]

        quarter = rows // EXPERT_PARTS
        parts = jnp.where(i == n_blk - 1,
                          EXPERT_PARTS - lax.div(pad_ref[e], quarter), EXPERT_PARTS)
        for n_parts in range(1, EXPERT_PARTS + 1):
            @pl.when(parts == n_parts)
            def _(n_parts=n_parts):
                mlp(n_parts * quarter)
        return 0
    lax.fori_loop(0, n_blk, block, 0)

    @pl.when((e == n_exp - 1) & (n_used > 0))
    def _():
        last = n_used - 1

        @pl.when(last != 1)
        def _():
            store(last).wait()

        finish(last)
        store(last).wait()

        @pl.when(last >= 1)
        def _():
            store(last - 1).wait()


def _experts(start_row, n_blk, n_pad, xs, w1, b1, w2, b2):
    n_rows, _, d = xs.shape
    n_exp, _, two_f = w1.shape
    f = two_f // 2
    rows = EXPERT_ROWS
    grid_spec = pltpu.PrefetchScalarGridSpec(
        num_scalar_prefetch=3,
        grid=(n_exp,),
        in_specs=[pl.BlockSpec(memory_space=pl.ANY),
                  pl.BlockSpec((1, d, two_f), lambda e, st, nb, pd: (e, 0, 0)),
                  pl.BlockSpec((1, 1, two_f), lambda e, st, nb, pd: (e, 0, 0)),
                  pl.BlockSpec((1, f, d), lambda e, st, nb, pd: (e, 0, 0)),
                  pl.BlockSpec((1, 1, d), lambda e, st, nb, pd: (e, 0, 0))],
        out_specs=pl.BlockSpec(memory_space=pl.ANY),
        scratch_shapes=[pltpu.VMEM((d, two_f), BF16),
                        pltpu.VMEM((f, d), BF16)]
        + [pltpu.VMEM((2 * rows, 1, d), F32) for _ in range(2)]
        + [pltpu.VMEM((rows, d), F32),
           pltpu.VMEM((rows, d), F32),
           pltpu.VMEM((rows, f), BF16),
           pltpu.SemaphoreType.DMA((2,)),
           pltpu.SemaphoreType.DMA((2,))],
    )
    return pl.pallas_call(
        _experts_kernel,
        grid_spec=grid_spec,
        out_shape=jax.ShapeDtypeStruct((n_rows, 1, d), F32),
        input_output_aliases={3: 0},
        compiler_params=_params(("arbitrary",)),
        name="experts",
    )(start_row, n_blk, n_pad, xs, w1, b1.reshape(n_exp, 1, two_f), w2, b2.reshape(n_exp, 1, d))


def _combine_kernel(dest_ref, ahead1_ref, ahead2_ref, ys_hbm, gate_ref, x1_ref, g2_ref, fn_ref,
                    o_ref, ring_scr, flat_scr, sem):
    j = pl.program_id(0)
    n_steps = pl.num_programs(0)
    tile = dest_ref.shape[2]
    ring = ring_scr.shape[0] // (TOP_K * tile)

    def slot_of(step):
        return step % ring

    def issue(idx_ref, slot, i):
        for kk in range(TOP_K):
            pltpu.make_async_copy(ys_hbm.at[idx_ref[0, kk, i]],
                                  ring_scr.at[(slot * TOP_K + kk) * tile + i],
                                  sem.at[slot]).start(priority=kk % 2)

    def drain(slot):
        for kk in range(TOP_K):
            pltpu.make_async_copy(ys_hbm.at[pl.ds(0, tile)], ring_scr.at[pl.ds(0, tile)],
                                  sem.at[slot]).wait()

    @pl.when(j == 0)
    def _():
        for idx_ref, slot in ((dest_ref, 0), (ahead1_ref, 1)):
            def body(i, _, idx_ref=idx_ref, slot=slot):
                issue(idx_ref, slot, i)
                return 0
            lax.fori_loop(0, tile, body, 0)

    slot = slot_of(j)
    drain(slot)
    ahead_slot = slot_of(j + 2)
    for i in range(tile):
        issue(ahead2_ref, ahead_slot, i)
    y = None
    for kk in range(TOP_K):
        rows = pl.ds(pl.multiple_of((slot * TOP_K + kk) * tile, tile), tile)
        flat_scr[...] = ring_scr[rows].reshape(flat_scr.shape)
        term = gate_ref[:, kk:kk + 1] * flat_scr[...]
        y = term if y is None else y + term
    x2 = x1_ref[...] + g2_ref[0] * y
    o_ref[...] = _rms(x2, fn_ref[...])

    @pl.when(j == n_steps - 1)
    def _():
        drain(slot_of(j + 1))
        drain(slot_of(j + 2))


def _combine(dest, ys, gates_tok, x1, mod3, final_norm, seq):
    n_tok, d = x1.shape
    n_tiles = dest.shape[0]
    tiles_per_sample = seq // TOKEN_TILE
    return pl.pallas_call(
        _combine_kernel,
        grid=(n_tiles,),
        in_specs=[pl.BlockSpec((1, TOP_K, TOKEN_TILE), lambda j: (j, 0, 0),
                               memory_space=pltpu.SMEM),
                  pl.BlockSpec((1, TOP_K, TOKEN_TILE),
                               lambda j: (jnp.minimum(j + 1, n_tiles - 1), 0, 0),
                               memory_space=pltpu.SMEM),
                  pl.BlockSpec((1, TOP_K, TOKEN_TILE),
                               lambda j: (jnp.minimum(j + 2, n_tiles - 1), 0, 0),
                               memory_space=pltpu.SMEM),
                  pl.BlockSpec(memory_space=pl.ANY),
                  pl.BlockSpec((TOKEN_TILE, TOP_K), lambda j: (j, 0)),
                  pl.BlockSpec((TOKEN_TILE, d), lambda j: (j, 0)),
                  pl.BlockSpec((1, 1, d), lambda j: ((j // tiles_per_sample) * 6 + 5, 0, 0)),
                  pl.BlockSpec((1, d), lambda j: (0, 0))],
        out_specs=pl.BlockSpec((TOKEN_TILE, d), lambda j: (j, 0)),
        out_shape=jax.ShapeDtypeStruct((n_tok, d), F32),
        scratch_shapes=[pltpu.VMEM((COMBINE_RING * TOP_K * TOKEN_TILE, 1, d), F32),
                        pltpu.VMEM((TOKEN_TILE, d), F32),
                        pltpu.SemaphoreType.DMA((COMBINE_RING,))],
        compiler_params=_params(("arbitrary",)),
        name="combine",
    )(dest, dest, dest, ys, gates_tok, x1, mod3, final_norm)


def kernel(x, c, ctx, c_ctx, w_mod, b_mod, norm1, w_in, sgu_ln, sgu_w, sgu_b, lb_fwd, lb_bwd,
           hgrn_norm, w_out, norm2, router_w, router_b, w1, b1, w2, b2, final_norm):
    depth = w_mod.shape[0]
    assert depth == 1, "single-layer block only"
    bsz, seq, d = x.shape
    assert bsz + 1 <= MOD_ROWS
    width = w_in.shape[2] // N_PROJ
    n_exp = router_w.shape[2]
    n_tok = bsz * seq

    cc = jnp.zeros((MOD_ROWS, d), F32).at[:bsz].set(c).at[bsz].set(c_ctx)
    mod = _modulation(cc, w_mod[0], b_mod[0])
    mod3 = mod.reshape(MOD_ROWS * 6, 1, d)

    w_in_b = w_in[0].astype(BF16)
    norm1_l = norm1[0].reshape(1, d)
    n_ctx = ctx.shape[1]
    proj_ctx = _inproj(ctx.reshape(1, bsz * n_ctx, d), mod3, lambda b: bsz, norm1_l,
                       w_in_b[:, 2 * width:6 * width], INPROJ_ROWS,
                       "inproj_ctx").reshape(bsz, n_ctx, 4 * width)
    states = _scan(proj_ctx, 0, 1, 2, 3, lb_fwd, lb_bwd, None, False, "scan_ctx")
    proj = _inproj(x, mod3, lambda b: b, norm1_l, w_in_b, INPROJ_ROWS, "inproj")
    o_fwd, o_bwd = _scan(proj, 2, 3, 4, 5, lb_fwd, lb_bwd, states, True, "scan")

    rw_t = router_w[0].T
    rw_hi = rw_t.astype(BF16)
    rw_parts = jnp.concatenate([rw_hi, (rw_t - rw_hi.astype(F32)).astype(BF16)], axis=0)
    x1, h2, lt3 = _postmix(
        proj, o_fwd, o_bwd, x, mod3, sgu_ln[0], sgu_w[0].astype(BF16),
        sgu_b[0].reshape(sgu_b.shape[1], sgu_b.shape[2], 1), hgrn_norm[0].reshape(1, -1),
        w_out[0].astype(BF16), norm2[0].reshape(1, d), rw_parts, router_b[0].reshape(n_exp, 1))

    n_blocks = -(-(n_tok * TOP_K + n_exp * (EXPERT_ROWS - 1)) // EXPERT_ROWS)
    dest, gates, pads = _route(lt3)
    tables = pads[:, :, 0]

    xs = _dispatch(tables[:3], dest, h2, n_blocks * EXPERT_ROWS)
    ys = _experts(tables[3], tables[4], tables[1], xs, w1[0], b1[0], w2[0], b2[0])
    gates_tok = gates.transpose(0, 2, 1).reshape(n_tok, TOP_K)
    out = _combine(dest, ys, gates_tok, x1.reshape(n_tok, d), mod3, final_norm.reshape(1, d), seq)
    return out.reshape(bsz, seq, d)
```
